```python
import math
import jax
import jax.numpy as jnp
from jax import lax
import numpy as np

D_MODEL = 1024
BATCH = 32
SEQ = 2048
DEPTH = 2

GRID_W = 64
CTX_LEN = 256
N_EVEN = (DEPTH + 1) // 2
N_ODD = DEPTH // 2
N_MOD = 6
FNET_GROUPS = 4
FNET_GROUP_DIM = 128
FNET_WIDTH = FNET_GROUPS * FNET_GROUP_DIM
NA_HEADS = 8
NA_HEAD_DIM = 64
NA_WIDTH = NA_HEADS * NA_HEAD_DIM
NA_KH_MAX = 8
NA_KW = 16
EVEN_IN_WIDTH = FNET_WIDTH + 3 * NA_WIDTH
EVEN_MIX_WIDTH = FNET_WIDTH + NA_WIDTH
GQA_Q_HEADS = 12
GQA_KV_HEADS = 3
GQA_HEAD_DIM = 64
GQA_Q_WIDTH = GQA_Q_HEADS * GQA_HEAD_DIM
GQA_KV_WIDTH = GQA_KV_HEADS * GQA_HEAD_DIM
Q_BLOCK = 128
ROPE_THETA = 10000.0
S5_GROUPS = 16
S5_GROUP_DIM = 16
S5_WIDTH = S5_GROUPS * S5_GROUP_DIM
S5_STATE = 64
S5_DT_MIN = 0.001
S5_DT_MAX = 0.1
ODD_IN_WIDTH = GQA_Q_WIDTH + 2 * GQA_KV_WIDTH + S5_WIDTH
ODD_MIX_WIDTH = GQA_Q_WIDTH + S5_WIDTH
D_FF_DENSE = 2816
N_EXPERTS = 8
TOP_K = 2
D_FF_EXPERT = 3584

EPS = 1e-6
NEG_INF = -1e30

kernel_name = 'hybrid_fourier_natten_gqa_s5_moe_dit'


def rms_norm(x, gain):
    xf = x.astype(jnp.float32)
    y = xf * lax.rsqrt(jnp.mean(xf * xf, axis=-1, keepdims=True) + EPS)
    return (y * gain.astype(jnp.float32)).astype(x.dtype)


def ada_modulation(cond, w_mod, b_mod):
    m = jax.nn.silu(cond) @ w_mod + b_mod
    return jnp.split(m, N_MOD, axis=-1)


def modulate(h, shift, scale):
    return h * (1.0 + scale) + shift


def swiglu(h, w1, w3, w2):
    return (jax.nn.silu(h @ w1) * (h @ w3)) @ w2


def split_heads(t, n_heads, head_dim):
    return t.reshape(t.shape[0], t.shape[1], n_heads, head_dim)


def axial_rope(n_tokens, head_dim):
    t = jnp.arange(n_tokens)
    row = (t // GRID_W).astype(jnp.float32)
    col = (t % GRID_W).astype(jnp.float32)
    n_pairs_axis = head_dim // 4
    freqs = ROPE_THETA ** (-jnp.arange(n_pairs_axis, dtype=jnp.float32) / n_pairs_axis)
    ang = jnp.concatenate([row[:, None] * freqs, col[:, None] * freqs], axis=-1)
    return jnp.cos(ang)[None, :, None, :], jnp.sin(ang)[None, :, None, :]


def apply_rope(x, cos, sin):
    xf = x.astype(jnp.float32)
    x1, x2 = xf[..., 0::2], xf[..., 1::2]
    out = jnp.stack([x1 * cos - x2 * sin, x1 * sin + x2 * cos], axis=-1)
    return out.reshape(x.shape).astype(x.dtype)


def fourier_mix(u):
    b, n, _ = u.shape
    ug = u.astype(jnp.float32).reshape(b, n, FNET_GROUPS, FNET_GROUP_DIM)
    y = jnp.fft.fft2(ug, axes=(1, 3), norm='ortho').real
    return y.reshape(b, n, FNET_WIDTH).astype(u.dtype)


def context_attention(q, k, v):
    b, lq, hq, dh = q.shape
    hkv = k.shape[2]
    qg = q.reshape(b, lq, hkv, hq // hkv, dh)
    s = jnp.einsum('bqhgd,bkhd->bhgqk', qg, k).astype(jnp.float32) * dh ** -0.5
    p = jax.nn.softmax(s, axis=-1).astype(v.dtype)
    o = jnp.einsum('bhgqk,bkhd->bqhgd', p, v)
    return o.reshape(b, lq, hq * dh)


def block_sweep_attention(q, k_all, v_all):
    b, n, hq, dh = q.shape
    hkv = k_all.shape[2]
    g = hq // hkv
    nb = n // Q_BLOCK
    qb = q.reshape(b, nb, Q_BLOCK, hkv, g, dh).transpose(1, 0, 2, 3, 4, 5)

    def one_block(q_blk):
        s = jnp.einsum('bqhgd,bkhd->bhgqk', q_blk, k_all).astype(jnp.float32) * dh ** -0.5
        p = jax.nn.softmax(s, axis=-1).astype(v_all.dtype)
        return jnp.einsum('bhgqk,bkhd->bqhgd', p, v_all)

    o = lax.map(one_block, qb)
    return o.transpose(1, 0, 2, 3, 4, 5).reshape(b, n, hq * dh)


def neighbourhood_attention(q, k, v, k_ctx, v_ctx, rpb):
    b, n, h, dh = q.shape
    rows = n // GRID_W
    kh = min(NA_KH_MAX, rows)
    qg = q.reshape(b, rows, GRID_W, h, dh)
    kg = k.reshape(b, rows, GRID_W, h, dh)
    vg = v.reshape(b, rows, GRID_W, h, dh)
    qcol = jnp.arange(GRID_W)
    kcol = jnp.arange(GRID_W)
    col_start = jnp.clip(qcol - NA_KW // 2, 0, GRID_W - NA_KW)
    col_in = (kcol[None, :] >= col_start[:, None]) & (kcol[None, :] < col_start[:, None] + NA_KW)
    mask = jnp.broadcast_to(col_in[:, None, :], (GRID_W, kh, GRID_W)).reshape(GRID_W, kh * GRID_W)
    dc_idx = jnp.clip(kcol[None, :] - qcol[:, None] + NA_KW - 1, 0, 2 * NA_KW - 2)
    scale = dh ** -0.5

    def one_row(r):
        r0 = jnp.clip(r - kh // 2, 0, rows - kh)
        kb = lax.dynamic_slice_in_dim(kg, r0, kh, axis=1).reshape(b, kh * GRID_W, h, dh)
        vb = lax.dynamic_slice_in_dim(vg, r0, kh, axis=1).reshape(b, kh * GRID_W, h, dh)
        qr = lax.dynamic_index_in_dim(qg, r, axis=1, keepdims=False)
        dr_idx = r0 + jnp.arange(kh) - r + NA_KH_MAX - 1
        bias = rpb[:, dr_idx][:, :, dc_idx]
        bias = bias.transpose(0, 2, 1, 3).reshape(h, GRID_W, kh * GRID_W).astype(jnp.float32)
        s_loc = jnp.einsum('bqhd,bkhd->bhqk', qr, kb).astype(jnp.float32) * scale + bias
        s_loc = jnp.where(mask, s_loc, NEG_INF)
        s_ctx = jnp.einsum('bqhd,bkhd->bhqk', qr, k_ctx).astype(jnp.float32) * scale
        p = jax.nn.softmax(jnp.concatenate([s_loc, s_ctx], axis=-1), axis=-1).astype(v.dtype)
        n_loc = kh * GRID_W
        return (jnp.einsum('bhqk,bkhd->bqhd', p[..., :n_loc], vb)
                + jnp.einsum('bhqk,bkhd->bqhd', p[..., n_loc:], v_ctx))

    o = lax.map(one_row, jnp.arange(rows))
    return o.transpose(1, 0, 2, 3, 4).reshape(b, n, h * dh)


def _linear_combine(left, right):
    a_l, b_l = left
    a_r, b_r = right
    return a_r * a_l, a_r * b_l + b_r


def linear_recurrence(lam_bar, bu, s0):
    if s0 is not None:
        bu = bu.at[:, 0].add(lam_bar * s0)
    a = jnp.broadcast_to(lam_bar, bu.shape)
    _, states = lax.associative_scan(_linear_combine, (a, bu), axis=1)
    return states


def s5_direction(ug_lat, ug_ctx, a_re, a_im, log_step, b_re, b_im, c_re, c_im, reverse, with_ctx_out):
    lam = lax.complex(a_re.astype(jnp.float32), a_im.astype(jnp.float32))
    step = jnp.exp(log_step.astype(jnp.float32))[:, None]
    lam_bar = jnp.exp(lam * step)
    b_mat = lax.complex(b_re.astype(jnp.float32), b_im.astype(jnp.float32))
    b_bar = ((lam_bar - 1.0) / lam)[..., None] * b_mat
    c_mat = lax.complex(c_re.astype(jnp.float32), c_im.astype(jnp.float32))

    def orient(z):
        return jnp.flip(z, axis=1) if reverse else z

    bu_ctx = jnp.einsum('blgn,gpn->blgp', orient(ug_ctx).astype(jnp.complex64), b_bar)
    st_ctx = linear_recurrence(lam_bar, bu_ctx, None)
    bu_lat = jnp.einsum('blgn,gpn->blgp', orient(ug_lat).astype(jnp.complex64), b_bar)
    st_lat = linear_recurrence(lam_bar, bu_lat, st_ctx[:, -1])
    y_lat = orient(jnp.einsum('blgp,gnp->blgn', st_lat, c_mat).real)
    y_ctx = orient(jnp.einsum('blgp,gnp->blgn', st_ctx, c_mat).real) if with_ctx_out else None
    return y_lat, y_ctx


def s5_mixer(u_lat, u_ctx, a_re, a_im, log_step, b_re, b_im, c_re, c_im, d_skip, glu_w, glu_b, with_ctx_out):
    def groups(u):
        return u.astype(jnp.float32).reshape(u.shape[0], u.shape[1], S5_GROUPS, S5_GROUP_DIM)

    ug_lat, ug_ctx = groups(u_lat), groups(u_ctx)
    y_lat, y_ctx = None, None
    for direction, reverse in ((0, False), (1, True)):
        yl, yc = s5_direction(ug_lat, ug_ctx, a_re[direction], a_im[direction], log_step[direction],
                              b_re[direction], b_im[direction], c_re[direction], c_im[direction],
                              reverse, with_ctx_out)
        y_lat = yl if y_lat is None else y_lat + yl
        if with_ctx_out:
            y_ctx = yc if y_ctx is None else y_ctx + yc

    def readout(u, y):
        y = y.reshape(u.shape) + d_skip.astype(jnp.float32) * u.astype(jnp.float32)
        y = jax.nn.gelu(y).astype(u.dtype)
        return y * jax.nn.sigmoid(y @ glu_w + glu_b)

    out_ctx = readout(u_ctx, y_ctx) if with_ctx_out else None
    return readout(u_lat, y_lat), out_ctx


def moe_swiglu(h, router_w, router_b, w1, w3, w2):
    shape = h.shape
    t = h.reshape(-1, shape[-1])
    logits = (t @ router_w + router_b).astype(jnp.float32)
    top_val, top_idx = lax.top_k(logits, TOP_K)
    top_w = jax.nn.softmax(top_val, axis=-1)
    gates = jnp.einsum('tk,tke->te', top_w,
                       jax.nn.one_hot(top_idx, N_EXPERTS, dtype=jnp.float32)).astype(t.dtype)
    out = jnp.zeros_like(t)
    for e in range(N_EXPERTS):
        out = out + gates[:, e:e + 1] * swiglu(t, w1[e], w3[e], w2[e])
    return out.reshape(shape)


def even_layer(x, xc, c, c_ctx, mod_w, mod_b, norm_mix, norm_ffn, w_in, q_gain, k_gain, rpb, w_out,
               w1, w3, w2, update_ctx):
    sh1, sc1, g1, sh2, sc2, g2 = ada_modulation(c[:, None, :], mod_w, mod_b)
    csh1, csc1, cg1, csh2, csc2, cg2 = ada_modulation(c_ctx, mod_w, mod_b)
    cuts = [FNET_WIDTH, FNET_WIDTH + NA_WIDTH, FNET_WIDTH + 2 * NA_WIDTH]
    h = modulate(rms_norm(x, norm_mix), sh1, sc1) @ w_in
    hc = modulate(rms_norm(xc, norm_mix), csh1, csc1) @ w_in
    u, q, k, v = jnp.split(h, cuts, axis=-1)
    uc, qc, kc, vc = jnp.split(hc, cuts, axis=-1)
    q = rms_norm(split_heads(q, NA_HEADS, NA_HEAD_DIM), q_gain)
    k = rms_norm(split_heads(k, NA_HEADS, NA_HEAD_DIM), k_gain)
    v = split_heads(v, NA_HEADS, NA_HEAD_DIM)
    kc = rms_norm(split_heads(kc, NA_HEADS, NA_HEAD_DIM), k_gain)
    vc = split_heads(vc, NA_HEADS, NA_HEAD_DIM)
    o_lat = jnp.concatenate([fourier_mix(u), neighbourhood_attention(q, k, v, kc, vc, rpb)], axis=-1)
    x = x + g1 * (o_lat @ w_out)
    if update_ctx:
        qc = rms_norm(split_heads(qc, NA_HEADS, NA_HEAD_DIM), q_gain)
        o_ctx = jnp.concatenate([fourier_mix(uc), context_attention(qc, kc, vc)], axis=-1)
        xc = xc + cg1 * (o_ctx @ w_out)
    x = x + g2 * swiglu(modulate(rms_norm(x, norm_ffn), sh2, sc2), w1, w3, w2)
    if update_ctx:
        xc = xc + cg2 * swiglu(modulate(rms_norm(xc, norm_ffn), csh2, csc2), w1, w3, w2)
    return x, xc


def odd_layer(x, xc, c, c_ctx, mod_w, mod_b, norm_mix, norm_ffn, w_in, q_gain, k_gain,
              a_re, a_im, log_step, b_re, b_im, c_re, c_im, d_skip, glu_w, glu_b, w_out,
              router_w, router_b, ew1, ew3, ew2, update_ctx):
    sh1, sc1, g1, sh2, sc2, g2 = ada_modulation(c[:, None, :], mod_w, mod_b)
    csh1, csc1, cg1, csh2, csc2, cg2 = ada_modulation(c_ctx, mod_w, mod_b)
    cuts = [GQA_Q_WIDTH, GQA_Q_WIDTH + GQA_KV_WIDTH, GQA_Q_WIDTH + 2 * GQA_KV_WIDTH]
    h = modulate(rms_norm(x, norm_mix), sh1, sc1) @ w_in
    hc = modulate(rms_norm(xc, norm_mix), csh1, csc1) @ w_in
    q, k, v, u = jnp.split(h, cuts, axis=-1)
    qc, kc, vc, uc = jnp.split(hc, cuts, axis=-1)
    cos, sin = axial_rope(x.shape[1], GQA_HEAD_DIM)
    q = apply_rope(rms_norm(split_heads(q, GQA_Q_HEADS, GQA_HEAD_DIM), q_gain), cos, sin)
    k = apply_rope(rms_norm(split_heads(k, GQA_KV_HEADS, GQA_HEAD_DIM), k_gain), cos, sin)
    v = split_heads(v, GQA_KV_HEADS, GQA_HEAD_DIM)
    kc = rms_norm(split_heads(kc, GQA_KV_HEADS, GQA_HEAD_DIM), k_gain)
    vc = split_heads(vc, GQA_KV_HEADS, GQA_HEAD_DIM)
    k_all = jnp.concatenate([kc, k], axis=1)
    v_all = jnp.concatenate([vc, v], axis=1)
    o_attn = block_sweep_attention(q, k_all, v_all)
    o_ssm, oc_ssm = s5_mixer(u, uc, a_re, a_im, log_step, b_re, b_im, c_re, c_im, d_skip, glu_w, glu_b,
                             update_ctx)
    x = x + g1 * (jnp.concatenate([o_attn, o_ssm], axis=-1) @ w_out)
    if update_ctx:
        qc = rms_norm(split_heads(qc, GQA_Q_HEADS, GQA_HEAD_DIM), q_gain)
        o_ctx = jnp.concatenate([context_attention(qc, kc, vc), oc_ssm], axis=-1)
        xc = xc + cg1 * (o_ctx @ w_out)
    x = x + g2 * moe_swiglu(modulate(rms_norm(x, norm_ffn), sh2, sc2), router_w, router_b, ew1, ew3, ew2)
    if update_ctx:
        xc = xc + cg2 * moe_swiglu(modulate(rms_norm(xc, norm_ffn), csh2, csc2),
                                   router_w, router_b, ew1, ew3, ew2)
    return x, xc


def setup_inputs(seed: int = 0) -> dict:
    key = jax.random.key(seed)
    keys = iter(jax.random.split(key, 64))

    def normal(shape, scale):
        return scale * jax.random.normal(next(keys), shape, jnp.float32)

    def gain(shape):
        return 1.0 + normal(shape, 0.05)

    d = D_MODEL
    ne, no = N_EVEN, N_ODD
    s5_shape = (no, 2, S5_GROUPS, S5_STATE)
    n_idx = jnp.arange(S5_STATE, dtype=jnp.float32)
    log_lo, log_hi = math.log(S5_DT_MIN), math.log(S5_DT_MAX)
    inp = {}
    inp['x'] = normal((BATCH, SEQ, d), 1.0)
    inp['c'] = normal((BATCH, d), 1.0)
    inp['ctx'] = normal((BATCH, CTX_LEN, d), 1.0)
    inp['c_ctx'] = normal((d,), 0.5)
    inp['ev_mod_w'] = normal((ne, d, N_MOD * d), 0.5 * d ** -0.5)
    inp['ev_mod_b'] = normal((ne, N_MOD * d), 0.02)
    inp['ev_norm_mix'] = gain((ne, d))
    inp['ev_norm_ffn'] = gain((ne, d))
    inp['ev_w_in'] = normal((ne, d, EVEN_IN_WIDTH), d ** -0.5)
    inp['ev_q_gain'] = gain((ne, NA_HEAD_DIM))
    inp['ev_k_gain'] = gain((ne, NA_HEAD_DIM))
    inp['ev_rpb'] = normal((ne, NA_HEADS, 2 * NA_KH_MAX - 1, 2 * NA_KW - 1), 0.1)
    inp['ev_w_out'] = normal((ne, EVEN_MIX_WIDTH, d), EVEN_MIX_WIDTH ** -0.5)
    inp['ev_ffn_w1'] = normal((ne, d, D_FF_DENSE), d ** -0.5)
    inp['ev_ffn_w3'] = normal((ne, d, D_FF_DENSE), d ** -0.5)
    inp['ev_ffn_w2'] = normal((ne, D_FF_DENSE, d), D_FF_DENSE ** -0.5)
    inp['od_mod_w'] = normal((no, d, N_MOD * d), 0.5 * d ** -0.5)
    inp['od_mod_b'] = normal((no, N_MOD * d), 0.02)
    inp['od_norm_mix'] = gain((no, d))
    inp['od_norm_ffn'] = gain((no, d))
    inp['od_w_in'] = normal((no, d, ODD_IN_WIDTH), d ** -0.5)
    inp['od_q_gain'] = gain((no, GQA_HEAD_DIM))
    inp['od_k_gain'] = gain((no, GQA_HEAD_DIM))
    inp['od_s5_a_re'] = -0.5 + normal(s5_shape, 0.01)
    inp['od_s5_a_im'] = jnp.pi * n_idx + normal(s5_shape, 0.01)
    inp['od_s5_log_step'] = log_lo + (log_hi - log_lo) * jax.random.uniform(
        next(keys), (no, 2, S5_GROUPS), jnp.float32)
    inp['od_s5_b_re'] = normal((no, 2, S5_GROUPS, S5_STATE, S5_GROUP_DIM), S5_GROUP_DIM ** -0.5)
    inp['od_s5_b_im'] = normal((no, 2, S5_GROUPS, S5_STATE, S5_GROUP_DIM), S5_GROUP_DIM ** -0.5)
    inp['od_s5_c_re'] = normal((no, 2, S5_GROUPS, S5_GROUP_DIM, S5_STATE), S5_STATE ** -0.5)
    inp['od_s5_c_im'] = normal((no, 2, S5_GROUPS, S5_GROUP_DIM, S5_STATE), S5_STATE ** -0.5)
    inp['od_s5_d'] = normal((no, S5_WIDTH), 1.0)
    inp['od_s5_glu_w'] = normal((no, S5_WIDTH, S5_WIDTH), S5_WIDTH ** -0.5)
    inp['od_s5_glu_b'] = normal((no, S5_WIDTH), 0.02)
    inp['od_w_out'] = normal((no, ODD_MIX_WIDTH, d), ODD_MIX_WIDTH ** -0.5)
    inp['od_router_w'] = normal((no, d, N_EXPERTS), d ** -0.5)
    inp['od_router_b'] = normal((no, N_EXPERTS), 0.01)
    inp['od_exp_w1'] = normal((no, N_EXPERTS, d, D_FF_EXPERT), d ** -0.5)
    inp['od_exp_w3'] = normal((no, N_EXPERTS, d, D_FF_EXPERT), d ** -0.5)
    inp['od_exp_w2'] = normal((no, N_EXPERTS, D_FF_EXPERT, d), D_FF_EXPERT ** -0.5)
    return inp


def reference(x, c, ctx, c_ctx,
              ev_mod_w, ev_mod_b, ev_norm_mix, ev_norm_ffn, ev_w_in, ev_q_gain, ev_k_gain, ev_rpb, ev_w_out,
              ev_ffn_w1, ev_ffn_w3, ev_ffn_w2,
              od_mod_w, od_mod_b, od_norm_mix, od_norm_ffn, od_w_in, od_q_gain, od_k_gain,
              od_s5_a_re, od_s5_a_im, od_s5_log_step, od_s5_b_re, od_s5_b_im, od_s5_c_re, od_s5_c_im,
              od_s5_d, od_s5_glu_w, od_s5_glu_b, od_w_out,
              od_router_w, od_router_b, od_exp_w1, od_exp_w3, od_exp_w2):
    xc = ctx
    for layer in range(DEPTH):
        update_ctx = layer < DEPTH - 1
        j = layer // 2
        if layer % 2 == 0:
            x, xc = even_layer(x, xc, c, c_ctx, ev_mod_w[j], ev_mod_b[j], ev_norm_mix[j], ev_norm_ffn[j],
                               ev_w_in[j], ev_q_gain[j], ev_k_gain[j], ev_rpb[j], ev_w_out[j],
                               ev_ffn_w1[j], ev_ffn_w3[j], ev_ffn_w2[j], update_ctx)
        else:
            x, xc = odd_layer(x, xc, c, c_ctx, od_mod_w[j], od_mod_b[j], od_norm_mix[j], od_norm_ffn[j],
                              od_w_in[j], od_q_gain[j], od_k_gain[j],
                              od_s5_a_re[j], od_s5_a_im[j], od_s5_log_step[j], od_s5_b_re[j], od_s5_b_im[j],
                              od_s5_c_re[j], od_s5_c_im[j], od_s5_d[j], od_s5_glu_w[j], od_s5_glu_b[j],
                              od_w_out[j], od_router_w[j], od_router_b[j],
                              od_exp_w1[j], od_exp_w3[j], od_exp_w2[j], update_ctx)
    return x
```

```python
import functools

import numpy as np
import jax
import jax.numpy as jnp
from jax import lax
from jax.experimental import pallas as pl
from jax.experimental.pallas import tpu as pltpu

BF = jnp.bfloat16
F32 = jnp.float32

EPS = 1e-6
NEG_INF = -1e30
GRID_W = 64
LANES = 128
HEAD_DIM = 64
FN_GROUPS, FN_GROUP_DIM = 4, 128
FN_WIDTH = FN_GROUPS * FN_GROUP_DIM
NA_HEADS = 8
NA_WIDTH = NA_HEADS * HEAD_DIM
NA_KH, NA_KW = 8, 16
NA_QROWS = 4
NA_WIN = NA_QROWS + NA_KH - 1
GQ_HEADS, GKV_HEADS = 12, 3
GQ_WIDTH = GQ_HEADS * HEAD_DIM
GKV_PAD = GKV_HEADS * LANES
S5_GROUPS, S5_GROUP_DIM, S5_STATE = 16, 16, 64
S5_WIDTH = S5_GROUPS * S5_GROUP_DIM
S5_NP = S5_GROUPS * S5_STATE
N_EXPERTS = 8
N_MOD = 6
ROPE_THETA = 10000.0
VMEM_LIMIT = 56 * 1024 * 1024

NT_DIMS = (((1,), (1,)), ((), ()))


def _cparams(sem):
    return pltpu.CompilerParams(dimension_semantics=sem, vmem_limit_bytes=VMEM_LIMIT)


def _const_spec(shape):
    nd = len(shape)
    return pl.BlockSpec(shape, lambda *_: (0,) * nd, pipeline_mode=pl.Buffered(1))


def _token_tile(n):
    return 512 if n % 512 == 0 else 256


def _mod_spec(d, row):
    if row is None:
        return pl.BlockSpec((1, N_MOD, d), lambda b, i: (b, 0, 0))
    return pl.BlockSpec((1, N_MOD, d), lambda b, i: (row, 0, 0))


def _norm_mod(x, gain, shift, scale):
    ms = jnp.mean(x * x, axis=-1, keepdims=True)
    return (x * lax.rsqrt(ms + EPS) * gain) * (1.0 + scale) + shift


def _head_norm(t, blockdiag, gain):
    ms = jnp.dot((t * t).astype(BF), blockdiag, preferred_element_type=F32)
    return t * lax.rsqrt(ms + EPS) * gain


def _silu(a):
    return a * jax.nn.sigmoid(a)


def _mod_kernel(c_ref, w_ref, b_ref, o_ref):
    a = _silu(c_ref[...]).astype(BF)
    o_ref[...] = jnp.dot(a, w_ref[...].astype(BF), preferred_element_type=F32) + b_ref[...]


def _modulation(cond, w, b):
    r, d = cond.shape
    n = w.shape[1]
    tn = n // 4
    out = pl.pallas_call(
        _mod_kernel,
        grid=(n // tn,),
        in_specs=[pl.BlockSpec((r, d), lambda j: (0, 0)),
                  pl.BlockSpec((d, tn), lambda j: (0, j)),
                  pl.BlockSpec((1, tn), lambda j: (0, j))],
        out_specs=pl.BlockSpec((r, tn), lambda j: (0, j)),
        out_shape=jax.ShapeDtypeStruct((r, n), F32),
        compiler_params=_cparams(("arbitrary",)),
        name="modulation",
    )(cond, w, b.reshape(1, n))
    return out.reshape(r, N_MOD, d)


def _in_even_kernel(x_ref, mod_ref, ng_ref, w_ref, qg_ref, kg_ref, bd_ref, u_ref, q_ref, k_ref, v_ref):
    h = _norm_mod(x_ref[0], ng_ref[...], mod_ref[0, 0:1, :], mod_ref[0, 1:2, :])
    p = jnp.dot(h.astype(BF), w_ref[...], preferred_element_type=F32)
    bd = bd_ref[...]
    w = FN_WIDTH
    u_ref[0] = p[:, 0:w].astype(BF)
    q_ref[0] = _head_norm(p[:, w:2 * w], bd, qg_ref[...]).astype(BF)
    k_ref[0] = _head_norm(p[:, 2 * w:3 * w], bd, kg_ref[...]).astype(BF)
    v_ref[0] = p[:, 3 * w:4 * w].astype(BF)


def _in_even(x, mods, mod_row, norm_gain, w_in, q_gain, k_gain, blockdiag):
    b, n, d = x.shape
    tm = _token_tile(n)
    wout = FN_WIDTH
    tok = lambda width: pl.BlockSpec((1, tm, width), lambda bi, i: (bi, i, 0))
    return pl.pallas_call(
        _in_even_kernel,
        grid=(b, n // tm),
        in_specs=[tok(d), _mod_spec(d, mod_row), _const_spec((1, d)), _const_spec(w_in.shape),
                  _const_spec((1, wout)), _const_spec((1, wout)), _const_spec(blockdiag.shape)],
        out_specs=[tok(wout)] * 4,
        out_shape=[jax.ShapeDtypeStruct((b, n, wout), BF)] * 4,
        compiler_params=_cparams(("parallel", "parallel")),
        name="in_even",
    )(x, mods, norm_gain, w_in, q_gain, k_gain, blockdiag)


def _fourier_kernel(u_ref, cm_ref, ml_ref, o_ref, z_ref):
    n = u_ref.shape[1]
    gd = FN_GROUP_DIM
    for g in range(FN_GROUPS):
        z = jnp.dot(u_ref[0, :, g * gd:(g + 1) * gd], cm_ref[...], preferred_element_type=F32)
        z_ref[0:n, g * gd:(g + 1) * gd] = z[:, :gd].astype(BF)
        z_ref[n:2 * n, g * gd:(g + 1) * gd] = z[:, gd:].astype(BF)
    o_ref[0] = jnp.dot(ml_ref[...], z_ref[...], preferred_element_type=F32).astype(BF)


def _dft_tables(n):
    def cos_sin(size):
        j = jnp.arange(size, dtype=jnp.int32)
        ang = ((j[:, None] * j[None, :]) % size).astype(F32) * (2.0 * np.pi / size)
        return jnp.cos(ang), jnp.sin(ang)
    cc, sc = cos_sin(FN_GROUP_DIM)
    cl, sl = cos_sin(n)
    scale = 1.0 / np.sqrt(n * FN_GROUP_DIM)
    return (jnp.concatenate([cc, sc], axis=1).astype(BF),
            (jnp.concatenate([cl, -sl], axis=1) * scale).astype(BF))


def _fourier(u):
    b, n, w = u.shape
    cm, ml = _dft_tables(n)
    return pl.pallas_call(
        _fourier_kernel,
        grid=(b,),
        in_specs=[pl.BlockSpec((1, n, w), lambda bi: (bi, 0, 0)), _const_spec(cm.shape), _const_spec(ml.shape)],
        out_specs=pl.BlockSpec((1, n, w), lambda bi: (bi, 0, 0)),
        out_shape=jax.ShapeDtypeStruct((b, n, w), BF),
        scratch_shapes=[pltpu.VMEM((2 * n, w), BF)],
        compiler_params=_cparams(("parallel",)),
        name="fourier",
    )(u, cm, ml)


def _pair_stack(qp, first_half):
    zero = jnp.zeros_like(qp)
    return jnp.concatenate([jnp.where(first_half, qp, zero), jnp.where(first_half, zero, qp)], axis=0)


def _pair_merge(o, nq, first_half):
    return jnp.where(first_half, o[:nq], o[nq:])


def _na_kernel(q_ref, k_ref, v_ref, kc_ref, vc_ref, bias_ref, o_ref, *, rows):
    rb = pl.program_id(1)
    ws = jnp.minimum(jnp.maximum(NA_QROWS * rb - NA_KH // 2, 0), rows - NA_WIN)
    start = pl.multiple_of(ws * GRID_W, GRID_W)
    nq = NA_QROWS * GRID_W
    nk = NA_WIN * GRID_W
    first_half = lax.broadcasted_iota(jnp.int32, (nq, LANES), 1) < HEAD_DIM
    for p in range(NA_HEADS // 2):
        sl = slice(p * LANES, (p + 1) * LANES)
        qs = _pair_stack(q_ref[0, :, sl], first_half)
        kw = k_ref[0, pl.ds(start, nk), sl]
        vw = v_ref[0, pl.ds(start, nk), sl]
        bias = jnp.concatenate([bias_ref[0, 2 * p], bias_ref[0, 2 * p + 1]], axis=0)
        s_loc = lax.dot_general(qs, kw, NT_DIMS, preferred_element_type=F32) + bias
        s_ctx = lax.dot_general(qs, kc_ref[0, :, sl], NT_DIMS, preferred_element_type=F32)
        m = jnp.maximum(jnp.max(s_loc, axis=-1, keepdims=True), jnp.max(s_ctx, axis=-1, keepdims=True))
        p_loc = jnp.exp(s_loc - m)
        p_ctx = jnp.exp(s_ctx - m)
        denom = jnp.sum(p_loc, axis=-1, keepdims=True) + jnp.sum(p_ctx, axis=-1, keepdims=True)
        o = (jnp.dot(p_loc.astype(BF), vw, preferred_element_type=F32)
             + jnp.dot(p_ctx.astype(BF), vc_ref[0, :, sl], preferred_element_type=F32)) / denom
        o_ref[0, :, sl] = _pair_merge(o, nq, first_half).astype(BF)


def _na_block_class(rb, n_blocks):
    return int(rb > 0) + int(rb == n_blocks - 1)


def _na_bias_tables(rpb, rows):
    n_blocks = rows // NA_QROWS
    col = np.arange(GRID_W)
    col_start = np.clip(col - NA_KW // 2, 0, GRID_W - NA_KW)
    col_in = (col[None, :] >= col_start[:, None]) & (col[None, :] < col_start[:, None] + NA_KW)
    dc_idx = np.clip(col[None, :] - col[:, None] + NA_KW - 1, 0, 2 * NA_KW - 2)
    per_class = {}
    for rb in range(n_blocks):
        ws = min(max(NA_QROWS * rb - NA_KH // 2, 0), rows - NA_WIN)
        r = NA_QROWS * rb + np.arange(NA_QROWS)
        kr = ws + np.arange(NA_WIN)
        r0 = np.clip(r - NA_KH // 2, 0, rows - NA_KH)
        row_in = (kr[None, :] >= r0[:, None]) & (kr[None, :] < r0[:, None] + NA_KH)
        dr_idx = np.clip(kr[None, :] - r[:, None] + NA_KH - 1, 0, 2 * NA_KH - 2)
        valid = row_in[:, None, :, None] & col_in[None, :, None, :]
        dr_full = np.broadcast_to(dr_idx[:, None, :, None], valid.shape)
        dc_full = np.broadcast_to(dc_idx[None, :, None, :], valid.shape)
        entry = (valid, dr_full, dc_full)
        cls = _na_block_class(rb, n_blocks)
        if cls in per_class:
            assert all(np.array_equal(a, b) for a, b in zip(per_class[cls], entry)), "row-block classes differ"
        per_class[cls] = entry
    nq, nk = NA_QROWS * GRID_W, NA_WIN * GRID_W
    tables = []
    for cls in range(3):
        valid, dr_full, dc_full = per_class[cls]
        t = rpb[:, dr_full.reshape(nq, nk), dc_full.reshape(nq, nk)].astype(F32)
        tables.append(jnp.where(valid.reshape(nq, nk)[None], t, NEG_INF))
    return jnp.stack(tables)


def _neighbourhood_attention(q, k, v, kc, vc, rpb):
    b, n, w = q.shape
    c = kc.shape[1]
    rows = n // GRID_W
    assert rows % NA_QROWS == 0 and rows >= NA_WIN + 1
    n_blocks = rows // NA_QROWS
    bias = _na_bias_tables(rpb, rows)
    nq, nk = NA_QROWS * GRID_W, NA_WIN * GRID_W
    full = lambda length: pl.BlockSpec((1, length, w), lambda bi, rb: (bi, 0, 0))
    bias_spec = pl.BlockSpec(
        (1, NA_HEADS, nq, nk),
        lambda bi, rb: (jnp.minimum(rb, 1) + (rb == n_blocks - 1).astype(jnp.int32), 0, 0, 0))
    return pl.pallas_call(
        functools.partial(_na_kernel, rows=rows),
        grid=(b, n_blocks),
        in_specs=[pl.BlockSpec((1, nq, w), lambda bi, rb: (bi, rb, 0)), full(n), full(n), full(c), full(c), bias_spec],
        out_specs=pl.BlockSpec((1, nq, w), lambda bi, rb: (bi, rb, 0)),
        out_shape=jax.ShapeDtypeStruct((b, n, w), BF),
        compiler_params=_cparams(("parallel", "arbitrary")),
        name="neighbourhood_attention",
    )(q, k, v, kc, vc, bias)


def _ctx_attn_kernel(q_ref, k_ref, v_ref, o_ref):
    nq = q_ref.shape[1]
    first_half = lax.broadcasted_iota(jnp.int32, (nq, LANES), 1) < HEAD_DIM
    for p in range(NA_HEADS // 2):
        sl = slice(p * LANES, (p + 1) * LANES)
        qs = _pair_stack(q_ref[0, :, sl], first_half)
        s = lax.dot_general(qs, k_ref[0, :, sl], NT_DIMS, preferred_element_type=F32)
        e = jnp.exp(s - jnp.max(s, axis=-1, keepdims=True))
        o = jnp.dot(e.astype(BF), v_ref[0, :, sl], preferred_element_type=F32) / jnp.sum(e, axis=-1, keepdims=True)
        o_ref[0, :, sl] = _pair_merge(o, nq, first_half).astype(BF)


def _context_attention(q, k, v):
    b, c, w = q.shape
    spec = pl.BlockSpec((1, c, w), lambda bi: (bi, 0, 0))
    return pl.pallas_call(
        _ctx_attn_kernel,
        grid=(b,),
        in_specs=[spec] * 3,
        out_specs=spec,
        out_shape=jax.ShapeDtypeStruct((b, c, w), BF),
        compiler_params=_cparams(("parallel",)),
        name="context_attention",
    )(q, k, v)


def _out_ffn_even_kernel(x_ref, oa_ref, ob_ref, mod_ref, ng_ref, woa_ref, wob_ref, w1_ref, w3_ref, w2_ref, o_ref,
                         *, f_chunks):
    mix = (jnp.dot(oa_ref[0], woa_ref[...], preferred_element_type=F32)
           + jnp.dot(ob_ref[0], wob_ref[...], preferred_element_type=F32))
    x1 = x_ref[0] + mod_ref[0, 2:3, :] * mix
    h = _norm_mod(x1, ng_ref[...], mod_ref[0, 3:4, :], mod_ref[0, 4:5, :]).astype(BF)
    fc = w1_ref.shape[1] // f_chunks
    y = None
    for c in range(f_chunks):
        a = jnp.dot(h, w1_ref[:, c * fc:(c + 1) * fc], preferred_element_type=F32)
        g = jnp.dot(h, w3_ref[:, c * fc:(c + 1) * fc], preferred_element_type=F32)
        part = jnp.dot((_silu(a) * g).astype(BF), w2_ref[c * fc:(c + 1) * fc, :], preferred_element_type=F32)
        y = part if y is None else y + part
    o_ref[0] = x1 + mod_ref[0, 5:6, :] * y


def _out_ffn_even(x, oa, ob, mods, mod_row, norm_gain, w_out, w1, w3, w2):
    b, n, d = x.shape
    tm = _token_tile(n)
    wa = oa.shape[2]
    tok = lambda width: pl.BlockSpec((1, tm, width), lambda bi, i: (bi, i, 0))
    return pl.pallas_call(
        functools.partial(_out_ffn_even_kernel, f_chunks=2),
        grid=(b, n // tm),
        in_specs=[tok(d), tok(wa), tok(ob.shape[2]), _mod_spec(d, mod_row), _const_spec((1, d)),
                  _const_spec((wa, d)), _const_spec((w_out.shape[0] - wa, d)),
                  _const_spec(w1.shape), _const_spec(w3.shape), _const_spec(w2.shape)],
        out_specs=tok(d),
        out_shape=jax.ShapeDtypeStruct((b, n, d), F32),
        compiler_params=_cparams(("parallel", "parallel")),
        name="out_ffn_even",
    )(x, oa, ob, mods, norm_gain, w_out[:wa], w_out[wa:], w1, w3, w2)


def _rope(t, cos, sin_signed):
    width = t.shape[1]
    lane = lax.broadcasted_iota(jnp.int32, t.shape, 1)
    partner = jnp.where(lane % 2 == 0, pltpu.roll(t, width - 1, 1), pltpu.roll(t, 1, 1))
    reps = width // LANES
    return t * jnp.concatenate([cos] * reps, axis=1) + partner * jnp.concatenate([sin_signed] * reps, axis=1)


def _in_odd_kernel(*refs, is_latent):
    if is_latent:
        (x_ref, mod_ref, ng_ref, w_ref, qg_ref, kg_ref, bdq_ref, bdk_ref, cos_ref, sin_ref,
         q_ref, kk_ref, vv_ref, u_ref) = refs
    else:
        x_ref, mod_ref, ng_ref, w_ref, kg_ref, bdk_ref, kk_ref, vv_ref, u_ref = refs
    h = _norm_mod(x_ref[0], ng_ref[...], mod_ref[0, 0:1, :], mod_ref[0, 1:2, :])
    p = jnp.dot(h.astype(BF), w_ref[...], preferred_element_type=F32)
    c0 = GQ_WIDTH if is_latent else 0
    kk = _head_norm(p[:, c0:c0 + GKV_PAD], bdk_ref[...], kg_ref[...])
    if is_latent:
        q = _head_norm(p[:, 0:GQ_WIDTH], bdq_ref[...], qg_ref[...])
        q_ref[0] = _rope(q, cos_ref[...], sin_ref[...]).astype(BF)
        kk = _rope(kk, cos_ref[...], sin_ref[...])
    kk_ref[0] = kk.astype(BF)
    vv_ref[0] = p[:, c0 + GKV_PAD:c0 + 2 * GKV_PAD].astype(BF)
    u_ref[0] = p[:, c0 + 2 * GKV_PAD:].astype(BF)


def _in_odd(x, mods, mod_row, norm_gain, w_ext, q_gain, k_gain, bdq, bdk, cos, sin_signed, is_latent):
    b, n, d = x.shape
    tm = _token_tile(n)
    tok = lambda width: pl.BlockSpec((1, tm, width), lambda bi, i: (bi, i, 0))
    vec = lambda width: _const_spec((1, width))
    if is_latent:
        rope_spec = pl.BlockSpec((tm, LANES), lambda bi, i: (i, 0))
        in_specs = [tok(d), _mod_spec(d, mod_row), vec(d), _const_spec(w_ext.shape), vec(GQ_WIDTH), vec(GKV_PAD),
                    _const_spec(bdq.shape), _const_spec(bdk.shape), rope_spec, rope_spec]
        args = (x, mods, norm_gain, w_ext, q_gain, k_gain, bdq, bdk, cos, sin_signed)
        widths = [GQ_WIDTH, GKV_PAD, GKV_PAD, S5_WIDTH]
    else:
        in_specs = [tok(d), _mod_spec(d, mod_row), vec(d), _const_spec(w_ext.shape), vec(GKV_PAD),
                    _const_spec(bdk.shape)]
        args = (x, mods, norm_gain, w_ext, k_gain, bdk)
        widths = [GKV_PAD, GKV_PAD, S5_WIDTH]
    return pl.pallas_call(
        functools.partial(_in_odd_kernel, is_latent=is_latent),
        grid=(b, n // tm),
        in_specs=in_specs,
        out_specs=[tok(wd) for wd in widths],
        out_shape=[jax.ShapeDtypeStruct((b, n, wd), BF) for wd in widths],
        compiler_params=_cparams(("parallel", "parallel")),
        name="in_odd_latent" if is_latent else "in_odd_context",
    )(*args)


def _rope_tables(n):
    t = jnp.arange(n)
    row = (t // GRID_W).astype(F32)
    col = (t % GRID_W).astype(F32)
    n_axis = HEAD_DIM // 4
    freqs = ROPE_THETA ** (-jnp.arange(n_axis, dtype=F32) / n_axis)
    ang = jnp.concatenate([row[:, None] * freqs, col[:, None] * freqs], axis=-1)
    ang = jnp.repeat(ang, 2, axis=-1)
    ang = jnp.concatenate([ang, ang], axis=-1)
    sign = jnp.where(jnp.arange(LANES) % 2 == 0, -1.0, 1.0).astype(F32)
    return jnp.cos(ang), jnp.sin(ang) * sign


def _gqa_kernel(q_ref, kk_ref, vv_ref, o_ref):
    tq = q_ref.shape[1]
    first_half = lax.broadcasted_iota(jnp.int32, (tq, LANES), 1) < HEAD_DIM
    group = GQ_HEADS // GKV_HEADS
    for j in range(GKV_HEADS):
        kj = kk_ref[0, :, j * LANES:(j + 1) * LANES]
        vj = vv_ref[0, :, j * LANES:(j + 1) * LANES]
        base = j * group * HEAD_DIM
        qs = jnp.concatenate(
            [_pair_stack(q_ref[0, :, base + i * LANES:base + (i + 1) * LANES], first_half)
             for i in range(group // 2)], axis=0)
        s = lax.dot_general(qs, kj, NT_DIMS, preferred_element_type=F32)
        e = jnp.exp(s - jnp.max(s, axis=-1, keepdims=True))
        o = jnp.dot(e.astype(BF), vj, preferred_element_type=F32) / jnp.sum(e, axis=-1, keepdims=True)
        for i in range(group // 2):
            pair = _pair_merge(o[2 * i * tq:(2 * i + 2) * tq], tq, first_half)
            o_ref[0, :, base + i * LANES:base + (i + 1) * LANES] = pair.astype(BF)


def _gqa(q, kk, vv):
    b, n, w = q.shape
    nk = kk.shape[1]
    tq = 128
    kv_spec = pl.BlockSpec((1, nk, GKV_PAD), lambda bi, i: (bi, 0, 0))
    return pl.pallas_call(
        _gqa_kernel,
        grid=(b, n // tq),
        in_specs=[pl.BlockSpec((1, tq, w), lambda bi, i: (bi, i, 0)), kv_spec, kv_spec],
        out_specs=pl.BlockSpec((1, tq, w), lambda bi, i: (bi, i, 0)),
        out_shape=jax.ShapeDtypeStruct((b, n, w), BF),
        compiler_params=_cparams(("parallel", "arbitrary")),
        name="gqa",
    )(q, kk, vv)


def _s5_kernel(u_ref, bbd_ref, lre_ref, lim_ref, cbd_ref, y_ref, bu_ref, st_ref, *, steps, bp):
    @pl.when(pl.program_id(1) == 0)
    def _():
        st_ref[...] = jnp.zeros_like(st_ref)

    bu_ref[...] = jnp.dot(u_ref[0], bbd_ref[0], preferred_element_type=F32)
    lc = 2 * LANES
    for j in range(S5_NP // lc):
        re_sl = slice(j * lc, (j + 1) * lc)
        im_sl = slice(S5_NP + j * lc, S5_NP + (j + 1) * lc)
        lre = jnp.broadcast_to(lre_ref[0, :, re_sl], (bp, lc))
        lim = jnp.broadcast_to(lim_ref[0, :, re_sl], (bp, lc))

        def step(t, carry):
            xre, xim = carry
            r0 = pl.multiple_of(t * bp, bp)
            nre = lre * xre - lim * xim + bu_ref[pl.ds(r0, bp), re_sl]
            nim = lre * xim + lim * xre + bu_ref[pl.ds(r0, bp), im_sl]
            bu_ref[pl.ds(r0, bp), re_sl] = nre
            bu_ref[pl.ds(r0, bp), im_sl] = nim
            return nre, nim

        xre, xim = lax.fori_loop(0, steps, step, (st_ref[:, re_sl], st_ref[:, im_sl]))
        st_ref[:, re_sl] = xre
        st_ref[:, im_sl] = xim
    y_ref[0] = jnp.dot(bu_ref[...].astype(BF), cbd_ref[0], preferred_element_type=F32)


def _s5_params(a_re, a_im, log_step, b_re, b_im, c_re, c_im):
    a_re, a_im = a_re.astype(F32), a_im.astype(F32)
    step = jnp.exp(log_step.astype(F32))[..., None]
    mag = jnp.exp(a_re * step)
    lre, lim = mag * jnp.cos(a_im * step), mag * jnp.sin(a_im * step)
    den = a_re * a_re + a_im * a_im
    kre = ((lre - 1.0) * a_re + lim * a_im) / den
    kim = (lim * a_re - (lre - 1.0) * a_im) / den
    bre = kre[..., None] * b_re - kim[..., None] * b_im
    bim = kre[..., None] * b_im + kim[..., None] * b_re
    eye = jnp.eye(S5_GROUPS, dtype=F32)

    def in_blockdiag(m):
        return jnp.einsum('dgpn,gh->dgnhp', m, eye).reshape(2, S5_WIDTH, S5_NP)

    def out_blockdiag(m):
        return jnp.einsum('dgnp,gh->dgphn', m, eye).reshape(2, S5_NP, S5_WIDTH)

    bbd = jnp.concatenate([in_blockdiag(bre), in_blockdiag(bim)], axis=2).astype(BF)
    cbd = jnp.concatenate([out_blockdiag(c_re.astype(F32)), -out_blockdiag(c_im.astype(F32))], axis=1).astype(BF)
    return bbd, lre.reshape(2, 1, S5_NP), lim.reshape(2, 1, S5_NP), cbd


def _s5_scan(u_seq, bbd, lre, lim, cbd, bp):
    rows = u_seq.shape[1]
    steps = 32
    rb = steps * bp
    assert rows % rb == 0
    per_dir = lambda shape: pl.BlockSpec((1,) + shape, lambda d, i: (d, 0, 0))
    return pl.pallas_call(
        functools.partial(_s5_kernel, steps=steps, bp=bp),
        grid=(2, rows // rb),
        in_specs=[pl.BlockSpec((1, rb, S5_WIDTH), lambda d, i: (d, i, 0)), per_dir((S5_WIDTH, 2 * S5_NP)),
                  per_dir((1, S5_NP)), per_dir((1, S5_NP)), per_dir((2 * S5_NP, S5_WIDTH))],
        out_specs=pl.BlockSpec((1, rb, S5_WIDTH), lambda d, i: (d, i, 0)),
        out_shape=jax.ShapeDtypeStruct((2, rows, S5_WIDTH), F32),
        scratch_shapes=[pltpu.VMEM((rb, 2 * S5_NP), F32), pltpu.VMEM((bp, 2 * S5_NP), F32)],
        compiler_params=_cparams(("arbitrary", "arbitrary")),
        name="s5_scan",
    )(u_seq, bbd, lre, lim, cbd)


def _s5_mixer_latent(u_lat, u_ctx, params):
    b, n, w = u_lat.shape
    c = u_ctx.shape[1]
    bp = -(-b // 8) * 8
    fwd = jnp.concatenate([u_ctx, u_lat], axis=1)
    bwd = jnp.concatenate([u_ctx[:, ::-1], u_lat[:, ::-1]], axis=1)
    seq = jnp.stack([fwd, bwd])
    seq = jnp.pad(seq, ((0, 0), (0, bp - b), (0, 0), (0, 0)))
    seq = seq.transpose(0, 2, 1, 3).reshape(2, (c + n) * bp, w)
    y = _s5_scan(seq, *params, bp).reshape(2, c + n, bp, w)[:, c:, :b]
    return (y[0] + y[1, ::-1]).transpose(1, 0, 2)


def _out_odd_kernel(x_ref, oa_ref, ys_ref, u_ref, mod_ref, ng_ref, d_ref, gw_ref, gb_ref, woa_ref, wob_ref,
                    rw_ref, rb_ref, x1_ref, h_ref, gate_ref):
    y = ys_ref[0] + d_ref[...] * u_ref[0].astype(F32)
    gl = 0.5 * y * (1.0 + jnp.tanh(np.sqrt(2.0 / np.pi) * (y + 0.044715 * (y * y * y))))
    z = gl * jax.nn.sigmoid(jnp.dot(gl.astype(BF), gw_ref[...], preferred_element_type=F32) + gb_ref[...])
    mix = (jnp.dot(oa_ref[0], woa_ref[...], preferred_element_type=F32)
           + jnp.dot(z.astype(BF), wob_ref[...], preferred_element_type=F32))
    x1 = x_ref[0] + mod_ref[0, 2:3, :] * mix
    x1_ref[0] = x1
    h = _norm_mod(x1, ng_ref[...], mod_ref[0, 3:4, :], mod_ref[0, 4:5, :]).astype(BF)
    h_ref[0] = h
    logits = jnp.dot(h, rw_ref[...], preferred_element_type=F32) + rb_ref[...]
    lane = lax.broadcasted_iota(jnp.int32, logits.shape, 1)
    m1 = jnp.max(logits, axis=-1, keepdims=True)
    i1 = jnp.min(jnp.where(logits == m1, lane, LANES), axis=-1, keepdims=True)
    rest = jnp.where(lane == i1, -jnp.inf, logits)
    m2 = jnp.max(rest, axis=-1, keepdims=True)
    i2 = jnp.min(jnp.where(rest == m2, lane, LANES), axis=-1, keepdims=True)
    e2 = jnp.exp(m2 - m1)
    gate_ref[0] = jnp.where(lane == i1, 1.0 / (1.0 + e2), 0.0) + jnp.where(lane == i2, e2 / (1.0 + e2), 0.0)


def _out_odd(x, oa, ys, u, mods, norm_gain, d_skip, glu_w, glu_b, w_out, router_w, router_b):
    b, n, d = x.shape
    tm = _token_tile(n)
    wa = oa.shape[2]
    tok = lambda width: pl.BlockSpec((1, tm, width), lambda bi, i: (bi, i, 0))
    vec = lambda width: _const_spec((1, width))
    return pl.pallas_call(
        _out_odd_kernel,
        grid=(b, n // tm),
        in_specs=[tok(d), tok(wa), tok(S5_WIDTH), tok(S5_WIDTH), _mod_spec(d, None), vec(d), vec(S5_WIDTH),
                  _const_spec(glu_w.shape), vec(S5_WIDTH), _const_spec((wa, d)), _const_spec((S5_WIDTH, d)),
                  _const_spec(router_w.shape), vec(LANES)],
        out_specs=[tok(d), tok(d), tok(LANES)],
        out_shape=[jax.ShapeDtypeStruct((b, n, d), F32), jax.ShapeDtypeStruct((b, n, d), BF),
                   jax.ShapeDtypeStruct((b, n, LANES), F32)],
        compiler_params=_cparams(("parallel", "parallel")),
        name="out_odd",
    )(x, oa, ys, u, mods, norm_gain, d_skip, glu_w, glu_b, w_out[:wa], w_out[wa:], router_w, router_b)


def _moe_kernel(h_ref, gate_ref, x1_ref, mod_ref, w1_ref, w3_ref, w2_ref, o_ref, acc_ref):
    e = pl.program_id(2)
    c = pl.program_id(3)

    @pl.when((e == 0) & (c == 0))
    def _():
        acc_ref[...] = jnp.zeros_like(acc_ref)

    h = h_ref[0]
    a = jnp.dot(h, w1_ref[0], preferred_element_type=F32)
    g = jnp.dot(h, w3_ref[0], preferred_element_type=F32)
    lane = lax.broadcasted_iota(jnp.int32, gate_ref.shape[1:], 1)
    gate = jnp.sum(jnp.where(lane == e, gate_ref[0], 0.0), axis=-1, keepdims=True)
    acc_ref[...] += jnp.dot((_silu(a) * g * gate).astype(BF), w2_ref[0], preferred_element_type=F32)

    @pl.when((e == pl.num_programs(2) - 1) & (c == pl.num_programs(3) - 1))
    def _():
        o_ref[0] = x1_ref[0] + mod_ref[0, 5:6, :] * acc_ref[...]


def _moe(h, gates, x1, mods, w1, w3, w2):
    b, n, d = h.shape
    n_exp, _, f = w1.shape
    tm = _token_tile(n)
    f_chunks = 2
    fc = f // f_chunks
    tok = lambda width: pl.BlockSpec((1, tm, width), lambda bi, i, e, c: (bi, i, 0))
    return pl.pallas_call(
        _moe_kernel,
        grid=(b, n // tm, n_exp, f_chunks),
        in_specs=[tok(d), tok(LANES), tok(d), pl.BlockSpec((1, N_MOD, d), lambda bi, i, e, c: (bi, 0, 0)),
                  pl.BlockSpec((1, d, fc), lambda bi, i, e, c: (e, 0, c)),
                  pl.BlockSpec((1, d, fc), lambda bi, i, e, c: (e, 0, c)),
                  pl.BlockSpec((1, fc, d), lambda bi, i, e, c: (e, c, 0))],
        out_specs=tok(d),
        out_shape=jax.ShapeDtypeStruct((b, n, d), F32),
        scratch_shapes=[pltpu.VMEM((tm, d), F32)],
        compiler_params=_cparams(("parallel", "parallel", "arbitrary", "arbitrary")),
        name="moe",
    )(h, gates, x1, mods, w1, w3, w2)


def _head_blockdiag(width):
    head = np.arange(width) // HEAD_DIM
    return jnp.asarray((head[:, None] == head[None, :]).astype(np.float32) / HEAD_DIM, dtype=BF)


def _tile_gain(gain, width, scale=1.0):
    return (jnp.tile(gain.astype(F32), width // HEAD_DIM) * scale).reshape(1, width)


def _even_layer(x, xc, mods, ctx_row, norm_mix, norm_ffn, w_in, q_gain, k_gain, rpb, w_out, w1, w3, w2):
    d = x.shape[2]
    bd = _head_blockdiag(NA_WIDTH)
    w_in = w_in.astype(BF)
    qg = _tile_gain(q_gain, NA_WIDTH, HEAD_DIM ** -0.5)
    kg = _tile_gain(k_gain, NA_WIDTH)
    ng_mix, ng_ffn = norm_mix.reshape(1, d), norm_ffn.reshape(1, d)
    u, q, k, v = _in_even(x, mods, None, ng_mix, w_in, qg, kg, bd)
    uc, qc, kc, vc = _in_even(xc, mods, ctx_row, ng_mix, w_in, qg, kg, bd)
    o_na = _neighbourhood_attention(q, k, v, kc, vc, rpb)
    o_ctx = _context_attention(qc, kc, vc)
    w_out, w1, w3, w2 = (t.astype(BF) for t in (w_out, w1, w3, w2))
    x = _out_ffn_even(x, _fourier(u), o_na, mods, None, ng_ffn, w_out, w1, w3, w2)
    xc = _out_ffn_even(xc, _fourier(uc), o_ctx, mods, ctx_row, ng_ffn, w_out, w1, w3, w2)
    return x, xc


def _odd_layer(x, xc, mods, ctx_row, norm_mix, norm_ffn, w_in, q_gain, k_gain, s5, d_skip, glu_w, glu_b, w_out,
               router_w, router_b, ew1, ew3, ew2):
    b, n, d = x.shape
    kvw = GKV_HEADS * HEAD_DIM
    wq, wk, wv, wu = jnp.split(w_in, [GQ_WIDTH, GQ_WIDTH + kvw, GQ_WIDTH + 2 * kvw], axis=1)

    def twice(w):
        return jnp.repeat(w.reshape(d, GKV_HEADS, 1, HEAD_DIM), 2, axis=2).reshape(d, GKV_PAD)

    w_lat = jnp.concatenate([wq, twice(wk), twice(wv), wu], axis=1).astype(BF)
    w_ctx = jnp.concatenate([twice(wk), twice(wv), wu], axis=1).astype(BF)
    qg = _tile_gain(q_gain, GQ_WIDTH, HEAD_DIM ** -0.5)
    kg = _tile_gain(k_gain, GKV_PAD)
    bdq, bdk = _head_blockdiag(GQ_WIDTH), _head_blockdiag(GKV_PAD)
    cos, sin_signed = _rope_tables(n)
    ng_mix, ng_ffn = norm_mix.reshape(1, d), norm_ffn.reshape(1, d)
    q, kk, vv, u = _in_odd(x, mods, None, ng_mix, w_lat, qg, kg, bdq, bdk, cos, sin_signed, True)
    kkc, vvc, uc = _in_odd(xc, mods, ctx_row, ng_mix, w_ctx, None, kg, None, bdk, None, None, False)
    o_attn = _gqa(q, jnp.concatenate([kkc, kk], axis=1), jnp.concatenate([vvc, vv], axis=1))
    y_ssm = _s5_mixer_latent(u, uc, _s5_params(*s5))
    rw = jnp.pad(router_w, ((0, 0), (0, LANES - N_EXPERTS))).astype(BF)
    rb = jnp.pad(router_b.astype(F32), (0, LANES - N_EXPERTS), constant_values=NEG_INF).reshape(1, LANES)
    x1, h, gates = _out_odd(x, o_attn, y_ssm, u, mods, ng_ffn, d_skip.reshape(1, S5_WIDTH).astype(F32),
                            glu_w.astype(BF), glu_b.reshape(1, S5_WIDTH).astype(F32), w_out.astype(BF), rw, rb)
    return _moe(h, gates, x1, mods, ew1.astype(BF), ew3.astype(BF), ew2.astype(BF))


def kernel(x, c, ctx, c_ctx, ev_mod_w, ev_mod_b, ev_norm_mix, ev_norm_ffn, ev_w_in, ev_q_gain, ev_k_gain, ev_rpb,
           ev_w_out, ev_ffn_w1, ev_ffn_w3, ev_ffn_w2, od_mod_w, od_mod_b, od_norm_mix, od_norm_ffn, od_w_in,
           od_q_gain, od_k_gain, od_s5_a_re, od_s5_a_im, od_s5_log_step, od_s5_b_re, od_s5_b_im, od_s5_c_re,
           od_s5_c_im, od_s5_d, od_s5_glu_w, od_s5_glu_b, od_w_out, od_router_w, od_router_b, od_exp_w1, od_exp_w3,
           od_exp_w2):
    assert ev_mod_w.shape[0] == 1 and od_mod_w.shape[0] == 1, "one even and one odd layer"
    b, n, d = x.shape
    rows = -(-(b + 1) // 8) * 8
    cond = jnp.zeros((rows, d), F32).at[:b].set(c).at[b].set(c_ctx)
    mods_even = _modulation(cond, ev_mod_w[0], ev_mod_b[0])
    mods_odd = _modulation(cond, od_mod_w[0], od_mod_b[0])
    x, xc = _even_layer(x, ctx, mods_even, b, ev_norm_mix[0], ev_norm_ffn[0], ev_w_in[0], ev_q_gain[0], ev_k_gain[0],
                        ev_rpb[0], ev_w_out[0], ev_ffn_w1[0], ev_ffn_w3[0], ev_ffn_w2[0])
    s5 = (od_s5_a_re[0], od_s5_a_im[0], od_s5_log_step[0], od_s5_b_re[0], od_s5_b_im[0], od_s5_c_re[0], od_s5_c_im[0])
    return _odd_layer(x, xc, mods_odd, b, od_norm_mix[0], od_norm_ffn[0], od_w_in[0], od_q_gain[0], od_k_gain[0],
                      s5, od_s5_d[0], od_s5_glu_w[0], od_s5_glu_b[0], od_w_out[0], od_router_w[0], od_router_b[0],
                      od_exp_w1[0], od_exp_w3[0], od_exp_w2[0])
```

```python
import functools

import numpy as np
import jax
import jax.numpy as jnp
from jax import lax
from jax.experimental import pallas as pl
from jax.experimental.pallas import tpu as pltpu

BF = jnp.bfloat16
F32 = jnp.float32

EPS = 1e-6
NEG_INF = -1e30
GRID_W = 64
LANES = 128
HEAD_DIM = 64
FN_GROUPS, FN_GROUP_DIM = 4, 128
FN_WIDTH = FN_GROUPS * FN_GROUP_DIM
NA_HEADS = 8
NA_WIDTH = NA_HEADS * HEAD_DIM
NA_KH, NA_KW = 8, 16
NA_QROWS = 4
NA_WIN = NA_QROWS + NA_KH - 1
GQ_HEADS, GKV_HEADS = 12, 3
GQ_WIDTH = GQ_HEADS * HEAD_DIM
GKV_PAD = GKV_HEADS * LANES
S5_GROUPS, S5_GROUP_DIM, S5_STATE = 16, 16, 64
S5_WIDTH = S5_GROUPS * S5_GROUP_DIM
S5_NP = S5_GROUPS * S5_STATE
N_EXPERTS = 8
N_MOD = 6
ROPE_THETA = 10000.0
VMEM_LIMIT = 56 * 1024 * 1024

NT_DIMS = (((1,), (1,)), ((), ()))


def _cparams(sem):
    return pltpu.CompilerParams(dimension_semantics=sem, vmem_limit_bytes=VMEM_LIMIT)


def _const_spec(shape):
    nd = len(shape)
    return pl.BlockSpec(shape, lambda *_: (0,) * nd, pipeline_mode=pl.Buffered(1))


def _token_tile(n):
    return 512 if n % 512 == 0 else 256


def _mod_spec(d, row):
    if row is None:
        return pl.BlockSpec((1, N_MOD, d), lambda b, i: (b, 0, 0))
    return pl.BlockSpec((1, N_MOD, d), lambda b, i: (row, 0, 0))


def _norm_mod(x, gain, shift, scale):
    ms = jnp.mean(x * x, axis=-1, keepdims=True)
    return (x * lax.rsqrt(ms + EPS) * gain) * (1.0 + scale) + shift


def _head_norm(t, blockdiag, gain):
    ms = jnp.dot((t * t).astype(BF), blockdiag, preferred_element_type=F32)
    return t * lax.rsqrt(ms + EPS) * gain


def _silu(a):
    return a * jax.nn.sigmoid(a)


def _mod_kernel(c_ref, w_ref, b_ref, o_ref):
    a = _silu(c_ref[...]).astype(BF)
    o_ref[...] = jnp.dot(a, w_ref[...].astype(BF), preferred_element_type=F32) + b_ref[...]


def _modulation(cond, w, b):
    r, d = cond.shape
    n = w.shape[1]
    tn = n // 4
    out = pl.pallas_call(
        _mod_kernel,
        grid=(n // tn,),
        in_specs=[pl.BlockSpec((r, d), lambda j: (0, 0)),
                  pl.BlockSpec((d, tn), lambda j: (0, j)),
                  pl.BlockSpec((1, tn), lambda j: (0, j))],
        out_specs=pl.BlockSpec((r, tn), lambda j: (0, j)),
        out_shape=jax.ShapeDtypeStruct((r, n), F32),
        compiler_params=_cparams(("arbitrary",)),
        name="modulation",
    )(cond, w, b.reshape(1, n))
    return out.reshape(r, N_MOD, d)


def _in_even_kernel(x_ref, mod_ref, ng_ref, w_ref, qg_ref, kg_ref, bd_ref, u_ref, q_ref, k_ref, v_ref):
    h = _norm_mod(x_ref[0], ng_ref[...], mod_ref[0, 0:1, :], mod_ref[0, 1:2, :])
    p = jnp.dot(h.astype(BF), w_ref[...], preferred_element_type=F32)
    bd = bd_ref[...]
    w = FN_WIDTH
    u_ref[0] = p[:, 0:w].astype(BF)
    q_ref[0] = _head_norm(p[:, w:2 * w], bd, qg_ref[...]).astype(BF)
    k_ref[0] = _head_norm(p[:, 2 * w:3 * w], bd, kg_ref[...]).astype(BF)
    v_ref[0] = p[:, 3 * w:4 * w].astype(BF)


def _in_even(x, mods, mod_row, norm_gain, w_in, q_gain, k_gain, blockdiag):
    b, n, d = x.shape
    tm = _token_tile(n)
    wout = FN_WIDTH
    tok = lambda width: pl.BlockSpec((1, tm, width), lambda bi, i: (bi, i, 0))
    return pl.pallas_call(
        _in_even_kernel,
        grid=(b, n // tm),
        in_specs=[tok(d), _mod_spec(d, mod_row), _const_spec((1, d)), _const_spec(w_in.shape),
                  _const_spec((1, wout)), _const_spec((1, wout)), _const_spec(blockdiag.shape)],
        out_specs=[tok(wout)] * 4,
        out_shape=[jax.ShapeDtypeStruct((b, n, wout), BF)] * 4,
        compiler_params=_cparams(("parallel", "parallel")),
        name="in_even",
    )(x, mods, norm_gain, w_in, q_gain, k_gain, blockdiag)


def _fourier_kernel(u_ref, cm_ref, ml_ref, o_ref, z_ref):
    n = u_ref.shape[1]
    gd = FN_GROUP_DIM
    for g in range(FN_GROUPS):
        z = jnp.dot(u_ref[0, :, g * gd:(g + 1) * gd], cm_ref[...], preferred_element_type=F32)
        z_ref[0:n, g * gd:(g + 1) * gd] = z[:, :gd].astype(BF)
        z_ref[n:2 * n, g * gd:(g + 1) * gd] = z[:, gd:].astype(BF)
    o_ref[0] = jnp.dot(ml_ref[...], z_ref[...], preferred_element_type=F32).astype(BF)


def _dft_tables(n):
    def cos_sin(size):
        j = jnp.arange(size, dtype=jnp.int32)
        ang = ((j[:, None] * j[None, :]) % size).astype(F32) * (2.0 * np.pi / size)
        return jnp.cos(ang), jnp.sin(ang)
    cc, sc = cos_sin(FN_GROUP_DIM)
    cl, sl = cos_sin(n)
    scale = 1.0 / np.sqrt(n * FN_GROUP_DIM)
    return (jnp.concatenate([cc, sc], axis=1).astype(BF),
            (jnp.concatenate([cl, -sl], axis=1) * scale).astype(BF))


def _fourier(u):
    b, n, w = u.shape
    cm, ml = _dft_tables(n)
    return pl.pallas_call(
        _fourier_kernel,
        grid=(b,),
        in_specs=[pl.BlockSpec((1, n, w), lambda bi: (bi, 0, 0)), _const_spec(cm.shape), _const_spec(ml.shape)],
        out_specs=pl.BlockSpec((1, n, w), lambda bi: (bi, 0, 0)),
        out_shape=jax.ShapeDtypeStruct((b, n, w), BF),
        scratch_shapes=[pltpu.VMEM((2 * n, w), BF)],
        compiler_params=_cparams(("parallel",)),
        name="fourier",
    )(u, cm, ml)


def _pair_stack(qp, first_half):
    zero = jnp.zeros_like(qp)
    return jnp.concatenate([jnp.where(first_half, qp, zero), jnp.where(first_half, zero, qp)], axis=0)


def _pair_merge(o, nq, first_half):
    return jnp.where(first_half, o[:nq], o[nq:])


def _na_kernel(q_ref, k_ref, v_ref, kc_ref, vc_ref, bias_ref, o_ref, *, rows):
    rb = pl.program_id(1)
    ws = jnp.minimum(jnp.maximum(NA_QROWS * rb - NA_KH // 2, 0), rows - NA_WIN)
    start = pl.multiple_of(ws * GRID_W, GRID_W)
    nq = NA_QROWS * GRID_W
    nk = NA_WIN * GRID_W
    first_half = lax.broadcasted_iota(jnp.int32, (nq, LANES), 1) < HEAD_DIM
    for p in range(NA_HEADS // 2):
        sl = slice(p * LANES, (p + 1) * LANES)
        qs = _pair_stack(q_ref[0, :, sl], first_half)
        kw = k_ref[0, pl.ds(start, nk), sl]
        vw = v_ref[0, pl.ds(start, nk), sl]
        bias = jnp.concatenate([bias_ref[0, 2 * p], bias_ref[0, 2 * p + 1]], axis=0)
        s_loc = lax.dot_general(qs, kw, NT_DIMS, preferred_element_type=F32) + bias
        s_ctx = lax.dot_general(qs, kc_ref[0, :, sl], NT_DIMS, preferred_element_type=F32)
        m = jnp.maximum(jnp.max(s_loc, axis=-1, keepdims=True), jnp.max(s_ctx, axis=-1, keepdims=True))
        p_loc = jnp.exp(s_loc - m)
        p_ctx = jnp.exp(s_ctx - m)
        denom = jnp.sum(p_loc, axis=-1, keepdims=True) + jnp.sum(p_ctx, axis=-1, keepdims=True)
        o = (jnp.dot(p_loc.astype(BF), vw, preferred_element_type=F32)
             + jnp.dot(p_ctx.astype(BF), vc_ref[0, :, sl], preferred_element_type=F32)) / denom
        o_ref[0, :, sl] = _pair_merge(o, nq, first_half).astype(BF)


def _na_block_class(rb, n_blocks):
    return int(rb > 0) + int(rb == n_blocks - 1)


def _na_bias_tables(rpb, rows):
    n_blocks = rows // NA_QROWS
    col = np.arange(GRID_W)
    col_start = np.clip(col - NA_KW // 2, 0, GRID_W - NA_KW)
    col_in = (col[None, :] >= col_start[:, None]) & (col[None, :] < col_start[:, None] + NA_KW)
    dc_idx = np.clip(col[None, :] - col[:, None] + NA_KW - 1, 0, 2 * NA_KW - 2)
    per_class = {}
    for rb in range(n_blocks):
        ws = min(max(NA_QROWS * rb - NA_KH // 2, 0), rows - NA_WIN)
        r = NA_QROWS * rb + np.arange(NA_QROWS)
        kr = ws + np.arange(NA_WIN)
        r0 = np.clip(r - NA_KH // 2, 0, rows - NA_KH)
        row_in = (kr[None, :] >= r0[:, None]) & (kr[None, :] < r0[:, None] + NA_KH)
        dr_idx = np.clip(kr[None, :] - r[:, None] + NA_KH - 1, 0, 2 * NA_KH - 2)
        valid = row_in[:, None, :, None] & col_in[None, :, None, :]
        entry = (valid, dr_idx)
        cls = _na_block_class(rb, n_blocks)
        if cls in per_class:
            assert all(np.array_equal(a, b) for a, b in zip(per_class[cls], entry)), "row-block classes differ"
        per_class[cls] = entry
    nq, nk = NA_QROWS * GRID_W, NA_WIN * GRID_W
    onehot = (np.arange(2 * NA_KW - 1)[:, None] == dc_idx.reshape(1, -1)).astype(np.float32)
    by_col = jnp.dot(rpb.astype(F32).reshape(-1, 2 * NA_KW - 1), onehot, precision=lax.Precision.HIGHEST)
    by_col = by_col.reshape(NA_HEADS, 2 * NA_KH - 1, GRID_W, GRID_W)
    tables = []
    for cls in range(3):
        valid, dr_idx = per_class[cls]
        blocks = jnp.stack([by_col[:, int(dr)] for dr in dr_idx.reshape(-1)], axis=1)
        t = blocks.reshape(NA_HEADS, NA_QROWS, NA_WIN, GRID_W, GRID_W).transpose(0, 1, 3, 2, 4)
        tables.append(jnp.where(valid.reshape(nq, nk)[None], t.reshape(NA_HEADS, nq, nk), NEG_INF))
    return jnp.stack(tables)


def _neighbourhood_attention(q, k, v, kc, vc, rpb):
    b, n, w = q.shape
    c = kc.shape[1]
    rows = n // GRID_W
    assert rows % NA_QROWS == 0 and rows >= NA_WIN + 1
    n_blocks = rows // NA_QROWS
    bias = _na_bias_tables(rpb, rows)
    nq, nk = NA_QROWS * GRID_W, NA_WIN * GRID_W
    full = lambda length: pl.BlockSpec((1, length, w), lambda bi, rb: (bi, 0, 0))
    bias_spec = pl.BlockSpec(
        (1, NA_HEADS, nq, nk),
        lambda bi, rb: (jnp.minimum(rb, 1) + (rb == n_blocks - 1).astype(jnp.int32), 0, 0, 0))
    return pl.pallas_call(
        functools.partial(_na_kernel, rows=rows),
        grid=(b, n_blocks),
        in_specs=[pl.BlockSpec((1, nq, w), lambda bi, rb: (bi, rb, 0)), full(n), full(n), full(c), full(c), bias_spec],
        out_specs=pl.BlockSpec((1, nq, w), lambda bi, rb: (bi, rb, 0)),
        out_shape=jax.ShapeDtypeStruct((b, n, w), BF),
        compiler_params=_cparams(("parallel", "arbitrary")),
        name="neighbourhood_attention",
    )(q, k, v, kc, vc, bias)


def _ctx_attn_kernel(q_ref, k_ref, v_ref, o_ref):
    nq = q_ref.shape[1]
    first_half = lax.broadcasted_iota(jnp.int32, (nq, LANES), 1) < HEAD_DIM
    for p in range(NA_HEADS // 2):
        sl = slice(p * LANES, (p + 1) * LANES)
        qs = _pair_stack(q_ref[0, :, sl], first_half)
        s = lax.dot_general(qs, k_ref[0, :, sl], NT_DIMS, preferred_element_type=F32)
        e = jnp.exp(s - jnp.max(s, axis=-1, keepdims=True))
        o = jnp.dot(e.astype(BF), v_ref[0, :, sl], preferred_element_type=F32) / jnp.sum(e, axis=-1, keepdims=True)
        o_ref[0, :, sl] = _pair_merge(o, nq, first_half).astype(BF)


def _context_attention(q, k, v):
    b, c, w = q.shape
    spec = pl.BlockSpec((1, c, w), lambda bi: (bi, 0, 0))
    return pl.pallas_call(
        _ctx_attn_kernel,
        grid=(b,),
        in_specs=[spec] * 3,
        out_specs=spec,
        out_shape=jax.ShapeDtypeStruct((b, c, w), BF),
        compiler_params=_cparams(("parallel",)),
        name="context_attention",
    )(q, k, v)


def _out_ffn_even_kernel(x_ref, oa_ref, ob_ref, mod_ref, ng_ref, woa_ref, wob_ref, w1_ref, w3_ref, w2_ref, o_ref,
                         *, f_chunks):
    mix = (jnp.dot(oa_ref[0], woa_ref[...], preferred_element_type=F32)
           + jnp.dot(ob_ref[0], wob_ref[...], preferred_element_type=F32))
    x1 = x_ref[0] + mod_ref[0, 2:3, :] * mix
    h = _norm_mod(x1, ng_ref[...], mod_ref[0, 3:4, :], mod_ref[0, 4:5, :]).astype(BF)
    fc = w1_ref.shape[1] // f_chunks
    y = None
    for c in range(f_chunks):
        a = jnp.dot(h, w1_ref[:, c * fc:(c + 1) * fc], preferred_element_type=F32)
        g = jnp.dot(h, w3_ref[:, c * fc:(c + 1) * fc], preferred_element_type=F32)
        part = jnp.dot((_silu(a) * g).astype(BF), w2_ref[c * fc:(c + 1) * fc, :], preferred_element_type=F32)
        y = part if y is None else y + part
    o_ref[0] = x1 + mod_ref[0, 5:6, :] * y


def _out_ffn_even(x, oa, ob, mods, mod_row, norm_gain, w_out, w1, w3, w2):
    b, n, d = x.shape
    tm = _token_tile(n)
    wa = oa.shape[2]
    tok = lambda width: pl.BlockSpec((1, tm, width), lambda bi, i: (bi, i, 0))
    return pl.pallas_call(
        functools.partial(_out_ffn_even_kernel, f_chunks=2),
        grid=(b, n // tm),
        in_specs=[tok(d), tok(wa), tok(ob.shape[2]), _mod_spec(d, mod_row), _const_spec((1, d)),
                  _const_spec((wa, d)), _const_spec((w_out.shape[0] - wa, d)),
                  _const_spec(w1.shape), _const_spec(w3.shape), _const_spec(w2.shape)],
        out_specs=tok(d),
        out_shape=jax.ShapeDtypeStruct((b, n, d), F32),
        compiler_params=_cparams(("parallel", "parallel")),
        name="out_ffn_even",
    )(x, oa, ob, mods, norm_gain, w_out[:wa], w_out[wa:], w1, w3, w2)


def _rope(t, cos, sin_signed):
    width = t.shape[1]
    lane = lax.broadcasted_iota(jnp.int32, t.shape, 1)
    partner = jnp.where(lane % 2 == 0, pltpu.roll(t, width - 1, 1), pltpu.roll(t, 1, 1))
    reps = width // LANES
    return t * jnp.concatenate([cos] * reps, axis=1) + partner * jnp.concatenate([sin_signed] * reps, axis=1)


def _in_odd_kernel(*refs, is_latent):
    if is_latent:
        (x_ref, mod_ref, ng_ref, w_ref, qg_ref, kg_ref, bdq_ref, bdk_ref, cos_ref, sin_ref,
         q_ref, kk_ref, vv_ref, u_ref) = refs
    else:
        x_ref, mod_ref, ng_ref, w_ref, kg_ref, bdk_ref, kk_ref, vv_ref, u_ref = refs
    h = _norm_mod(x_ref[0], ng_ref[...], mod_ref[0, 0:1, :], mod_ref[0, 1:2, :])
    p = jnp.dot(h.astype(BF), w_ref[...], preferred_element_type=F32)
    c0 = GQ_WIDTH if is_latent else 0
    kk = _head_norm(p[:, c0:c0 + GKV_PAD], bdk_ref[...], kg_ref[...])
    if is_latent:
        q = _head_norm(p[:, 0:GQ_WIDTH], bdq_ref[...], qg_ref[...])
        q_ref[0] = _rope(q, cos_ref[...], sin_ref[...]).astype(BF)
        kk = _rope(kk, cos_ref[...], sin_ref[...])
    kk_ref[0] = kk.astype(BF)
    vv_ref[0] = p[:, c0 + GKV_PAD:c0 + 2 * GKV_PAD].astype(BF)
    u_ref[...] = p[:, c0 + 2 * GKV_PAD:].astype(BF)


def _in_odd(x, mods, mod_row, norm_gain, w_ext, q_gain, k_gain, bdq, bdk, cos, sin_signed, is_latent):
    b, n, d = x.shape
    tm = _token_tile(n)
    tok = lambda width: pl.BlockSpec((1, tm, width), lambda bi, i: (bi, i, 0))
    vec = lambda width: _const_spec((1, width))
    if is_latent:
        rope_spec = pl.BlockSpec((tm, LANES), lambda bi, i: (i, 0))
        in_specs = [tok(d), _mod_spec(d, mod_row), vec(d), _const_spec(w_ext.shape), vec(GQ_WIDTH), vec(GKV_PAD),
                    _const_spec(bdq.shape), _const_spec(bdk.shape), rope_spec, rope_spec]
        args = (x, mods, norm_gain, w_ext, q_gain, k_gain, bdq, bdk, cos, sin_signed)
        widths = [GQ_WIDTH, GKV_PAD, GKV_PAD]
    else:
        in_specs = [tok(d), _mod_spec(d, mod_row), vec(d), _const_spec(w_ext.shape), vec(GKV_PAD),
                    _const_spec(bdk.shape)]
        args = (x, mods, norm_gain, w_ext, k_gain, bdk)
        widths = [GKV_PAD, GKV_PAD]
    u_spec = pl.BlockSpec((tm, S5_WIDTH), lambda bi, i: (i, bi))
    return pl.pallas_call(
        functools.partial(_in_odd_kernel, is_latent=is_latent),
        grid=(b, n // tm),
        in_specs=in_specs,
        out_specs=[tok(wd) for wd in widths] + [u_spec],
        out_shape=[jax.ShapeDtypeStruct((b, n, wd), BF) for wd in widths]
        + [jax.ShapeDtypeStruct((n, b * S5_WIDTH), BF)],
        compiler_params=_cparams(("parallel", "parallel")),
        name="in_odd_latent" if is_latent else "in_odd_context",
    )(*args)


def _rope_tables(n):
    t = jnp.arange(n)
    row = (t // GRID_W).astype(F32)
    col = (t % GRID_W).astype(F32)
    n_axis = HEAD_DIM // 4
    freqs = ROPE_THETA ** (-jnp.arange(n_axis, dtype=F32) / n_axis)
    ang = jnp.concatenate([row[:, None] * freqs, col[:, None] * freqs], axis=-1)
    ang = jnp.repeat(ang, 2, axis=-1)
    ang = jnp.concatenate([ang, ang], axis=-1)
    sign = jnp.where(jnp.arange(LANES) % 2 == 0, -1.0, 1.0).astype(F32)
    return jnp.cos(ang), jnp.sin(ang) * sign


def _gqa_kernel(q_ref, kk_ref, vv_ref, o_ref):
    tq = q_ref.shape[1]
    first_half = lax.broadcasted_iota(jnp.int32, (tq, LANES), 1) < HEAD_DIM
    group = GQ_HEADS // GKV_HEADS
    for j in range(GKV_HEADS):
        kj = kk_ref[0, :, j * LANES:(j + 1) * LANES]
        vj = vv_ref[0, :, j * LANES:(j + 1) * LANES]
        base = j * group * HEAD_DIM
        qs = jnp.concatenate(
            [_pair_stack(q_ref[0, :, base + i * LANES:base + (i + 1) * LANES], first_half)
             for i in range(group // 2)], axis=0)
        s = lax.dot_general(qs, kj, NT_DIMS, preferred_element_type=F32)
        e = jnp.exp(s - jnp.max(s, axis=-1, keepdims=True))
        o = jnp.dot(e.astype(BF), vj, preferred_element_type=F32) / jnp.sum(e, axis=-1, keepdims=True)
        for i in range(group // 2):
            pair = _pair_merge(o[2 * i * tq:(2 * i + 2) * tq], tq, first_half)
            o_ref[0, :, base + i * LANES:base + (i + 1) * LANES] = pair.astype(BF)


def _gqa(q, kk, vv):
    b, n, w = q.shape
    nk = kk.shape[1]
    tq = 128
    kv_spec = pl.BlockSpec((1, nk, GKV_PAD), lambda bi, i: (bi, 0, 0))
    return pl.pallas_call(
        _gqa_kernel,
        grid=(b, n // tq),
        in_specs=[pl.BlockSpec((1, tq, w), lambda bi, i: (bi, i, 0)), kv_spec, kv_spec],
        out_specs=pl.BlockSpec((1, tq, w), lambda bi, i: (bi, i, 0)),
        out_shape=jax.ShapeDtypeStruct((b, n, w), BF),
        compiler_params=_cparams(("parallel", "arbitrary")),
        name="gqa",
    )(q, kk, vv)


def _s5_kernel(u_ref, bbd_ref, lre_ref, lim_ref, cbd_ref, y_ref, bu_ref, st_ref, *, steps, bp):
    backward = pl.program_id(0) == 1

    @pl.when(pl.program_id(1) == 0)
    def _():
        st_ref[...] = jnp.zeros_like(st_ref)

    bu_ref[...] = jnp.dot(u_ref[...], bbd_ref[0], preferred_element_type=F32)
    lc = 2 * LANES
    for j in range(S5_NP // lc):
        re_sl = slice(j * lc, (j + 1) * lc)
        im_sl = slice(S5_NP + j * lc, S5_NP + (j + 1) * lc)
        lre = jnp.broadcast_to(lre_ref[0, :, re_sl], (bp, lc))
        lim = jnp.broadcast_to(lim_ref[0, :, re_sl], (bp, lc))

        def step(s, carry):
            xre, xim = carry
            t = jnp.where(backward, steps - 1 - s, s)
            r0 = pl.multiple_of(t * bp, bp)
            nre = lre * xre - lim * xim + bu_ref[pl.ds(r0, bp), re_sl]
            nim = lre * xim + lim * xre + bu_ref[pl.ds(r0, bp), im_sl]
            bu_ref[pl.ds(r0, bp), re_sl] = nre
            bu_ref[pl.ds(r0, bp), im_sl] = nim
            return nre, nim

        xre, xim = lax.fori_loop(0, steps, step, (st_ref[:, re_sl], st_ref[:, im_sl]), unroll=2)
        st_ref[:, re_sl] = xre
        st_ref[:, im_sl] = xim
    y_ref[0] = jnp.dot(bu_ref[...].astype(BF), cbd_ref[0], preferred_element_type=F32)


def _s5_params(a_re, a_im, log_step, b_re, b_im, c_re, c_im):
    a_re, a_im = a_re.astype(F32), a_im.astype(F32)
    step = jnp.exp(log_step.astype(F32))[..., None]
    mag = jnp.exp(a_re * step)
    lre, lim = mag * jnp.cos(a_im * step), mag * jnp.sin(a_im * step)
    den = a_re * a_re + a_im * a_im
    kre = ((lre - 1.0) * a_re + lim * a_im) / den
    kim = (lim * a_re - (lre - 1.0) * a_im) / den
    bre = kre[..., None] * b_re - kim[..., None] * b_im
    bim = kre[..., None] * b_im + kim[..., None] * b_re
    eye = jnp.eye(S5_GROUPS, dtype=F32)

    def in_blockdiag(m):
        return jnp.einsum('dgpn,gh->dgnhp', m, eye).reshape(2, S5_WIDTH, S5_NP)

    def out_blockdiag(m):
        return jnp.einsum('dgnp,gh->dgphn', m, eye).reshape(2, S5_NP, S5_WIDTH)

    bbd = jnp.concatenate([in_blockdiag(bre), in_blockdiag(bim)], axis=2).astype(BF)
    cbd = jnp.concatenate([out_blockdiag(c_re.astype(F32)), -out_blockdiag(c_im.astype(F32))], axis=1).astype(BF)
    return bbd, lre.reshape(2, 1, S5_NP), lim.reshape(2, 1, S5_NP), cbd


def _s5_scan(u_seq, n_ctx_rows, bbd, lre, lim, cbd, bp):
    rows = u_seq.shape[0]
    steps = 32
    rb = steps * bp
    assert rows % rb == 0 and n_ctx_rows % rb == 0
    nc = n_ctx_rows // rb
    nl = rows // rb - nc

    def in_block(d, i):
        rev = jnp.where(i < nc, nc - 1 - i, nc + nl - 1 - (i - nc))
        return jnp.where(d == 0, i, rev)

    def out_block(d, i):
        k = jnp.maximum(i - nc, 0)
        return jnp.where(d == 0, k, nl - 1 - k)

    per_dir = lambda shape: pl.BlockSpec((1,) + shape, lambda d, i: (d, 0, 0))
    return pl.pallas_call(
        functools.partial(_s5_kernel, steps=steps, bp=bp),
        grid=(2, nc + nl),
        in_specs=[pl.BlockSpec((rb, S5_WIDTH), lambda d, i: (in_block(d, i), 0)), per_dir((S5_WIDTH, 2 * S5_NP)),
                  per_dir((1, S5_NP)), per_dir((1, S5_NP)), per_dir((2 * S5_NP, S5_WIDTH))],
        out_specs=pl.BlockSpec((1, rb, S5_WIDTH), lambda d, i: (d, out_block(d, i), 0)),
        out_shape=jax.ShapeDtypeStruct((2, nl * rb, S5_WIDTH), F32),
        scratch_shapes=[pltpu.VMEM((rb, 2 * S5_NP), F32), pltpu.VMEM((bp, 2 * S5_NP), F32)],
        compiler_params=_cparams(("arbitrary", "arbitrary")),
        name="s5_scan",
    )(u_seq, bbd, lre, lim, cbd)


def _s5_mixer_latent(u_lat, u_ctx, b, params):
    n, c = u_lat.shape[0], u_ctx.shape[0]
    bp = -(-b // 8) * 8
    seq = jnp.concatenate([u_ctx, u_lat], axis=0).reshape(c + n, b, S5_WIDTH)
    seq = jnp.pad(seq, ((0, 0), (0, bp - b), (0, 0))).reshape((c + n) * bp, S5_WIDTH)
    y = _s5_scan(seq, c * bp, *params, bp)
    return (y[0] + y[1]).reshape(n, bp, S5_WIDTH)[:, :b].reshape(n, b * S5_WIDTH)


def _out_odd_kernel(x_ref, oa_ref, ys_ref, u_ref, mod_ref, ng_ref, d_ref, gw_ref, gb_ref, woa_ref, wob_ref,
                    rw_ref, rb_ref, x1_ref, h_ref, gate_ref):
    y = ys_ref[...] + d_ref[...] * u_ref[...].astype(F32)
    gl = 0.5 * y * (1.0 + jnp.tanh(np.sqrt(2.0 / np.pi) * (y + 0.044715 * (y * y * y))))
    z = gl * jax.nn.sigmoid(jnp.dot(gl.astype(BF), gw_ref[...], preferred_element_type=F32) + gb_ref[...])
    mix = (jnp.dot(oa_ref[0], woa_ref[...], preferred_element_type=F32)
           + jnp.dot(z.astype(BF), wob_ref[...], preferred_element_type=F32))
    x1 = x_ref[0] + mod_ref[0, 2:3, :] * mix
    x1_ref[0] = x1
    h = _norm_mod(x1, ng_ref[...], mod_ref[0, 3:4, :], mod_ref[0, 4:5, :]).astype(BF)
    h_ref[0] = h
    logits = jnp.dot(h, rw_ref[...], preferred_element_type=F32) + rb_ref[...]
    lane = lax.broadcasted_iota(jnp.int32, logits.shape, 1)
    m1 = jnp.max(logits, axis=-1, keepdims=True)
    i1 = jnp.min(jnp.where(logits == m1, lane, LANES), axis=-1, keepdims=True)
    rest = jnp.where(lane == i1, -jnp.inf, logits)
    m2 = jnp.max(rest, axis=-1, keepdims=True)
    i2 = jnp.min(jnp.where(rest == m2, lane, LANES), axis=-1, keepdims=True)
    e2 = jnp.exp(m2 - m1)
    gate_ref[0] = jnp.where(lane == i1, 1.0 / (1.0 + e2), 0.0) + jnp.where(lane == i2, e2 / (1.0 + e2), 0.0)


def _out_odd(x, oa, ys, u, mods, norm_gain, d_skip, glu_w, glu_b, w_out, router_w, router_b):
    b, n, d = x.shape
    tm = _token_tile(n)
    wa = oa.shape[2]
    tok = lambda width: pl.BlockSpec((1, tm, width), lambda bi, i: (bi, i, 0))
    vec = lambda width: _const_spec((1, width))
    tmajor = pl.BlockSpec((tm, S5_WIDTH), lambda bi, i: (i, bi))
    return pl.pallas_call(
        _out_odd_kernel,
        grid=(b, n // tm),
        in_specs=[tok(d), tok(wa), tmajor, tmajor, _mod_spec(d, None), vec(d), vec(S5_WIDTH),
                  _const_spec(glu_w.shape), vec(S5_WIDTH), _const_spec((wa, d)), _const_spec((S5_WIDTH, d)),
                  _const_spec(router_w.shape), vec(LANES)],
        out_specs=[tok(d), tok(d), tok(LANES)],
        out_shape=[jax.ShapeDtypeStruct((b, n, d), F32), jax.ShapeDtypeStruct((b, n, d), BF),
                   jax.ShapeDtypeStruct((b, n, LANES), F32)],
        compiler_params=_cparams(("parallel", "parallel")),
        name="out_odd",
    )(x, oa, ys, u, mods, norm_gain, d_skip, glu_w, glu_b, w_out[:wa], w_out[wa:], router_w, router_b)


MOE_ROWS = 256
MOE_F_CHUNKS = 4


def _moe_kernel(h_ref, gate_ref, x1_ref, mod_ref, w1_ref, w3_ref, w2_ref, o_ref,
                xs_ref, ys_ref, rank_ref, rankt_ref, cnt_ref):
    e = pl.program_id(1)
    q = pl.program_id(2)
    tb, d = h_ref.shape[1:]
    g_rows = MOE_ROWS
    half = g_rows // 2

    @pl.when((e == 0) & (q == 0))
    def _init():
        o_ref[...] = jnp.zeros_like(o_ref)
        tri = (lax.broadcasted_iota(jnp.int32, (g_rows, g_rows), 0)
               > lax.broadcasted_iota(jnp.int32, (g_rows, g_rows), 1)).astype(BF)
        offset = jnp.zeros((1, LANES), F32)
        for c in range(tb // g_rows):
            sel = gate_ref[0, c * g_rows:(c + 1) * g_rows, :] > 0.0
            sel_f = sel.astype(F32)
            earlier = jnp.dot(tri, sel_f.astype(BF), preferred_element_type=F32) + offset
            rank_ref[c * g_rows:(c + 1) * g_rows, :] = jnp.where(sel, earlier, -1.0)
            offset = offset + jnp.sum(sel_f, axis=0, keepdims=True)
        rankt_ref[...] = rank_ref[...].T
        lane = lax.broadcasted_iota(jnp.int32, (1, LANES), 1)
        for k in range(N_EXPERTS):
            cnt_ref[k] = jnp.sum(jnp.where(lane == k, offset, 0.0)).astype(jnp.int32)

    n = cnt_ref[e]
    n_groups = (n + g_rows - 1) // g_rows
    n_full = n // g_rows
    rem = n - n_full * g_rows

    @pl.when(q == 0)
    def _gather():
        rank_row = rankt_ref[pl.ds(e, 1), :]
        row_id = lax.broadcasted_iota(jnp.int32, (g_rows, tb), 0).astype(F32)

        def body(g, carry):
            r0 = pl.multiple_of(g * g_rows, g_rows)
            onehot = (rank_row == row_id + r0.astype(F32)).astype(BF)
            xs_ref[pl.ds(r0, g_rows), :] = jnp.dot(onehot, h_ref[0], preferred_element_type=F32).astype(BF)
            return carry

        lax.fori_loop(0, n_groups, body, 0)

    def ffn_tile(r0, size):
        x = xs_ref[pl.ds(r0, size), :]
        a = jnp.dot(x, w1_ref[0], preferred_element_type=F32)
        g = jnp.dot(x, w3_ref[0], preferred_element_type=F32)
        y = jnp.dot((_silu(a) * g).astype(BF), w2_ref[0], preferred_element_type=F32)

        @pl.when(q == 0)
        def _():
            ys_ref[pl.ds(r0, size), :] = y

        @pl.when(q > 0)
        def _():
            ys_ref[pl.ds(r0, size), :] += y

    def full_tile(g, carry):
        ffn_tile(pl.multiple_of(g * g_rows, g_rows), g_rows)
        return carry

    lax.fori_loop(0, n_full, full_tile, 0)
    tail = pl.multiple_of(n_full * g_rows, g_rows)

    @pl.when(rem > half)
    def _():
        ffn_tile(tail, g_rows)

    @pl.when((rem > 0) & (rem <= half))
    def _():
        ffn_tile(tail, half)

        @pl.when(q == 0)
        def _():
            ys_ref[pl.ds(pl.multiple_of(tail + half, half), half), :] = jnp.zeros((half, d), F32)

    @pl.when(q == pl.num_programs(2) - 1)
    def _combine():
        mine = lax.broadcasted_iota(jnp.int32, (tb, LANES), 1) == e
        gate_col = jnp.sum(jnp.where(mine, gate_ref[0], 0.0), axis=-1, keepdims=True)
        rank_col = jnp.sum(jnp.where(mine, rank_ref[...], 0.0), axis=-1, keepdims=True)
        tc = 512
        col_id = lax.broadcasted_iota(jnp.int32, (tc, g_rows), 1).astype(F32)
        for c in range(tb // tc):
            rows = slice(c * tc, (c + 1) * tc)
            rc = rank_col[rows]

            def body(g, acc):
                r0 = pl.multiple_of(g * g_rows, g_rows)
                onehot = (rc == col_id + r0.astype(F32)).astype(BF)
                return acc + jnp.dot(onehot, ys_ref[pl.ds(r0, g_rows), :].astype(BF), preferred_element_type=F32)

            acc = lax.fori_loop(0, n_groups, body, jnp.zeros((tc, d), F32))
            o_ref[0, rows, :] += gate_col[rows] * acc

        @pl.when(e == pl.num_programs(1) - 1)
        def _():
            o_ref[0] = x1_ref[0] + mod_ref[0, 5:6, :] * o_ref[0]


def _moe(h, gates, x1, mods, w1, w3, w2):
    b, n, d = h.shape
    n_exp, _, f = w1.shape
    assert n_exp == N_EXPERTS and n % 512 == 0
    fc = f // MOE_F_CHUNKS
    per_batch = lambda width: pl.BlockSpec((1, n, width), lambda bi, e, q: (bi, 0, 0), pipeline_mode=pl.Buffered(1))
    return pl.pallas_call(
        _moe_kernel,
        grid=(b, n_exp, MOE_F_CHUNKS),
        in_specs=[per_batch(d), per_batch(LANES), per_batch(d),
                  pl.BlockSpec((1, N_MOD, d), lambda bi, e, q: (bi, 0, 0)),
                  pl.BlockSpec((1, d, fc), lambda bi, e, q: (e, 0, q)),
                  pl.BlockSpec((1, d, fc), lambda bi, e, q: (e, 0, q)),
                  pl.BlockSpec((1, fc, d), lambda bi, e, q: (e, q, 0))],
        out_specs=per_batch(d),
        out_shape=jax.ShapeDtypeStruct((b, n, d), F32),
        scratch_shapes=[pltpu.VMEM((n, d), BF), pltpu.VMEM((n, d), F32), pltpu.VMEM((n, LANES), F32),
                        pltpu.VMEM((LANES, n), F32), pltpu.SMEM((N_EXPERTS,), jnp.int32)],
        compiler_params=_cparams(("arbitrary", "arbitrary", "arbitrary")),
        name="moe",
    )(h, gates, x1, mods, w1, w3, w2)


def _head_blockdiag(width):
    head = np.arange(width) // HEAD_DIM
    return jnp.asarray((head[:, None] == head[None, :]).astype(np.float32) / HEAD_DIM, dtype=BF)


def _tile_gain(gain, width, scale=1.0):
    return (jnp.tile(gain.astype(F32), width // HEAD_DIM) * scale).reshape(1, width)


def _even_layer(x, xc, mods, ctx_row, norm_mix, norm_ffn, w_in, q_gain, k_gain, rpb, w_out, w1, w3, w2):
    d = x.shape[2]
    bd = _head_blockdiag(NA_WIDTH)
    w_in = w_in.astype(BF)
    qg = _tile_gain(q_gain, NA_WIDTH, HEAD_DIM ** -0.5)
    kg = _tile_gain(k_gain, NA_WIDTH)
    ng_mix, ng_ffn = norm_mix.reshape(1, d), norm_ffn.reshape(1, d)
    u, q, k, v = _in_even(x, mods, None, ng_mix, w_in, qg, kg, bd)
    uc, qc, kc, vc = _in_even(xc, mods, ctx_row, ng_mix, w_in, qg, kg, bd)
    o_na = _neighbourhood_attention(q, k, v, kc, vc, rpb)
    o_ctx = _context_attention(qc, kc, vc)
    w_out, w1, w3, w2 = (t.astype(BF) for t in (w_out, w1, w3, w2))
    x = _out_ffn_even(x, _fourier(u), o_na, mods, None, ng_ffn, w_out, w1, w3, w2)
    xc = _out_ffn_even(xc, _fourier(uc), o_ctx, mods, ctx_row, ng_ffn, w_out, w1, w3, w2)
    return x, xc


def _odd_layer(x, xc, mods, ctx_row, norm_mix, norm_ffn, w_in, q_gain, k_gain, s5, d_skip, glu_w, glu_b, w_out,
               router_w, router_b, ew1, ew3, ew2):
    b, n, d = x.shape
    kvw = GKV_HEADS * HEAD_DIM
    wq, wk, wv, wu = jnp.split(w_in, [GQ_WIDTH, GQ_WIDTH + kvw, GQ_WIDTH + 2 * kvw], axis=1)

    def twice(w):
        return jnp.repeat(w.reshape(d, GKV_HEADS, 1, HEAD_DIM), 2, axis=2).reshape(d, GKV_PAD)

    w_lat = jnp.concatenate([wq, twice(wk), twice(wv), wu], axis=1).astype(BF)
    w_ctx = jnp.concatenate([twice(wk), twice(wv), wu], axis=1).astype(BF)
    qg = _tile_gain(q_gain, GQ_WIDTH, HEAD_DIM ** -0.5)
    kg = _tile_gain(k_gain, GKV_PAD)
    bdq, bdk = _head_blockdiag(GQ_WIDTH), _head_blockdiag(GKV_PAD)
    cos, sin_signed = _rope_tables(n)
    ng_mix, ng_ffn = norm_mix.reshape(1, d), norm_ffn.reshape(1, d)
    q, kk, vv, u = _in_odd(x, mods, None, ng_mix, w_lat, qg, kg, bdq, bdk, cos, sin_signed, True)
    kkc, vvc, uc = _in_odd(xc, mods, ctx_row, ng_mix, w_ctx, None, kg, None, bdk, None, None, False)
    o_attn = _gqa(q, jnp.concatenate([kkc, kk], axis=1), jnp.concatenate([vvc, vv], axis=1))
    y_ssm = _s5_mixer_latent(u, uc, b, _s5_params(*s5))
    rw = jnp.pad(router_w, ((0, 0), (0, LANES - N_EXPERTS))).astype(BF)
    rb = jnp.pad(router_b.astype(F32), (0, LANES - N_EXPERTS), constant_values=NEG_INF).reshape(1, LANES)
    x1, h, gates = _out_odd(x, o_attn, y_ssm, u, mods, ng_ffn, d_skip.reshape(1, S5_WIDTH).astype(F32),
                            glu_w.astype(BF), glu_b.reshape(1, S5_WIDTH).astype(F32), w_out.astype(BF), rw, rb)
    return _moe(h, gates, x1, mods, ew1.astype(BF), ew3.astype(BF), ew2.astype(BF))


def kernel(x, c, ctx, c_ctx, ev_mod_w, ev_mod_b, ev_norm_mix, ev_norm_ffn, ev_w_in, ev_q_gain, ev_k_gain, ev_rpb,
           ev_w_out, ev_ffn_w1, ev_ffn_w3, ev_ffn_w2, od_mod_w, od_mod_b, od_norm_mix, od_norm_ffn, od_w_in,
           od_q_gain, od_k_gain, od_s5_a_re, od_s5_a_im, od_s5_log_step, od_s5_b_re, od_s5_b_im, od_s5_c_re,
           od_s5_c_im, od_s5_d, od_s5_glu_w, od_s5_glu_b, od_w_out, od_router_w, od_router_b, od_exp_w1, od_exp_w3,
           od_exp_w2):
    assert ev_mod_w.shape[0] == 1 and od_mod_w.shape[0] == 1, "one even and one odd layer"
    b, n, d = x.shape
    rows = -(-(b + 1) // 8) * 8
    cond = jnp.zeros((rows, d), F32).at[:b].set(c).at[b].set(c_ctx)
    mods_even = _modulation(cond, ev_mod_w[0], ev_mod_b[0])
    mods_odd = _modulation(cond, od_mod_w[0], od_mod_b[0])
    x, xc = _even_layer(x, ctx, mods_even, b, ev_norm_mix[0], ev_norm_ffn[0], ev_w_in[0], ev_q_gain[0], ev_k_gain[0],
                        ev_rpb[0], ev_w_out[0], ev_ffn_w1[0], ev_ffn_w3[0], ev_ffn_w2[0])
    s5 = (od_s5_a_re[0], od_s5_a_im[0], od_s5_log_step[0], od_s5_b_re[0], od_s5_b_im[0], od_s5_c_re[0], od_s5_c_im[0])
    return _odd_layer(x, xc, mods_odd, b, od_norm_mix[0], od_norm_ffn[0], od_w_in[0], od_q_gain[0], od_k_gain[0],
                      s5, od_s5_d[0], od_s5_glu_w[0], od_s5_glu_b[0], od_w_out[0], od_router_w[0], od_router_b[0],
                      od_exp_w1[0], od_exp_w3[0], od_exp_w2[0])
```

```python
import functools

import numpy as np
import jax
import jax.numpy as jnp
from jax import lax
from jax.experimental import pallas as pl
from jax.experimental.pallas import tpu as pltpu

BF = jnp.bfloat16
F32 = jnp.float32

EPS = 1e-6
NEG_INF = -1e30
GRID_W = 64
LANES = 128
MXU_TILE = 256
HEAD_DIM = 64
FN_GROUPS, FN_GROUP_DIM = 4, 128
FN_WIDTH = FN_GROUPS * FN_GROUP_DIM
NA_HEADS = 8
NA_WIDTH = NA_HEADS * HEAD_DIM
NA_KH, NA_KW = 8, 16
NA_QROWS = 4
NA_WIN = NA_QROWS + NA_KH - 1
GQ_HEADS, GKV_HEADS = 12, 3
GQ_WIDTH = GQ_HEADS * HEAD_DIM
GKV_PAD = GKV_HEADS * LANES
S5_GROUPS, S5_GROUP_DIM, S5_STATE = 16, 16, 64
S5_WIDTH = S5_GROUPS * S5_GROUP_DIM
S5_NP = S5_GROUPS * S5_STATE
N_EXPERTS = 8
N_MOD = 6
ROPE_THETA = 10000.0
VMEM_LIMIT = 56 * 1024 * 1024

NT_DIMS = (((1,), (1,)), ((), ()))


def _cparams(sem):
    return pltpu.CompilerParams(dimension_semantics=sem, vmem_limit_bytes=VMEM_LIMIT)


def _const_spec(shape):
    nd = len(shape)
    return pl.BlockSpec(shape, lambda *_: (0,) * nd, pipeline_mode=pl.Buffered(1))


def _token_tile(n):
    return 512 if n % 512 == 0 else 256


def _mod_spec(d, row):
    if row is None:
        return pl.BlockSpec((1, N_MOD, d), lambda b, i: (b, 0, 0))
    return pl.BlockSpec((1, N_MOD, d), lambda b, i: (row, 0, 0))


def _norm_mod(x, gain, shift, scale):
    ms = jnp.mean(x * x, axis=-1, keepdims=True)
    return (x * lax.rsqrt(ms + EPS) * gain) * (1.0 + scale) + shift


def _head_norm(t, blockdiag, gain):
    ms = jnp.dot((t * t).astype(BF), blockdiag, preferred_element_type=F32)
    return t * lax.rsqrt(ms + EPS) * gain


def _silu(a):
    return a * jax.nn.sigmoid(a)


def _mod_kernel(c_ref, w_ref, b_ref, o_ref):
    a = _silu(c_ref[...]).astype(BF)
    o_ref[...] = jnp.dot(a, w_ref[...].astype(BF), preferred_element_type=F32) + b_ref[...]


def _modulation(cond, w, b):
    r, d = cond.shape
    n = w.shape[1]
    tn = n // 4
    out = pl.pallas_call(
        _mod_kernel,
        grid=(n // tn,),
        in_specs=[pl.BlockSpec((r, d), lambda j: (0, 0)),
                  pl.BlockSpec((d, tn), lambda j: (0, j)),
                  pl.BlockSpec((1, tn), lambda j: (0, j))],
        out_specs=pl.BlockSpec((r, tn), lambda j: (0, j)),
        out_shape=jax.ShapeDtypeStruct((r, n), F32),
        compiler_params=_cparams(("arbitrary",)),
        name="modulation",
    )(cond, w, b.reshape(1, n))
    return out.reshape(r, N_MOD, d)


def _in_even_kernel(x_ref, mod_ref, ng_ref, w_ref, qg_ref, kg_ref, bd_ref, u_ref, q_ref, k_ref, v_ref):
    h = _norm_mod(x_ref[0], ng_ref[...], mod_ref[0, 0:1, :], mod_ref[0, 1:2, :])
    p = jnp.dot(h.astype(BF), w_ref[...], preferred_element_type=F32)
    bd = bd_ref[...]
    w = FN_WIDTH
    u_ref[0] = p[:, 0:w].astype(BF)
    q_ref[0] = _head_norm(p[:, w:2 * w], bd, qg_ref[...]).astype(BF)
    k_ref[0] = _head_norm(p[:, 2 * w:3 * w], bd, kg_ref[...]).astype(BF)
    v_ref[0] = p[:, 3 * w:4 * w].astype(BF)


def _in_even(x, mods, mod_row, norm_gain, w_in, q_gain, k_gain, blockdiag):
    b, n, d = x.shape
    tm = _token_tile(n)
    wout = FN_WIDTH
    tok = lambda width: pl.BlockSpec((1, tm, width), lambda bi, i: (bi, i, 0))
    return pl.pallas_call(
        _in_even_kernel,
        grid=(b, n // tm),
        in_specs=[tok(d), _mod_spec(d, mod_row), _const_spec((1, d)), _const_spec(w_in.shape),
                  _const_spec((1, wout)), _const_spec((1, wout)), _const_spec(blockdiag.shape)],
        out_specs=[tok(wout)] * 4,
        out_shape=[jax.ShapeDtypeStruct((b, n, wout), BF)] * 4,
        compiler_params=_cparams(("parallel", "parallel")),
        name="in_even",
    )(x, mods, norm_gain, w_in, q_gain, k_gain, blockdiag)


def _fourier_kernel(u_ref, cm_ref, ml_ref, o_ref, z_ref):
    n = u_ref.shape[1]
    gd = FN_GROUP_DIM
    for g in range(FN_GROUPS):
        z = jnp.dot(u_ref[0, :, g * gd:(g + 1) * gd], cm_ref[...], preferred_element_type=F32)
        z_ref[0:n, g * gd:(g + 1) * gd] = z[:, :gd].astype(BF)
        z_ref[n:2 * n, g * gd:(g + 1) * gd] = z[:, gd:].astype(BF)
    o_ref[0] = jnp.dot(ml_ref[...], z_ref[...], preferred_element_type=F32).astype(BF)


def _dft_tables(n):
    def cos_sin(size):
        j = jnp.arange(size, dtype=jnp.int32)
        ang = ((j[:, None] * j[None, :]) % size).astype(F32) * (2.0 * np.pi / size)
        return jnp.cos(ang), jnp.sin(ang)
    cc, sc = cos_sin(FN_GROUP_DIM)
    cl, sl = cos_sin(n)
    scale = 1.0 / np.sqrt(n * FN_GROUP_DIM)
    return (jnp.concatenate([cc, sc], axis=1).astype(BF),
            (jnp.concatenate([cl, -sl], axis=1) * scale).astype(BF))


def _fourier(u):
    b, n, w = u.shape
    cm, ml = _dft_tables(n)
    return pl.pallas_call(
        _fourier_kernel,
        grid=(b,),
        in_specs=[pl.BlockSpec((1, n, w), lambda bi: (bi, 0, 0)), _const_spec(cm.shape), _const_spec(ml.shape)],
        out_specs=pl.BlockSpec((1, n, w), lambda bi: (bi, 0, 0)),
        out_shape=jax.ShapeDtypeStruct((b, n, w), BF),
        scratch_shapes=[pltpu.VMEM((2 * n, w), BF)],
        compiler_params=_cparams(("parallel",)),
        name="fourier",
    )(u, cm, ml)


def _pair_stack(qp, first_half):
    zero = jnp.zeros_like(qp)
    return jnp.concatenate([jnp.where(first_half, qp, zero), jnp.where(first_half, zero, qp)], axis=0)


def _pair_merge(o, nq, first_half):
    return jnp.where(first_half, o[:nq], o[nq:])


def _na_kernel(q_ref, k_ref, v_ref, kc_ref, vc_ref, bias_ref, o_ref, *, rows):
    rb = pl.program_id(1)
    ws = jnp.minimum(jnp.maximum(NA_QROWS * rb - NA_KH // 2, 0), rows - NA_WIN)
    start = pl.multiple_of(ws * GRID_W, GRID_W)
    nq = NA_QROWS * GRID_W
    nk = NA_WIN * GRID_W
    first_half = lax.broadcasted_iota(jnp.int32, (nq, LANES), 1) < HEAD_DIM
    for p in range(NA_HEADS // 2):
        sl = slice(p * LANES, (p + 1) * LANES)
        qs = _pair_stack(q_ref[0, :, sl], first_half)
        kw = k_ref[0, pl.ds(start, nk), sl]
        vw = v_ref[0, pl.ds(start, nk), sl]
        bias = jnp.concatenate([bias_ref[0, 2 * p], bias_ref[0, 2 * p + 1]], axis=0)
        s_loc = lax.dot_general(qs, kw, NT_DIMS, preferred_element_type=F32) + bias
        s_ctx = lax.dot_general(qs, kc_ref[0, :, sl], NT_DIMS, preferred_element_type=F32)
        m = jnp.maximum(jnp.max(s_loc, axis=-1, keepdims=True), jnp.max(s_ctx, axis=-1, keepdims=True))
        p_loc = jnp.exp(s_loc - m)
        p_ctx = jnp.exp(s_ctx - m)
        denom = jnp.sum(p_loc, axis=-1, keepdims=True) + jnp.sum(p_ctx, axis=-1, keepdims=True)
        o = (jnp.dot(p_loc.astype(BF), vw, preferred_element_type=F32)
             + jnp.dot(p_ctx.astype(BF), vc_ref[0, :, sl], preferred_element_type=F32)) / denom
        o_ref[0, :, sl] = _pair_merge(o, nq, first_half).astype(BF)


def _na_block_class(rb, n_blocks):
    return int(rb > 0) + int(rb == n_blocks - 1)


def _na_bias_tables(rpb, rows):
    n_blocks = rows // NA_QROWS
    col = np.arange(GRID_W)
    col_start = np.clip(col - NA_KW // 2, 0, GRID_W - NA_KW)
    col_in = (col[None, :] >= col_start[:, None]) & (col[None, :] < col_start[:, None] + NA_KW)
    dc_idx = np.clip(col[None, :] - col[:, None] + NA_KW - 1, 0, 2 * NA_KW - 2)
    per_class = {}
    for rb in range(n_blocks):
        ws = min(max(NA_QROWS * rb - NA_KH // 2, 0), rows - NA_WIN)
        r = NA_QROWS * rb + np.arange(NA_QROWS)
        kr = ws + np.arange(NA_WIN)
        r0 = np.clip(r - NA_KH // 2, 0, rows - NA_KH)
        row_in = (kr[None, :] >= r0[:, None]) & (kr[None, :] < r0[:, None] + NA_KH)
        dr_idx = np.clip(kr[None, :] - r[:, None] + NA_KH - 1, 0, 2 * NA_KH - 2)
        valid = row_in[:, None, :, None] & col_in[None, :, None, :]
        entry = (valid, dr_idx)
        cls = _na_block_class(rb, n_blocks)
        if cls in per_class:
            assert all(np.array_equal(a, b) for a, b in zip(per_class[cls], entry)), "row-block classes differ"
        per_class[cls] = entry
    nq, nk = NA_QROWS * GRID_W, NA_WIN * GRID_W
    onehot = (np.arange(2 * NA_KW - 1)[:, None] == dc_idx.reshape(1, -1)).astype(np.float32)
    by_col = jnp.dot(rpb.astype(F32).reshape(-1, 2 * NA_KW - 1), onehot, precision=lax.Precision.HIGHEST)
    by_col = by_col.reshape(NA_HEADS, 2 * NA_KH - 1, GRID_W, GRID_W)
    tables = []
    for cls in range(3):
        valid, dr_idx = per_class[cls]
        blocks = jnp.stack([by_col[:, int(dr)] for dr in dr_idx.reshape(-1)], axis=1)
        t = blocks.reshape(NA_HEADS, NA_QROWS, NA_WIN, GRID_W, GRID_W).transpose(0, 1, 3, 2, 4)
        tables.append(jnp.where(valid.reshape(nq, nk)[None], t.reshape(NA_HEADS, nq, nk), NEG_INF))
    return jnp.stack(tables)


def _neighbourhood_attention(q, k, v, kc, vc, rpb):
    b, n, w = q.shape
    c = kc.shape[1]
    rows = n // GRID_W
    assert rows % NA_QROWS == 0 and rows >= NA_WIN + 1
    n_blocks = rows // NA_QROWS
    bias = _na_bias_tables(rpb, rows)
    nq, nk = NA_QROWS * GRID_W, NA_WIN * GRID_W
    full = lambda length: pl.BlockSpec((1, length, w), lambda bi, rb: (bi, 0, 0))
    bias_spec = pl.BlockSpec(
        (1, NA_HEADS, nq, nk),
        lambda bi, rb: (jnp.minimum(rb, 1) + (rb == n_blocks - 1).astype(jnp.int32), 0, 0, 0))
    return pl.pallas_call(
        functools.partial(_na_kernel, rows=rows),
        grid=(b, n_blocks),
        in_specs=[pl.BlockSpec((1, nq, w), lambda bi, rb: (bi, rb, 0)), full(n), full(n), full(c), full(c), bias_spec],
        out_specs=pl.BlockSpec((1, nq, w), lambda bi, rb: (bi, rb, 0)),
        out_shape=jax.ShapeDtypeStruct((b, n, w), BF),
        compiler_params=_cparams(("parallel", "arbitrary")),
        name="neighbourhood_attention",
    )(q, k, v, kc, vc, bias)


def _ctx_attn_kernel(q_ref, k_ref, v_ref, o_ref):
    nq = q_ref.shape[1]
    first_half = lax.broadcasted_iota(jnp.int32, (nq, LANES), 1) < HEAD_DIM
    for p in range(NA_HEADS // 2):
        sl = slice(p * LANES, (p + 1) * LANES)
        qs = _pair_stack(q_ref[0, :, sl], first_half)
        s = lax.dot_general(qs, k_ref[0, :, sl], NT_DIMS, preferred_element_type=F32)
        e = jnp.exp(s - jnp.max(s, axis=-1, keepdims=True))
        o = jnp.dot(e.astype(BF), v_ref[0, :, sl], preferred_element_type=F32) / jnp.sum(e, axis=-1, keepdims=True)
        o_ref[0, :, sl] = _pair_merge(o, nq, first_half).astype(BF)


def _context_attention(q, k, v):
    b, c, w = q.shape
    spec = pl.BlockSpec((1, c, w), lambda bi: (bi, 0, 0))
    return pl.pallas_call(
        _ctx_attn_kernel,
        grid=(b,),
        in_specs=[spec] * 3,
        out_specs=spec,
        out_shape=jax.ShapeDtypeStruct((b, c, w), BF),
        compiler_params=_cparams(("parallel",)),
        name="context_attention",
    )(q, k, v)


def _out_ffn_even_kernel(x_ref, oa_ref, ob_ref, mod_ref, ng_ref, woa_ref, wob_ref, w1_ref, w3_ref, w2_ref, o_ref,
                         *, f_chunks):
    mix = (jnp.dot(oa_ref[0], woa_ref[...], preferred_element_type=F32)
           + jnp.dot(ob_ref[0], wob_ref[...], preferred_element_type=F32))
    x1 = x_ref[0] + mod_ref[0, 2:3, :] * mix
    h = _norm_mod(x1, ng_ref[...], mod_ref[0, 3:4, :], mod_ref[0, 4:5, :]).astype(BF)
    f = w1_ref.shape[1]
    fc = -(-f // (f_chunks * MXU_TILE)) * MXU_TILE
    y = None
    for lo in range(0, f, fc):
        hi = min(lo + fc, f)
        a = jnp.dot(h, w1_ref[:, lo:hi], preferred_element_type=F32)
        g = jnp.dot(h, w3_ref[:, lo:hi], preferred_element_type=F32)
        part = jnp.dot((_silu(a) * g).astype(BF), w2_ref[lo:hi, :], preferred_element_type=F32)
        y = part if y is None else y + part
    o_ref[0] = x1 + mod_ref[0, 5:6, :] * y


def _out_ffn_even(x, oa, ob, mods, mod_row, norm_gain, w_out, w1, w3, w2):
    b, n, d = x.shape
    tm = _token_tile(n)
    wa = oa.shape[2]
    tok = lambda width: pl.BlockSpec((1, tm, width), lambda bi, i: (bi, i, 0))
    return pl.pallas_call(
        functools.partial(_out_ffn_even_kernel, f_chunks=2),
        grid=(b, n // tm),
        in_specs=[tok(d), tok(wa), tok(ob.shape[2]), _mod_spec(d, mod_row), _const_spec((1, d)),
                  _const_spec((wa, d)), _const_spec((w_out.shape[0] - wa, d)),
                  _const_spec(w1.shape), _const_spec(w3.shape), _const_spec(w2.shape)],
        out_specs=tok(d),
        out_shape=jax.ShapeDtypeStruct((b, n, d), F32),
        compiler_params=_cparams(("parallel", "parallel")),
        name="out_ffn_even",
    )(x, oa, ob, mods, norm_gain, w_out[:wa], w_out[wa:], w1, w3, w2)


def _rope(t, cos, sin_signed):
    width = t.shape[1]
    lane = lax.broadcasted_iota(jnp.int32, t.shape, 1)
    partner = jnp.where(lane % 2 == 0, pltpu.roll(t, width - 1, 1), pltpu.roll(t, 1, 1))
    reps = width // LANES
    return t * jnp.concatenate([cos] * reps, axis=1) + partner * jnp.concatenate([sin_signed] * reps, axis=1)


def _in_odd_kernel(*refs, is_latent):
    if is_latent:
        (x_ref, mod_ref, ng_ref, w_ref, qg_ref, kg_ref, bdq_ref, bdk_ref, cos_ref, sin_ref,
         q_ref, kk_ref, vv_ref, u_ref) = refs
    else:
        x_ref, mod_ref, ng_ref, w_ref, kg_ref, bdk_ref, kk_ref, vv_ref, u_ref = refs
    h = _norm_mod(x_ref[0], ng_ref[...], mod_ref[0, 0:1, :], mod_ref[0, 1:2, :])
    p = jnp.dot(h.astype(BF), w_ref[...], preferred_element_type=F32)
    c0 = GQ_WIDTH if is_latent else 0
    kk = _head_norm(p[:, c0:c0 + GKV_PAD], bdk_ref[...], kg_ref[...])
    if is_latent:
        q = _head_norm(p[:, 0:GQ_WIDTH], bdq_ref[...], qg_ref[...])
        q_ref[0] = _rope(q, cos_ref[...], sin_ref[...]).astype(BF)
        kk = _rope(kk, cos_ref[...], sin_ref[...])
    kk_ref[0] = kk.astype(BF)
    vv_ref[0] = p[:, c0 + GKV_PAD:c0 + 2 * GKV_PAD].astype(BF)
    u_ref[...] = p[:, c0 + 2 * GKV_PAD:].astype(BF)


def _in_odd(x, mods, mod_row, norm_gain, w_ext, q_gain, k_gain, bdq, bdk, cos, sin_signed, is_latent):
    b, n, d = x.shape
    tm = _token_tile(n)
    tok = lambda width: pl.BlockSpec((1, tm, width), lambda bi, i: (bi, i, 0))
    vec = lambda width: _const_spec((1, width))
    if is_latent:
        rope_spec = pl.BlockSpec((tm, LANES), lambda bi, i: (i, 0))
        in_specs = [tok(d), _mod_spec(d, mod_row), vec(d), _const_spec(w_ext.shape), vec(GQ_WIDTH), vec(GKV_PAD),
                    _const_spec(bdq.shape), _const_spec(bdk.shape), rope_spec, rope_spec]
        args = (x, mods, norm_gain, w_ext, q_gain, k_gain, bdq, bdk, cos, sin_signed)
        widths = [GQ_WIDTH, GKV_PAD, GKV_PAD]
    else:
        in_specs = [tok(d), _mod_spec(d, mod_row), vec(d), _const_spec(w_ext.shape), vec(GKV_PAD),
                    _const_spec(bdk.shape)]
        args = (x, mods, norm_gain, w_ext, k_gain, bdk)
        widths = [GKV_PAD, GKV_PAD]
    u_spec = pl.BlockSpec((tm, S5_WIDTH), lambda bi, i: (i, bi))
    return pl.pallas_call(
        functools.partial(_in_odd_kernel, is_latent=is_latent),
        grid=(b, n // tm),
        in_specs=in_specs,
        out_specs=[tok(wd) for wd in widths] + [u_spec],
        out_shape=[jax.ShapeDtypeStruct((b, n, wd), BF) for wd in widths]
        + [jax.ShapeDtypeStruct((n, b * S5_WIDTH), BF)],
        compiler_params=_cparams(("parallel", "parallel")),
        name="in_odd_latent" if is_latent else "in_odd_context",
    )(*args)


def _rope_tables(n):
    t = jnp.arange(n)
    row = (t // GRID_W).astype(F32)
    col = (t % GRID_W).astype(F32)
    n_axis = HEAD_DIM // 4
    freqs = ROPE_THETA ** (-jnp.arange(n_axis, dtype=F32) / n_axis)
    ang = jnp.concatenate([row[:, None] * freqs, col[:, None] * freqs], axis=-1)
    ang = jnp.repeat(ang, 2, axis=-1)
    ang = jnp.concatenate([ang, ang], axis=-1)
    sign = jnp.where(jnp.arange(LANES) % 2 == 0, -1.0, 1.0).astype(F32)
    return jnp.cos(ang), jnp.sin(ang) * sign


def _gqa_kernel(q_ref, kk_ref, vv_ref, o_ref):
    tq = q_ref.shape[1]
    first_half = lax.broadcasted_iota(jnp.int32, (tq, LANES), 1) < HEAD_DIM
    group = GQ_HEADS // GKV_HEADS
    for j in range(GKV_HEADS):
        kj = kk_ref[0, :, j * LANES:(j + 1) * LANES]
        vj = vv_ref[0, :, j * LANES:(j + 1) * LANES]
        base = j * group * HEAD_DIM
        qs = jnp.concatenate(
            [_pair_stack(q_ref[0, :, base + i * LANES:base + (i + 1) * LANES], first_half)
             for i in range(group // 2)], axis=0)
        s = lax.dot_general(qs, kj, NT_DIMS, preferred_element_type=F32)
        e = jnp.exp(s - jnp.max(s, axis=-1, keepdims=True))
        o = jnp.dot(e.astype(BF), vj, preferred_element_type=F32) / jnp.sum(e, axis=-1, keepdims=True)
        for i in range(group // 2):
            pair = _pair_merge(o[2 * i * tq:(2 * i + 2) * tq], tq, first_half)
            o_ref[0, :, base + i * LANES:base + (i + 1) * LANES] = pair.astype(BF)


def _gqa(q, kk, vv):
    b, n, w = q.shape
    nk = kk.shape[1]
    tq = 128
    kv_spec = pl.BlockSpec((1, nk, GKV_PAD), lambda bi, i: (bi, 0, 0))
    return pl.pallas_call(
        _gqa_kernel,
        grid=(b, n // tq),
        in_specs=[pl.BlockSpec((1, tq, w), lambda bi, i: (bi, i, 0)), kv_spec, kv_spec],
        out_specs=pl.BlockSpec((1, tq, w), lambda bi, i: (bi, i, 0)),
        out_shape=jax.ShapeDtypeStruct((b, n, w), BF),
        compiler_params=_cparams(("parallel", "arbitrary")),
        name="gqa",
    )(q, kk, vv)


def _s5_kernel(u_ref, bbd_ref, lre_ref, lim_ref, cbd_ref, y_ref, bu_ref, st_ref, *, steps, bp):
    backward = pl.program_id(0) == 1

    @pl.when(pl.program_id(1) == 0)
    def _():
        st_ref[...] = jnp.zeros_like(st_ref)

    bu_ref[...] = jnp.dot(u_ref[...], bbd_ref[0], preferred_element_type=F32)
    lc = 2 * LANES
    for j in range(S5_NP // lc):
        re_sl = slice(j * lc, (j + 1) * lc)
        im_sl = slice(S5_NP + j * lc, S5_NP + (j + 1) * lc)
        lre = jnp.broadcast_to(lre_ref[0, :, re_sl], (bp, lc))
        lim = jnp.broadcast_to(lim_ref[0, :, re_sl], (bp, lc))

        def step(s, carry):
            xre, xim = carry
            t = jnp.where(backward, steps - 1 - s, s)
            r0 = pl.multiple_of(t * bp, bp)
            nre = lre * xre - lim * xim + bu_ref[pl.ds(r0, bp), re_sl]
            nim = lre * xim + lim * xre + bu_ref[pl.ds(r0, bp), im_sl]
            bu_ref[pl.ds(r0, bp), re_sl] = nre
            bu_ref[pl.ds(r0, bp), im_sl] = nim
            return nre, nim

        xre, xim = lax.fori_loop(0, steps, step, (st_ref[:, re_sl], st_ref[:, im_sl]), unroll=2)
        st_ref[:, re_sl] = xre
        st_ref[:, im_sl] = xim
    y_ref[0] = jnp.dot(bu_ref[...].astype(BF), cbd_ref[0], preferred_element_type=F32)


def _s5_params(a_re, a_im, log_step, b_re, b_im, c_re, c_im):
    a_re, a_im = a_re.astype(F32), a_im.astype(F32)
    step = jnp.exp(log_step.astype(F32))[..., None]
    mag = jnp.exp(a_re * step)
    lre, lim = mag * jnp.cos(a_im * step), mag * jnp.sin(a_im * step)
    den = a_re * a_re + a_im * a_im
    kre = ((lre - 1.0) * a_re + lim * a_im) / den
    kim = (lim * a_re - (lre - 1.0) * a_im) / den
    bre = kre[..., None] * b_re - kim[..., None] * b_im
    bim = kre[..., None] * b_im + kim[..., None] * b_re
    eye = jnp.eye(S5_GROUPS, dtype=F32)

    def in_blockdiag(m):
        return jnp.einsum('dgpn,gh->dgnhp', m, eye).reshape(2, S5_WIDTH, S5_NP)

    def out_blockdiag(m):
        return jnp.einsum('dgnp,gh->dgphn', m, eye).reshape(2, S5_NP, S5_WIDTH)

    bbd = jnp.concatenate([in_blockdiag(bre), in_blockdiag(bim)], axis=2).astype(BF)
    cbd = jnp.concatenate([out_blockdiag(c_re.astype(F32)), -out_blockdiag(c_im.astype(F32))], axis=1).astype(BF)
    return bbd, lre.reshape(2, 1, S5_NP), lim.reshape(2, 1, S5_NP), cbd


def _s5_scan(u_seq, n_ctx_rows, bbd, lre, lim, cbd, bp):
    rows = u_seq.shape[0]
    steps = 32
    rb = steps * bp
    assert rows % rb == 0 and n_ctx_rows % rb == 0
    nc = n_ctx_rows // rb
    nl = rows // rb - nc

    def in_block(d, i):
        rev = jnp.where(i < nc, nc - 1 - i, nc + nl - 1 - (i - nc))
        return jnp.where(d == 0, i, rev)

    def out_block(d, i):
        k = jnp.maximum(i - nc, 0)
        return jnp.where(d == 0, k, nl - 1 - k)

    per_dir = lambda shape: pl.BlockSpec((1,) + shape, lambda d, i: (d, 0, 0))
    return pl.pallas_call(
        functools.partial(_s5_kernel, steps=steps, bp=bp),
        grid=(2, nc + nl),
        in_specs=[pl.BlockSpec((rb, S5_WIDTH), lambda d, i: (in_block(d, i), 0)), per_dir((S5_WIDTH, 2 * S5_NP)),
                  per_dir((1, S5_NP)), per_dir((1, S5_NP)), per_dir((2 * S5_NP, S5_WIDTH))],
        out_specs=pl.BlockSpec((1, rb, S5_WIDTH), lambda d, i: (d, out_block(d, i), 0)),
        out_shape=jax.ShapeDtypeStruct((2, nl * rb, S5_WIDTH), F32),
        scratch_shapes=[pltpu.VMEM((rb, 2 * S5_NP), F32), pltpu.VMEM((bp, 2 * S5_NP), F32)],
        compiler_params=_cparams(("arbitrary", "arbitrary")),
        name="s5_scan",
    )(u_seq, bbd, lre, lim, cbd)


def _s5_mixer_latent(u_lat, u_ctx, b, params):
    n, c = u_lat.shape[0], u_ctx.shape[0]
    bp = -(-b // 8) * 8
    seq = jnp.concatenate([u_ctx, u_lat], axis=0).reshape(c + n, b, S5_WIDTH)
    seq = jnp.pad(seq, ((0, 0), (0, bp - b), (0, 0))).reshape((c + n) * bp, S5_WIDTH)
    y = _s5_scan(seq, c * bp, *params, bp)
    return (y[0] + y[1]).reshape(n, bp, S5_WIDTH)[:, :b].reshape(n, b * S5_WIDTH)


def _out_odd_kernel(x_ref, oa_ref, ys_ref, u_ref, mod_ref, ng_ref, d_ref, gw_ref, gb_ref, woa_ref, wob_ref,
                    rw_ref, rb_ref, x1_ref, h_ref, gate_ref):
    y = ys_ref[...] + d_ref[...] * u_ref[...].astype(F32)
    gl = 0.5 * y * (1.0 + jnp.tanh(np.sqrt(2.0 / np.pi) * (y + 0.044715 * (y * y * y))))
    z = gl * jax.nn.sigmoid(jnp.dot(gl.astype(BF), gw_ref[...], preferred_element_type=F32) + gb_ref[...])
    mix = (jnp.dot(oa_ref[0], woa_ref[...], preferred_element_type=F32)
           + jnp.dot(z.astype(BF), wob_ref[...], preferred_element_type=F32))
    x1 = x_ref[0] + mod_ref[0, 2:3, :] * mix
    x1_ref[0] = x1
    h = _norm_mod(x1, ng_ref[...], mod_ref[0, 3:4, :], mod_ref[0, 4:5, :]).astype(BF)
    h_ref[0] = h
    logits = jnp.dot(h, rw_ref[...], preferred_element_type=F32) + rb_ref[...]
    lane = lax.broadcasted_iota(jnp.int32, logits.shape, 1)
    m1 = jnp.max(logits, axis=-1, keepdims=True)
    i1 = jnp.min(jnp.where(logits == m1, lane, LANES), axis=-1, keepdims=True)
    rest = jnp.where(lane == i1, -jnp.inf, logits)
    m2 = jnp.max(rest, axis=-1, keepdims=True)
    i2 = jnp.min(jnp.where(rest == m2, lane, LANES), axis=-1, keepdims=True)
    e2 = jnp.exp(m2 - m1)
    gate_ref[0] = jnp.where(lane == i1, 1.0 / (1.0 + e2), 0.0) + jnp.where(lane == i2, e2 / (1.0 + e2), 0.0)


def _out_odd(x, oa, ys, u, mods, norm_gain, d_skip, glu_w, glu_b, w_out, router_w, router_b):
    b, n, d = x.shape
    tm = _token_tile(n)
    wa = oa.shape[2]
    tok = lambda width: pl.BlockSpec((1, tm, width), lambda bi, i: (bi, i, 0))
    vec = lambda width: _const_spec((1, width))
    tmajor = pl.BlockSpec((tm, S5_WIDTH), lambda bi, i: (i, bi))
    return pl.pallas_call(
        _out_odd_kernel,
        grid=(b, n // tm),
        in_specs=[tok(d), tok(wa), tmajor, tmajor, _mod_spec(d, None), vec(d), vec(S5_WIDTH),
                  _const_spec(glu_w.shape), vec(S5_WIDTH), _const_spec((wa, d)), _const_spec((S5_WIDTH, d)),
                  _const_spec(router_w.shape), vec(LANES)],
        out_specs=[tok(d), tok(d), tok(LANES)],
        out_shape=[jax.ShapeDtypeStruct((b, n, d), F32), jax.ShapeDtypeStruct((b, n, d), BF),
                   jax.ShapeDtypeStruct((b, n, LANES), F32)],
        compiler_params=_cparams(("parallel", "parallel")),
        name="out_odd",
    )(x, oa, ys, u, mods, norm_gain, d_skip, glu_w, glu_b, w_out[:wa], w_out[wa:], router_w, router_b)


MOE_CHUNK = 512
MOE_F_CHUNKS = 4
MOE_ALIGN = 16


def _moe_capacity(tb):
    need = 2 * tb + N_EXPERTS * MOE_ALIGN + MOE_CHUNK
    return -(-need // MOE_CHUNK) * MOE_CHUNK


def _moe_kernel(h_ref, gate_ref, x1_ref, mod_ref, w1_ref, w3_ref, w2_ref, o_ref, xy_ref, tok_ref, tokt_ref,
                cnt_ref, seg_ref):
    e = pl.program_id(1)
    q = pl.program_id(2)
    last_e = pl.num_programs(1) - 1
    last_q = pl.num_programs(2) - 1
    tb, d = h_ref.shape[1:]
    cap = xy_ref.shape[0]
    ch = MOE_CHUNK

    @pl.when((e == 0) & (q == 0))
    def _pack():
        lane1 = lax.broadcasted_iota(jnp.int32, (1, LANES), 1)
        total = jnp.zeros((1, LANES), F32)
        for c in range(tb // ch):
            total = total + jnp.sum((gate_ref[0, c * ch:(c + 1) * ch, :] > 0.0).astype(F32), axis=0, keepdims=True)
        seg_vec = jnp.zeros((1, LANES), F32)
        start = jnp.int32(0)
        for k in range(N_EXPERTS):
            nk = jnp.sum(jnp.where(lane1 == k, total, 0.0)).astype(jnp.int32)
            cnt_ref[k] = nk
            seg_ref[k] = start
            seg_vec = jnp.where(lane1 == k, start.astype(F32), seg_vec)
            start = start + (nk + MOE_ALIGN - 1) // MOE_ALIGN * MOE_ALIGN
        tri = (lax.broadcasted_iota(jnp.int32, (ch, ch), 0)
               > lax.broadcasted_iota(jnp.int32, (ch, ch), 1)).astype(BF)
        lane = lax.broadcasted_iota(jnp.int32, (ch, LANES), 1)
        offset = jnp.zeros((1, LANES), F32)
        for c in range(tb // ch):
            g = gate_ref[0, c * ch:(c + 1) * ch, :]
            sel = g > 0.0
            sel_f = sel.astype(F32)
            rank = jnp.dot(tri, sel_f.astype(BF), preferred_element_type=F32) + offset
            pos = jnp.where(sel, rank + seg_vec, -1.0)
            pmax = jnp.max(pos, axis=-1, keepdims=True)
            pmin = jnp.min(jnp.where(sel, pos, 1e9), axis=-1, keepdims=True)
            gmin = jnp.sum(jnp.where(pos == pmin, g, 0.0), axis=-1, keepdims=True)
            gmax = jnp.sum(jnp.where(pos == pmax, g, 0.0), axis=-1, keepdims=True)
            pmax = jnp.where(pmax == pmin, -1.0, pmax)
            tok_ref[c * ch:(c + 1) * ch, :] = jnp.where(
                lane == 0, pmin, jnp.where(lane == 1, pmax, jnp.where(lane == 2, gmin, jnp.where(lane == 3, gmax, 0.0))))
            offset = offset + jnp.sum(sel_f, axis=0, keepdims=True)
        tokt_ref[...] = tok_ref[...].T
        pmin_t, pmax_t = tokt_ref[0:1, :], tokt_ref[1:2, :]
        row_id = lax.broadcasted_iota(jnp.int32, (ch, tb), 0).astype(F32)
        for c in range(cap // ch):
            rid = row_id + float(c * ch)
            onehot = jnp.where(rid == pmin_t, 1.0, jnp.where(rid == pmax_t, 1.0, 0.0)).astype(BF)
            xy_ref[c * ch:(c + 1) * ch, :] = jnp.dot(onehot, h_ref[0], preferred_element_type=F32).astype(BF)

    n = cnt_ref[e]
    seg = seg_ref[e]

    def ffn_tile(r0, size):
        src = pl.multiple_of(seg + r0, MOE_ALIGN)
        x = xy_ref[pl.ds(src, size), :]
        a = jnp.dot(x, w1_ref[0], preferred_element_type=F32)
        g = jnp.dot(x, w3_ref[0], preferred_element_type=F32)
        y = jnp.dot((_silu(a) * g).astype(BF), w2_ref[0], preferred_element_type=F32)

        @pl.when(q == 0)
        def _():
            o_ref[0, pl.ds(r0, size), :] = y

        @pl.when((q > 0) & (q < last_q))
        def _():
            o_ref[0, pl.ds(r0, size), :] += y

        @pl.when(q == last_q)
        def _():
            rid = lax.broadcasted_iota(jnp.int32, (size, tb), 0).astype(F32) + src.astype(F32)
            gate_rows = jnp.sum(jnp.where(rid == tokt_ref[0:1, :], tokt_ref[2:3, :], 0.0)
                                + jnp.where(rid == tokt_ref[1:2, :], tokt_ref[3:4, :], 0.0), axis=-1, keepdims=True)
            mine = lax.broadcasted_iota(jnp.int32, (size, 1), 0) < n - r0
            res = gate_rows * (o_ref[0, pl.ds(r0, size), :] + y)
            xy_ref[pl.ds(src, size), :] = jnp.where(mine, res.astype(BF), x)

    n_big = n // ch
    left = n - n_big * ch
    has_mid = (left >= ch // 2).astype(jnp.int32)
    tail = left - has_mid * (ch // 2)

    def big_tile(i, carry):
        ffn_tile(pl.multiple_of(i * ch, ch), ch)
        return carry

    lax.fori_loop(0, n_big, big_tile, 0)
    mid_r0 = pl.multiple_of(n_big * ch, ch)
    tail_r0 = pl.multiple_of(mid_r0 + has_mid * (ch // 2), ch // 2)

    @pl.when(has_mid == 1)
    def _():
        ffn_tile(mid_r0, ch // 2)

    @pl.when(tail > ch // 4)
    def _():
        ffn_tile(tail_r0, ch // 2)

    @pl.when((tail > 0) & (tail <= ch // 4))
    def _():
        ffn_tile(tail_r0, ch // 4)

    @pl.when((e == last_e) & (q == last_q))
    def _combine():
        col_id = lax.broadcasted_iota(jnp.int32, (ch, cap), 1).astype(F32)
        for c in range(tb // ch):
            rows = slice(c * ch, (c + 1) * ch)
            pmin, pmax = tok_ref[rows, 0:1], tok_ref[rows, 1:2]
            onehot = jnp.where(col_id == pmin, 1.0, jnp.where(col_id == pmax, 1.0, 0.0)).astype(BF)
            mix = jnp.dot(onehot, xy_ref[...], preferred_element_type=F32)
            o_ref[0, rows, :] = x1_ref[0, rows, :] + mod_ref[0, 5:6, :] * mix


def _moe(h, gates, x1, mods, w1, w3, w2):
    b, n, d = h.shape
    n_exp, _, f = w1.shape
    assert n_exp == N_EXPERTS and n % MOE_CHUNK == 0 and MOE_F_CHUNKS > 1
    fc = f // MOE_F_CHUNKS
    per_batch = lambda width: pl.BlockSpec((1, n, width), lambda bi, e, q: (bi, 0, 0), pipeline_mode=pl.Buffered(1))
    return pl.pallas_call(
        _moe_kernel,
        grid=(b, n_exp, MOE_F_CHUNKS),
        in_specs=[per_batch(d), per_batch(LANES), per_batch(d),
                  pl.BlockSpec((1, N_MOD, d), lambda bi, e, q: (bi, 0, 0)),
                  pl.BlockSpec((1, d, fc), lambda bi, e, q: (e, 0, q)),
                  pl.BlockSpec((1, d, fc), lambda bi, e, q: (e, 0, q)),
                  pl.BlockSpec((1, fc, d), lambda bi, e, q: (e, q, 0))],
        out_specs=per_batch(d),
        out_shape=jax.ShapeDtypeStruct((b, n, d), F32),
        scratch_shapes=[pltpu.VMEM((_moe_capacity(n), d), BF), pltpu.VMEM((n, LANES), F32),
                        pltpu.VMEM((LANES, n), F32), pltpu.SMEM((N_EXPERTS,), jnp.int32),
                        pltpu.SMEM((N_EXPERTS,), jnp.int32)],
        compiler_params=_cparams(("arbitrary", "arbitrary", "arbitrary")),
        name="moe",
    )(h, gates, x1, mods, w1, w3, w2)


def _head_blockdiag(width):
    head = np.arange(width) // HEAD_DIM
    return jnp.asarray((head[:, None] == head[None, :]).astype(np.float32) / HEAD_DIM, dtype=BF)


def _tile_gain(gain, width, scale=1.0):
    return (jnp.tile(gain.astype(F32), width // HEAD_DIM) * scale).reshape(1, width)


def _even_layer(x, xc, mods, ctx_row, norm_mix, norm_ffn, w_in, q_gain, k_gain, rpb, w_out, w1, w3, w2):
    d = x.shape[2]
    bd = _head_blockdiag(NA_WIDTH)
    w_in = w_in.astype(BF)
    qg = _tile_gain(q_gain, NA_WIDTH, HEAD_DIM ** -0.5)
    kg = _tile_gain(k_gain, NA_WIDTH)
    ng_mix, ng_ffn = norm_mix.reshape(1, d), norm_ffn.reshape(1, d)
    u, q, k, v = _in_even(x, mods, None, ng_mix, w_in, qg, kg, bd)
    uc, qc, kc, vc = _in_even(xc, mods, ctx_row, ng_mix, w_in, qg, kg, bd)
    o_na = _neighbourhood_attention(q, k, v, kc, vc, rpb)
    o_ctx = _context_attention(qc, kc, vc)
    w_out, w1, w3, w2 = (t.astype(BF) for t in (w_out, w1, w3, w2))
    x = _out_ffn_even(x, _fourier(u), o_na, mods, None, ng_ffn, w_out, w1, w3, w2)
    xc = _out_ffn_even(xc, _fourier(uc), o_ctx, mods, ctx_row, ng_ffn, w_out, w1, w3, w2)
    return x, xc


def _odd_layer(x, xc, mods, ctx_row, norm_mix, norm_ffn, w_in, q_gain, k_gain, s5, d_skip, glu_w, glu_b, w_out,
               router_w, router_b, ew1, ew3, ew2):
    b, n, d = x.shape
    kvw = GKV_HEADS * HEAD_DIM
    wq, wk, wv, wu = jnp.split(w_in, [GQ_WIDTH, GQ_WIDTH + kvw, GQ_WIDTH + 2 * kvw], axis=1)

    def twice(w):
        return jnp.repeat(w.reshape(d, GKV_HEADS, 1, HEAD_DIM), 2, axis=2).reshape(d, GKV_PAD)

    w_lat = jnp.concatenate([wq, twice(wk), twice(wv), wu], axis=1).astype(BF)
    w_ctx = jnp.concatenate([twice(wk), twice(wv), wu], axis=1).astype(BF)
    qg = _tile_gain(q_gain, GQ_WIDTH, HEAD_DIM ** -0.5)
    kg = _tile_gain(k_gain, GKV_PAD)
    bdq, bdk = _head_blockdiag(GQ_WIDTH), _head_blockdiag(GKV_PAD)
    cos, sin_signed = _rope_tables(n)
    ng_mix, ng_ffn = norm_mix.reshape(1, d), norm_ffn.reshape(1, d)
    q, kk, vv, u = _in_odd(x, mods, None, ng_mix, w_lat, qg, kg, bdq, bdk, cos, sin_signed, True)
    kkc, vvc, uc = _in_odd(xc, mods, ctx_row, ng_mix, w_ctx, None, kg, None, bdk, None, None, False)
    o_attn = _gqa(q, jnp.concatenate([kkc, kk], axis=1), jnp.concatenate([vvc, vv], axis=1))
    y_ssm = _s5_mixer_latent(u, uc, b, _s5_params(*s5))
    rw = jnp.pad(router_w, ((0, 0), (0, LANES - N_EXPERTS))).astype(BF)
    rb = jnp.pad(router_b.astype(F32), (0, LANES - N_EXPERTS), constant_values=NEG_INF).reshape(1, LANES)
    x1, h, gates = _out_odd(x, o_attn, y_ssm, u, mods, ng_ffn, d_skip.reshape(1, S5_WIDTH).astype(F32),
                            glu_w.astype(BF), glu_b.reshape(1, S5_WIDTH).astype(F32), w_out.astype(BF), rw, rb)
    return _moe(h, gates, x1, mods, ew1.astype(BF), ew3.astype(BF), ew2.astype(BF))


def kernel(x, c, ctx, c_ctx, ev_mod_w, ev_mod_b, ev_norm_mix, ev_norm_ffn, ev_w_in, ev_q_gain, ev_k_gain, ev_rpb,
           ev_w_out, ev_ffn_w1, ev_ffn_w3, ev_ffn_w2, od_mod_w, od_mod_b, od_norm_mix, od_norm_ffn, od_w_in,
           od_q_gain, od_k_gain, od_s5_a_re, od_s5_a_im, od_s5_log_step, od_s5_b_re, od_s5_b_im, od_s5_c_re,
           od_s5_c_im, od_s5_d, od_s5_glu_w, od_s5_glu_b, od_w_out, od_router_w, od_router_b, od_exp_w1, od_exp_w3,
           od_exp_w2):
    assert ev_mod_w.shape[0] == 1 and od_mod_w.shape[0] == 1, "one even and one odd layer"
    b, n, d = x.shape
    rows = -(-(b + 1) // 8) * 8
    cond = jnp.zeros((rows, d), F32).at[:b].set(c).at[b].set(c_ctx)
    mods_even = _modulation(cond, ev_mod_w[0], ev_mod_b[0])
    mods_odd = _modulation(cond, od_mod_w[0], od_mod_b[0])
    x, xc = _even_layer(x, ctx, mods_even, b, ev_norm_mix[0], ev_norm_ffn[0], ev_w_in[0], ev_q_gain[0], ev_k_gain[0],
                        ev_rpb[0], ev_w_out[0], ev_ffn_w1[0], ev_ffn_w3[0], ev_ffn_w2[0])
    s5 = (od_s5_a_re[0], od_s5_a_im[0], od_s5_log_step[0], od_s5_b_re[0], od_s5_b_im[0], od_s5_c_re[0], od_s5_c_im[0])
    return _odd_layer(x, xc, mods_odd, b, od_norm_mix[0], od_norm_ffn[0], od_w_in[0], od_q_gain[0], od_k_gain[0],
                      s5, od_s5_d[0], od_s5_glu_w[0], od_s5_glu_b[0], od_w_out[0], od_router_w[0], od_router_b[0],
                      od_exp_w1[0], od_exp_w3[0], od_exp_w2[0])
```

```python
import functools

import numpy as np
import jax
import jax.numpy as jnp
from jax import lax
from jax.experimental import pallas as pl
from jax.experimental.pallas import tpu as pltpu

BF = jnp.bfloat16
F32 = jnp.float32

EPS = 1e-6
NEG_INF = -1e30
GRID_W = 64
LANES = 128
MXU_TILE = 256
HEAD_DIM = 64
FN_GROUPS, FN_GROUP_DIM = 4, 128
FN_WIDTH = FN_GROUPS * FN_GROUP_DIM
NA_HEADS = 8
NA_WIDTH = NA_HEADS * HEAD_DIM
NA_KH, NA_KW = 8, 16
NA_QROWS = 4
NA_WIN = NA_QROWS + NA_KH - 1
GQ_HEADS, GKV_HEADS = 12, 3
GQ_WIDTH = GQ_HEADS * HEAD_DIM
GKV_PAD = GKV_HEADS * LANES
S5_GROUPS, S5_GROUP_DIM, S5_STATE = 16, 16, 64
S5_WIDTH = S5_GROUPS * S5_GROUP_DIM
S5_NP = S5_GROUPS * S5_STATE
N_EXPERTS = 8
N_MOD = 6
ROPE_THETA = 10000.0
VMEM_LIMIT = 56 * 1024 * 1024

NT_DIMS = (((1,), (1,)), ((), ()))


def _cparams(sem):
    return pltpu.CompilerParams(dimension_semantics=sem, vmem_limit_bytes=VMEM_LIMIT)


def _const_spec(shape):
    nd = len(shape)
    return pl.BlockSpec(shape, lambda *_: (0,) * nd, pipeline_mode=pl.Buffered(1))


def _token_tile(n):
    return 512 if n % 512 == 0 else 256


def _mod_spec(d, row):
    if row is None:
        return pl.BlockSpec((1, N_MOD, d), lambda b, i: (b, 0, 0))
    return pl.BlockSpec((1, N_MOD, d), lambda b, i: (row, 0, 0))


def _norm_mod(x, gain, shift, scale):
    ms = jnp.mean(x * x, axis=-1, keepdims=True)
    return (x * lax.rsqrt(ms + EPS) * gain) * (1.0 + scale) + shift


def _head_norm(t, blockdiag, gain):
    ms = jnp.dot((t * t).astype(BF), blockdiag, preferred_element_type=F32)
    return t * lax.rsqrt(ms + EPS) * gain


def _silu(a):
    return a * jax.nn.sigmoid(a)


def _mod_kernel(c_ref, w_ref, b_ref, o_ref):
    a = _silu(c_ref[...]).astype(BF)
    o_ref[...] = jnp.dot(a, w_ref[...].astype(BF), preferred_element_type=F32) + b_ref[...]


def _modulation(cond, w, b):
    r, d = cond.shape
    n = w.shape[1]
    tn = n // 4
    out = pl.pallas_call(
        _mod_kernel,
        grid=(n // tn,),
        in_specs=[pl.BlockSpec((r, d), lambda j: (0, 0)),
                  pl.BlockSpec((d, tn), lambda j: (0, j)),
                  pl.BlockSpec((1, tn), lambda j: (0, j))],
        out_specs=pl.BlockSpec((r, tn), lambda j: (0, j)),
        out_shape=jax.ShapeDtypeStruct((r, n), F32),
        compiler_params=_cparams(("arbitrary",)),
        name="modulation",
    )(cond, w, b.reshape(1, n))
    return out.reshape(r, N_MOD, d)


def _in_even_kernel(x_ref, mod_ref, ng_ref, w_ref, qg_ref, kg_ref, bd_ref, u_ref, q_ref, k_ref, v_ref):
    h = _norm_mod(x_ref[0], ng_ref[...], mod_ref[0, 0:1, :], mod_ref[0, 1:2, :])
    p = jnp.dot(h.astype(BF), w_ref[...], preferred_element_type=F32)
    bd = bd_ref[...]
    w = FN_WIDTH
    u_ref[0] = p[:, 0:w].astype(BF)
    q_ref[0] = _head_norm(p[:, w:2 * w], bd, qg_ref[...]).astype(BF)
    k_ref[0] = _head_norm(p[:, 2 * w:3 * w], bd, kg_ref[...]).astype(BF)
    v_ref[0] = p[:, 3 * w:4 * w].astype(BF)


def _in_even(x, mods, mod_row, norm_gain, w_in, q_gain, k_gain, blockdiag):
    b, n, d = x.shape
    tm = _token_tile(n)
    wout = FN_WIDTH
    tok = lambda width: pl.BlockSpec((1, tm, width), lambda bi, i: (bi, i, 0))
    return pl.pallas_call(
        _in_even_kernel,
        grid=(b, n // tm),
        in_specs=[tok(d), _mod_spec(d, mod_row), _const_spec((1, d)), _const_spec(w_in.shape),
                  _const_spec((1, wout)), _const_spec((1, wout)), _const_spec(blockdiag.shape)],
        out_specs=[tok(wout)] * 4,
        out_shape=[jax.ShapeDtypeStruct((b, n, wout), BF)] * 4,
        compiler_params=_cparams(("parallel", "parallel")),
        name="in_even",
    )(x, mods, norm_gain, w_in, q_gain, k_gain, blockdiag)


def _fourier_kernel(u_ref, cm_ref, ml_ref, o_ref, z_ref):
    n = u_ref.shape[1]
    gd = FN_GROUP_DIM
    for g in range(FN_GROUPS):
        z = jnp.dot(u_ref[0, :, g * gd:(g + 1) * gd], cm_ref[...], preferred_element_type=F32)
        z_ref[0:n, g * gd:(g + 1) * gd] = z[:, :gd].astype(BF)
        z_ref[n:2 * n, g * gd:(g + 1) * gd] = z[:, gd:].astype(BF)
    o_ref[0] = jnp.dot(ml_ref[...], z_ref[...], preferred_element_type=F32).astype(BF)


def _dft_tables(n):
    def cos_sin(size):
        j = jnp.arange(size, dtype=jnp.int32)
        ang = ((j[:, None] * j[None, :]) % size).astype(F32) * (2.0 * np.pi / size)
        return jnp.cos(ang), jnp.sin(ang)
    cc, sc = cos_sin(FN_GROUP_DIM)
    cl, sl = cos_sin(n)
    scale = 1.0 / np.sqrt(n * FN_GROUP_DIM)
    return (jnp.concatenate([cc, sc], axis=1).astype(BF),
            (jnp.concatenate([cl, -sl], axis=1) * scale).astype(BF))


def _fourier(u):
    b, n, w = u.shape
    cm, ml = _dft_tables(n)
    return pl.pallas_call(
        _fourier_kernel,
        grid=(b,),
        in_specs=[pl.BlockSpec((1, n, w), lambda bi: (bi, 0, 0)), _const_spec(cm.shape), _const_spec(ml.shape)],
        out_specs=pl.BlockSpec((1, n, w), lambda bi: (bi, 0, 0)),
        out_shape=jax.ShapeDtypeStruct((b, n, w), BF),
        scratch_shapes=[pltpu.VMEM((2 * n, w), BF)],
        compiler_params=_cparams(("parallel",)),
        name="fourier",
    )(u, cm, ml)


def _pair_stack(qp, first_half):
    zero = jnp.zeros_like(qp)
    return jnp.concatenate([jnp.where(first_half, qp, zero), jnp.where(first_half, zero, qp)], axis=0)


def _pair_merge(o, nq, first_half):
    return jnp.where(first_half, o[:nq], o[nq:])


def _na_kernel(q_ref, k_ref, v_ref, kc_ref, vc_ref, bias_ref, o_ref, *, rows):
    rb = pl.program_id(1)
    ws = jnp.minimum(jnp.maximum(NA_QROWS * rb - NA_KH // 2, 0), rows - NA_WIN)
    start = pl.multiple_of(ws * GRID_W, GRID_W)
    nq = NA_QROWS * GRID_W
    nk = NA_WIN * GRID_W
    first_half = lax.broadcasted_iota(jnp.int32, (nq, LANES), 1) < HEAD_DIM
    for p in range(NA_HEADS // 2):
        sl = slice(p * LANES, (p + 1) * LANES)
        qs = _pair_stack(q_ref[0, :, sl], first_half)
        kw = k_ref[0, pl.ds(start, nk), sl]
        vw = v_ref[0, pl.ds(start, nk), sl]
        bias = jnp.concatenate([bias_ref[0, 2 * p], bias_ref[0, 2 * p + 1]], axis=0)
        s_loc = lax.dot_general(qs, kw, NT_DIMS, preferred_element_type=F32) + bias
        s_ctx = lax.dot_general(qs, kc_ref[0, :, sl], NT_DIMS, preferred_element_type=F32)
        m = jnp.maximum(jnp.max(s_loc, axis=-1, keepdims=True), jnp.max(s_ctx, axis=-1, keepdims=True))
        p_loc = jnp.exp(s_loc - m)
        p_ctx = jnp.exp(s_ctx - m)
        denom = jnp.sum(p_loc, axis=-1, keepdims=True) + jnp.sum(p_ctx, axis=-1, keepdims=True)
        o = (jnp.dot(p_loc.astype(BF), vw, preferred_element_type=F32)
             + jnp.dot(p_ctx.astype(BF), vc_ref[0, :, sl], preferred_element_type=F32)) / denom
        o_ref[0, :, sl] = _pair_merge(o, nq, first_half).astype(BF)


def _na_block_class(rb, n_blocks):
    return int(rb > 0) + int(rb == n_blocks - 1)


def _na_bias_tables(rpb, rows):
    n_blocks = rows // NA_QROWS
    col = np.arange(GRID_W)
    col_start = np.clip(col - NA_KW // 2, 0, GRID_W - NA_KW)
    col_in = (col[None, :] >= col_start[:, None]) & (col[None, :] < col_start[:, None] + NA_KW)
    dc_idx = np.clip(col[None, :] - col[:, None] + NA_KW - 1, 0, 2 * NA_KW - 2)
    per_class = {}
    for rb in range(n_blocks):
        ws = min(max(NA_QROWS * rb - NA_KH // 2, 0), rows - NA_WIN)
        r = NA_QROWS * rb + np.arange(NA_QROWS)
        kr = ws + np.arange(NA_WIN)
        r0 = np.clip(r - NA_KH // 2, 0, rows - NA_KH)
        row_in = (kr[None, :] >= r0[:, None]) & (kr[None, :] < r0[:, None] + NA_KH)
        dr_idx = np.clip(kr[None, :] - r[:, None] + NA_KH - 1, 0, 2 * NA_KH - 2)
        valid = row_in[:, None, :, None] & col_in[None, :, None, :]
        entry = (valid, dr_idx)
        cls = _na_block_class(rb, n_blocks)
        if cls in per_class:
            assert all(np.array_equal(a, b) for a, b in zip(per_class[cls], entry)), "row-block classes differ"
        per_class[cls] = entry
    nq, nk = NA_QROWS * GRID_W, NA_WIN * GRID_W
    onehot = (np.arange(2 * NA_KW - 1)[:, None] == dc_idx.reshape(1, -1)).astype(np.float32)
    by_col = jnp.dot(rpb.astype(F32).reshape(-1, 2 * NA_KW - 1), onehot, precision=lax.Precision.HIGHEST)
    by_col = by_col.reshape(NA_HEADS, 2 * NA_KH - 1, GRID_W, GRID_W)
    tables = []
    for cls in range(3):
        valid, dr_idx = per_class[cls]
        blocks = jnp.stack([by_col[:, int(dr)] for dr in dr_idx.reshape(-1)], axis=1)
        t = blocks.reshape(NA_HEADS, NA_QROWS, NA_WIN, GRID_W, GRID_W).transpose(0, 1, 3, 2, 4)
        tables.append(jnp.where(valid.reshape(nq, nk)[None], t.reshape(NA_HEADS, nq, nk), NEG_INF))
    return jnp.stack(tables)


def _neighbourhood_attention(q, k, v, kc, vc, rpb):
    b, n, w = q.shape
    c = kc.shape[1]
    rows = n // GRID_W
    assert rows % NA_QROWS == 0 and rows >= NA_WIN + 1
    n_blocks = rows // NA_QROWS
    bias = _na_bias_tables(rpb, rows)
    nq, nk = NA_QROWS * GRID_W, NA_WIN * GRID_W
    full = lambda length: pl.BlockSpec((1, length, w), lambda bi, rb: (bi, 0, 0))
    bias_spec = pl.BlockSpec(
        (1, NA_HEADS, nq, nk),
        lambda bi, rb: (jnp.minimum(rb, 1) + (rb == n_blocks - 1).astype(jnp.int32), 0, 0, 0))
    return pl.pallas_call(
        functools.partial(_na_kernel, rows=rows),
        grid=(b, n_blocks),
        in_specs=[pl.BlockSpec((1, nq, w), lambda bi, rb: (bi, rb, 0)), full(n), full(n), full(c), full(c), bias_spec],
        out_specs=pl.BlockSpec((1, nq, w), lambda bi, rb: (bi, rb, 0)),
        out_shape=jax.ShapeDtypeStruct((b, n, w), BF),
        compiler_params=_cparams(("parallel", "arbitrary")),
        name="neighbourhood_attention",
    )(q, k, v, kc, vc, bias)


def _ctx_attn_kernel(q_ref, k_ref, v_ref, o_ref):
    nq = q_ref.shape[1]
    first_half = lax.broadcasted_iota(jnp.int32, (nq, LANES), 1) < HEAD_DIM
    for p in range(NA_HEADS // 2):
        sl = slice(p * LANES, (p + 1) * LANES)
        qs = _pair_stack(q_ref[0, :, sl], first_half)
        s = lax.dot_general(qs, k_ref[0, :, sl], NT_DIMS, preferred_element_type=F32)
        e = jnp.exp(s - jnp.max(s, axis=-1, keepdims=True))
        o = jnp.dot(e.astype(BF), v_ref[0, :, sl], preferred_element_type=F32) / jnp.sum(e, axis=-1, keepdims=True)
        o_ref[0, :, sl] = _pair_merge(o, nq, first_half).astype(BF)


def _context_attention(q, k, v):
    b, c, w = q.shape
    spec = pl.BlockSpec((1, c, w), lambda bi: (bi, 0, 0))
    return pl.pallas_call(
        _ctx_attn_kernel,
        grid=(b,),
        in_specs=[spec] * 3,
        out_specs=spec,
        out_shape=jax.ShapeDtypeStruct((b, c, w), BF),
        compiler_params=_cparams(("parallel",)),
        name="context_attention",
    )(q, k, v)


def _out_ffn_even_kernel(x_ref, oa_ref, ob_ref, mod_ref, ng_ref, woa_ref, wob_ref, w1_ref, w3_ref, w2_ref, o_ref,
                         *, f_chunks):
    mix = (jnp.dot(oa_ref[0], woa_ref[...], preferred_element_type=F32)
           + jnp.dot(ob_ref[0], wob_ref[...], preferred_element_type=F32))
    x1 = x_ref[0] + mod_ref[0, 2:3, :] * mix
    h = _norm_mod(x1, ng_ref[...], mod_ref[0, 3:4, :], mod_ref[0, 4:5, :]).astype(BF)
    f = w1_ref.shape[1]
    fc = -(-f // (f_chunks * MXU_TILE)) * MXU_TILE
    y = None
    for lo in range(0, f, fc):
        hi = min(lo + fc, f)
        a = jnp.dot(h, w1_ref[:, lo:hi], preferred_element_type=F32)
        g = jnp.dot(h, w3_ref[:, lo:hi], preferred_element_type=F32)
        part = jnp.dot((_silu(a) * g).astype(BF), w2_ref[lo:hi, :], preferred_element_type=F32)
        y = part if y is None else y + part
    o_ref[0] = x1 + mod_ref[0, 5:6, :] * y


def _out_ffn_even(x, oa, ob, mods, mod_row, norm_gain, w_out, w1, w3, w2):
    b, n, d = x.shape
    tm = _token_tile(n)
    wa = oa.shape[2]
    tok = lambda width: pl.BlockSpec((1, tm, width), lambda bi, i: (bi, i, 0))
    return pl.pallas_call(
        functools.partial(_out_ffn_even_kernel, f_chunks=2),
        grid=(b, n // tm),
        in_specs=[tok(d), tok(wa), tok(ob.shape[2]), _mod_spec(d, mod_row), _const_spec((1, d)),
                  _const_spec((wa, d)), _const_spec((w_out.shape[0] - wa, d)),
                  _const_spec(w1.shape), _const_spec(w3.shape), _const_spec(w2.shape)],
        out_specs=tok(d),
        out_shape=jax.ShapeDtypeStruct((b, n, d), F32),
        compiler_params=_cparams(("parallel", "parallel")),
        name="out_ffn_even",
    )(x, oa, ob, mods, norm_gain, w_out[:wa], w_out[wa:], w1, w3, w2)


def _rope(t, cos, sin_signed):
    width = t.shape[1]
    lane = lax.broadcasted_iota(jnp.int32, t.shape, 1)
    partner = jnp.where(lane % 2 == 0, pltpu.roll(t, width - 1, 1), pltpu.roll(t, 1, 1))
    reps = width // LANES
    return t * jnp.concatenate([cos] * reps, axis=1) + partner * jnp.concatenate([sin_signed] * reps, axis=1)


def _in_odd_kernel(*refs, is_latent):
    if is_latent:
        (x_ref, mod_ref, ng_ref, w_ref, qg_ref, kg_ref, bdq_ref, bdk_ref, cos_ref, sin_ref,
         q_ref, kk_ref, vv_ref, u_ref) = refs
    else:
        x_ref, mod_ref, ng_ref, w_ref, kg_ref, bdk_ref, kk_ref, vv_ref, u_ref = refs
    h = _norm_mod(x_ref[0], ng_ref[...], mod_ref[0, 0:1, :], mod_ref[0, 1:2, :])
    p = jnp.dot(h.astype(BF), w_ref[...], preferred_element_type=F32)
    c0 = GQ_WIDTH if is_latent else 0
    kk = _head_norm(p[:, c0:c0 + GKV_PAD], bdk_ref[...], kg_ref[...])
    if is_latent:
        q = _head_norm(p[:, 0:GQ_WIDTH], bdq_ref[...], qg_ref[...])
        q_ref[0] = _rope(q, cos_ref[...], sin_ref[...]).astype(BF)
        kk = _rope(kk, cos_ref[...], sin_ref[...])
    kk_ref[0] = kk.astype(BF)
    vv_ref[0] = p[:, c0 + GKV_PAD:c0 + 2 * GKV_PAD].astype(BF)
    u_ref[...] = p[:, c0 + 2 * GKV_PAD:].astype(BF)


def _in_odd(x, mods, mod_row, norm_gain, w_ext, q_gain, k_gain, bdq, bdk, cos, sin_signed, is_latent):
    b, n, d = x.shape
    tm = _token_tile(n)
    tok = lambda width: pl.BlockSpec((1, tm, width), lambda bi, i: (bi, i, 0))
    vec = lambda width: _const_spec((1, width))
    if is_latent:
        rope_spec = pl.BlockSpec((tm, LANES), lambda bi, i: (i, 0))
        in_specs = [tok(d), _mod_spec(d, mod_row), vec(d), _const_spec(w_ext.shape), vec(GQ_WIDTH), vec(GKV_PAD),
                    _const_spec(bdq.shape), _const_spec(bdk.shape), rope_spec, rope_spec]
        args = (x, mods, norm_gain, w_ext, q_gain, k_gain, bdq, bdk, cos, sin_signed)
        widths = [GQ_WIDTH, GKV_PAD, GKV_PAD]
    else:
        in_specs = [tok(d), _mod_spec(d, mod_row), vec(d), _const_spec(w_ext.shape), vec(GKV_PAD),
                    _const_spec(bdk.shape)]
        args = (x, mods, norm_gain, w_ext, k_gain, bdk)
        widths = [GKV_PAD, GKV_PAD]
    u_spec = pl.BlockSpec((tm, S5_WIDTH), lambda bi, i: (i, bi))
    return pl.pallas_call(
        functools.partial(_in_odd_kernel, is_latent=is_latent),
        grid=(b, n // tm),
        in_specs=in_specs,
        out_specs=[tok(wd) for wd in widths] + [u_spec],
        out_shape=[jax.ShapeDtypeStruct((b, n, wd), BF) for wd in widths]
        + [jax.ShapeDtypeStruct((n, b * S5_WIDTH), BF)],
        compiler_params=_cparams(("parallel", "parallel")),
        name="in_odd_latent" if is_latent else "in_odd_context",
    )(*args)


def _rope_tables(n):
    t = jnp.arange(n)
    row = (t // GRID_W).astype(F32)
    col = (t % GRID_W).astype(F32)
    n_axis = HEAD_DIM // 4
    freqs = ROPE_THETA ** (-jnp.arange(n_axis, dtype=F32) / n_axis)
    ang = jnp.concatenate([row[:, None] * freqs, col[:, None] * freqs], axis=-1)
    ang = jnp.repeat(ang, 2, axis=-1)
    ang = jnp.concatenate([ang, ang], axis=-1)
    sign = jnp.where(jnp.arange(LANES) % 2 == 0, -1.0, 1.0).astype(F32)
    return jnp.cos(ang), jnp.sin(ang) * sign


def _gqa_kernel(q_ref, kk_ref, vv_ref, o_ref):
    tq = q_ref.shape[1]
    first_half = lax.broadcasted_iota(jnp.int32, (tq, LANES), 1) < HEAD_DIM
    group = GQ_HEADS // GKV_HEADS
    for j in range(GKV_HEADS):
        kj = kk_ref[0, :, j * LANES:(j + 1) * LANES]
        vj = vv_ref[0, :, j * LANES:(j + 1) * LANES]
        base = j * group * HEAD_DIM
        qs = jnp.concatenate(
            [_pair_stack(q_ref[0, :, base + i * LANES:base + (i + 1) * LANES], first_half)
             for i in range(group // 2)], axis=0)
        s = lax.dot_general(qs, kj, NT_DIMS, preferred_element_type=F32)
        e = jnp.exp(s - jnp.max(s, axis=-1, keepdims=True))
        o = jnp.dot(e.astype(BF), vj, preferred_element_type=F32) / jnp.sum(e, axis=-1, keepdims=True)
        for i in range(group // 2):
            pair = _pair_merge(o[2 * i * tq:(2 * i + 2) * tq], tq, first_half)
            o_ref[0, :, base + i * LANES:base + (i + 1) * LANES] = pair.astype(BF)


def _gqa(q, kk, vv):
    b, n, w = q.shape
    nk = kk.shape[1]
    tq = 128
    kv_spec = pl.BlockSpec((1, nk, GKV_PAD), lambda bi, i: (bi, 0, 0))
    return pl.pallas_call(
        _gqa_kernel,
        grid=(b, n // tq),
        in_specs=[pl.BlockSpec((1, tq, w), lambda bi, i: (bi, i, 0)), kv_spec, kv_spec],
        out_specs=pl.BlockSpec((1, tq, w), lambda bi, i: (bi, i, 0)),
        out_shape=jax.ShapeDtypeStruct((b, n, w), BF),
        compiler_params=_cparams(("parallel", "arbitrary")),
        name="gqa",
    )(q, kk, vv)


def _s5_kernel(u_ref, bbd_ref, lre_ref, lim_ref, cbd_ref, y_ref, bu_ref, st_ref, *, steps, bp):
    backward = pl.program_id(0) == 1

    @pl.when(pl.program_id(1) == 0)
    def _():
        st_ref[...] = jnp.zeros_like(st_ref)

    bu_ref[...] = jnp.dot(u_ref[...], bbd_ref[0], preferred_element_type=F32)
    lc = 2 * LANES
    for j in range(S5_NP // lc):
        re_sl = slice(j * lc, (j + 1) * lc)
        im_sl = slice(S5_NP + j * lc, S5_NP + (j + 1) * lc)
        lre = jnp.broadcast_to(lre_ref[0, :, re_sl], (bp, lc))
        lim = jnp.broadcast_to(lim_ref[0, :, re_sl], (bp, lc))

        def step(s, carry):
            xre, xim = carry
            t = jnp.where(backward, steps - 1 - s, s)
            r0 = pl.multiple_of(t * bp, bp)
            nre = lre * xre - lim * xim + bu_ref[pl.ds(r0, bp), re_sl]
            nim = lre * xim + lim * xre + bu_ref[pl.ds(r0, bp), im_sl]
            bu_ref[pl.ds(r0, bp), re_sl] = nre
            bu_ref[pl.ds(r0, bp), im_sl] = nim
            return nre, nim

        xre, xim = lax.fori_loop(0, steps, step, (st_ref[:, re_sl], st_ref[:, im_sl]), unroll=2)
        st_ref[:, re_sl] = xre
        st_ref[:, im_sl] = xim
    y_ref[0] = jnp.dot(bu_ref[...].astype(BF), cbd_ref[0], preferred_element_type=F32)


def _s5_params(a_re, a_im, log_step, b_re, b_im, c_re, c_im):
    a_re, a_im = a_re.astype(F32), a_im.astype(F32)
    step = jnp.exp(log_step.astype(F32))[..., None]
    mag = jnp.exp(a_re * step)
    lre, lim = mag * jnp.cos(a_im * step), mag * jnp.sin(a_im * step)
    den = a_re * a_re + a_im * a_im
    kre = ((lre - 1.0) * a_re + lim * a_im) / den
    kim = (lim * a_re - (lre - 1.0) * a_im) / den
    bre = kre[..., None] * b_re - kim[..., None] * b_im
    bim = kre[..., None] * b_im + kim[..., None] * b_re
    eye = jnp.eye(S5_GROUPS, dtype=F32)

    def in_blockdiag(m):
        return jnp.einsum('dgpn,gh->dgnhp', m, eye).reshape(2, S5_WIDTH, S5_NP)

    def out_blockdiag(m):
        return jnp.einsum('dgnp,gh->dgphn', m, eye).reshape(2, S5_NP, S5_WIDTH)

    bbd = jnp.concatenate([in_blockdiag(bre), in_blockdiag(bim)], axis=2).astype(BF)
    cbd = jnp.concatenate([out_blockdiag(c_re.astype(F32)), -out_blockdiag(c_im.astype(F32))], axis=1).astype(BF)
    return bbd, lre.reshape(2, 1, S5_NP), lim.reshape(2, 1, S5_NP), cbd


def _s5_scan(u_seq, n_ctx_rows, bbd, lre, lim, cbd, bp):
    rows = u_seq.shape[0]
    steps = 32
    rb = steps * bp
    assert rows % rb == 0 and n_ctx_rows % rb == 0
    nc = n_ctx_rows // rb
    nl = rows // rb - nc

    def in_block(d, i):
        rev = jnp.where(i < nc, nc - 1 - i, nc + nl - 1 - (i - nc))
        return jnp.where(d == 0, i, rev)

    def out_block(d, i):
        k = jnp.maximum(i - nc, 0)
        return jnp.where(d == 0, k, nl - 1 - k)

    per_dir = lambda shape: pl.BlockSpec((1,) + shape, lambda d, i: (d, 0, 0))
    return pl.pallas_call(
        functools.partial(_s5_kernel, steps=steps, bp=bp),
        grid=(2, nc + nl),
        in_specs=[pl.BlockSpec((rb, S5_WIDTH), lambda d, i: (in_block(d, i), 0)), per_dir((S5_WIDTH, 2 * S5_NP)),
                  per_dir((1, S5_NP)), per_dir((1, S5_NP)), per_dir((2 * S5_NP, S5_WIDTH))],
        out_specs=pl.BlockSpec((1, rb, S5_WIDTH), lambda d, i: (d, out_block(d, i), 0)),
        out_shape=jax.ShapeDtypeStruct((2, nl * rb, S5_WIDTH), F32),
        scratch_shapes=[pltpu.VMEM((rb, 2 * S5_NP), F32), pltpu.VMEM((bp, 2 * S5_NP), F32)],
        compiler_params=_cparams(("arbitrary", "arbitrary")),
        name="s5_scan",
    )(u_seq, bbd, lre, lim, cbd)


def _s5_mixer_latent(u_lat, u_ctx, b, params):
    n, c = u_lat.shape[0], u_ctx.shape[0]
    bp = -(-b // 8) * 8
    seq = jnp.concatenate([u_ctx, u_lat], axis=0).reshape(c + n, b, S5_WIDTH)
    seq = jnp.pad(seq, ((0, 0), (0, bp - b), (0, 0))).reshape((c + n) * bp, S5_WIDTH)
    y = _s5_scan(seq, c * bp, *params, bp)
    return (y[0] + y[1]).reshape(n, bp, S5_WIDTH)[:, :b].reshape(n, b * S5_WIDTH)


def _out_odd_kernel(x_ref, oa_ref, ys_ref, u_ref, mod_ref, ng_ref, d_ref, gw_ref, gb_ref, woa_ref, wob_ref,
                    rw_ref, rb_ref, x1_ref, h_ref, gate_ref):
    y = ys_ref[...] + d_ref[...] * u_ref[...].astype(F32)
    gl = 0.5 * y * (1.0 + jnp.tanh(np.sqrt(2.0 / np.pi) * (y + 0.044715 * (y * y * y))))
    z = gl * jax.nn.sigmoid(jnp.dot(gl.astype(BF), gw_ref[...], preferred_element_type=F32) + gb_ref[...])
    mix = (jnp.dot(oa_ref[0], woa_ref[...], preferred_element_type=F32)
           + jnp.dot(z.astype(BF), wob_ref[...], preferred_element_type=F32))
    x1 = x_ref[0] + mod_ref[0, 2:3, :] * mix
    x1_ref[0] = x1
    h = _norm_mod(x1, ng_ref[...], mod_ref[0, 3:4, :], mod_ref[0, 4:5, :]).astype(BF)
    h_ref[0] = h
    logits = jnp.dot(h, rw_ref[...], preferred_element_type=F32) + rb_ref[...]
    lane = lax.broadcasted_iota(jnp.int32, logits.shape, 1)
    m1 = jnp.max(logits, axis=-1, keepdims=True)
    i1 = jnp.min(jnp.where(logits == m1, lane, LANES), axis=-1, keepdims=True)
    rest = jnp.where(lane == i1, -jnp.inf, logits)
    m2 = jnp.max(rest, axis=-1, keepdims=True)
    i2 = jnp.min(jnp.where(rest == m2, lane, LANES), axis=-1, keepdims=True)
    e2 = jnp.exp(m2 - m1)
    gate_ref[0] = jnp.where(lane == i1, 1.0 / (1.0 + e2), 0.0) + jnp.where(lane == i2, e2 / (1.0 + e2), 0.0)


def _out_odd(x, oa, ys, u, mods, norm_gain, d_skip, glu_w, glu_b, w_out, router_w, router_b):
    b, n, d = x.shape
    tm = _token_tile(n)
    wa = oa.shape[2]
    tok = lambda width: pl.BlockSpec((1, tm, width), lambda bi, i: (bi, i, 0))
    vec = lambda width: _const_spec((1, width))
    tmajor = pl.BlockSpec((tm, S5_WIDTH), lambda bi, i: (i, bi))
    return pl.pallas_call(
        _out_odd_kernel,
        grid=(b, n // tm),
        in_specs=[tok(d), tok(wa), tmajor, tmajor, _mod_spec(d, None), vec(d), vec(S5_WIDTH),
                  _const_spec(glu_w.shape), vec(S5_WIDTH), _const_spec((wa, d)), _const_spec((S5_WIDTH, d)),
                  _const_spec(router_w.shape), vec(LANES)],
        out_specs=[tok(d), tok(d), tok(LANES)],
        out_shape=[jax.ShapeDtypeStruct((b, n, d), F32), jax.ShapeDtypeStruct((b, n, d), BF),
                   jax.ShapeDtypeStruct((b, n, LANES), F32)],
        compiler_params=_cparams(("parallel", "parallel")),
        name="out_odd",
    )(x, oa, ys, u, mods, norm_gain, d_skip, glu_w, glu_b, w_out[:wa], w_out[wa:], router_w, router_b)


MOE_CHUNK = 512
MOE_ROWS = 256
MOE_F_CHUNKS = 4
MOE_ALIGN = 16


def _moe_capacity(tb):
    need = 2 * tb + N_EXPERTS * MOE_ALIGN + MOE_CHUNK
    return -(-need // MOE_CHUNK) * MOE_CHUNK


def _moe_kernel(h_ref, gate_ref, x1_ref, mod_ref, w1_ref, w3_ref, w2_ref, o_ref, xy_ref, tok_ref, tokt_ref,
                cnt_ref, seg_ref):
    e = pl.program_id(1)
    q = pl.program_id(2)
    last_e = pl.num_programs(1) - 1
    last_q = pl.num_programs(2) - 1
    tb, d = h_ref.shape[1:]
    cap = xy_ref.shape[0]
    ch = MOE_CHUNK

    @pl.when((e == 0) & (q == 0))
    def _pack():
        lane1 = lax.broadcasted_iota(jnp.int32, (1, LANES), 1)
        total = jnp.zeros((1, LANES), F32)
        for c in range(tb // ch):
            total = total + jnp.sum((gate_ref[0, c * ch:(c + 1) * ch, :] > 0.0).astype(F32), axis=0, keepdims=True)
        seg_vec = jnp.zeros((1, LANES), F32)
        start = jnp.int32(0)
        for k in range(N_EXPERTS):
            nk = jnp.sum(jnp.where(lane1 == k, total, 0.0)).astype(jnp.int32)
            cnt_ref[k] = nk
            seg_ref[k] = start
            seg_vec = jnp.where(lane1 == k, start.astype(F32), seg_vec)
            start = start + (nk + MOE_ALIGN - 1) // MOE_ALIGN * MOE_ALIGN
        tri = (lax.broadcasted_iota(jnp.int32, (ch, ch), 0)
               > lax.broadcasted_iota(jnp.int32, (ch, ch), 1)).astype(BF)
        lane = lax.broadcasted_iota(jnp.int32, (ch, LANES), 1)
        offset = jnp.zeros((1, LANES), F32)
        for c in range(tb // ch):
            g = gate_ref[0, c * ch:(c + 1) * ch, :]
            sel = g > 0.0
            sel_f = sel.astype(F32)
            rank = jnp.dot(tri, sel_f.astype(BF), preferred_element_type=F32) + offset
            pos = jnp.where(sel, rank + seg_vec, -1.0)
            pmax = jnp.max(pos, axis=-1, keepdims=True)
            pmin = jnp.min(jnp.where(sel, pos, 1e9), axis=-1, keepdims=True)
            gmin = jnp.sum(jnp.where(pos == pmin, g, 0.0), axis=-1, keepdims=True)
            gmax = jnp.sum(jnp.where(pos == pmax, g, 0.0), axis=-1, keepdims=True)
            pmax = jnp.where(pmax == pmin, -1.0, pmax)
            tok_ref[c * ch:(c + 1) * ch, :] = jnp.where(
                lane == 0, pmin, jnp.where(lane == 1, pmax, jnp.where(lane == 2, gmin, jnp.where(lane == 3, gmax, 0.0))))
            offset = offset + jnp.sum(sel_f, axis=0, keepdims=True)
        tokt_ref[...] = tok_ref[...].T
        pmin_t, pmax_t = tokt_ref[0:1, :], tokt_ref[1:2, :]
        row_id = lax.broadcasted_iota(jnp.int32, (ch, tb), 0).astype(F32)

        def pack_chunk(c, carry):
            r0 = pl.multiple_of(c * ch, ch)
            rid = row_id + r0.astype(F32)
            onehot = jnp.where(rid == pmin_t, 1.0, jnp.where(rid == pmax_t, 1.0, 0.0)).astype(BF)
            xy_ref[pl.ds(r0, ch), :] = jnp.dot(onehot, h_ref[0], preferred_element_type=F32).astype(BF)
            return carry

        lax.fori_loop(0, cap // ch, pack_chunk, 0)

    n = cnt_ref[e]
    seg = seg_ref[e]

    def ffn_tile(r0, size):
        src = pl.multiple_of(seg + r0, MOE_ALIGN)
        x = xy_ref[pl.ds(src, size), :]
        a = jnp.dot(x, w1_ref[0], preferred_element_type=F32)
        g = jnp.dot(x, w3_ref[0], preferred_element_type=F32)
        y = jnp.dot((_silu(a) * g).astype(BF), w2_ref[0], preferred_element_type=F32)

        @pl.when(q == 0)
        def _():
            o_ref[0, pl.ds(r0, size), :] = y

        @pl.when((q > 0) & (q < last_q))
        def _():
            o_ref[0, pl.ds(r0, size), :] += y

        @pl.when(q == last_q)
        def _():
            rid = lax.broadcasted_iota(jnp.int32, (size, tb), 0).astype(F32) + src.astype(F32)
            gate_rows = jnp.sum(jnp.where(rid == tokt_ref[0:1, :], tokt_ref[2:3, :], 0.0)
                                + jnp.where(rid == tokt_ref[1:2, :], tokt_ref[3:4, :], 0.0), axis=-1, keepdims=True)
            mine = lax.broadcasted_iota(jnp.int32, (size, 1), 0) < n - r0
            res = gate_rows * (o_ref[0, pl.ds(r0, size), :] + y)
            xy_ref[pl.ds(src, size), :] = jnp.where(mine, res.astype(BF), x)

    rows = MOE_ROWS
    n_full = n // rows
    tail = n - n_full * rows

    def full_tile(i, carry):
        ffn_tile(pl.multiple_of(i * rows, rows), rows)
        return carry

    lax.fori_loop(0, n_full, full_tile, 0)
    tail_r0 = pl.multiple_of(n_full * rows, rows)

    @pl.when(tail > rows // 2)
    def _():
        ffn_tile(tail_r0, rows)

    @pl.when((tail > 0) & (tail <= rows // 2))
    def _():
        ffn_tile(tail_r0, rows // 2)

    @pl.when((e == last_e) & (q == last_q))
    def _combine():
        col_id = lax.broadcasted_iota(jnp.int32, (ch, cap), 1).astype(F32)

        def combine_chunk(c, carry):
            tok_rows = pl.ds(pl.multiple_of(c * ch, ch), ch)
            pmin, pmax = tok_ref[tok_rows, 0:1], tok_ref[tok_rows, 1:2]
            onehot = jnp.where(col_id == pmin, 1.0, jnp.where(col_id == pmax, 1.0, 0.0)).astype(BF)
            mix = jnp.dot(onehot, xy_ref[...], preferred_element_type=F32)
            o_ref[0, tok_rows, :] = x1_ref[0, tok_rows, :] + mod_ref[0, 5:6, :] * mix
            return carry

        lax.fori_loop(0, tb // ch, combine_chunk, 0)


def _moe(h, gates, x1, mods, w1, w3, w2):
    b, n, d = h.shape
    n_exp, _, f = w1.shape
    assert n_exp == N_EXPERTS and n % MOE_CHUNK == 0 and MOE_F_CHUNKS > 1
    fc = f // MOE_F_CHUNKS
    per_batch = lambda width: pl.BlockSpec((1, n, width), lambda bi, e, q: (bi, 0, 0), pipeline_mode=pl.Buffered(1))
    return pl.pallas_call(
        _moe_kernel,
        grid=(b, n_exp, MOE_F_CHUNKS),
        in_specs=[per_batch(d), per_batch(LANES), per_batch(d),
                  pl.BlockSpec((1, N_MOD, d), lambda bi, e, q: (bi, 0, 0)),
                  pl.BlockSpec((1, d, fc), lambda bi, e, q: (e, 0, q)),
                  pl.BlockSpec((1, d, fc), lambda bi, e, q: (e, 0, q)),
                  pl.BlockSpec((1, fc, d), lambda bi, e, q: (e, q, 0))],
        out_specs=per_batch(d),
        out_shape=jax.ShapeDtypeStruct((b, n, d), F32),
        scratch_shapes=[pltpu.VMEM((_moe_capacity(n), d), BF), pltpu.VMEM((n, LANES), F32),
                        pltpu.VMEM((LANES, n), F32), pltpu.SMEM((N_EXPERTS,), jnp.int32),
                        pltpu.SMEM((N_EXPERTS,), jnp.int32)],
        compiler_params=_cparams(("arbitrary", "arbitrary", "arbitrary")),
        name="moe",
    )(h, gates, x1, mods, w1, w3, w2)


def _head_blockdiag(width):
    head = np.arange(width) // HEAD_DIM
    return jnp.asarray((head[:, None] == head[None, :]).astype(np.float32) / HEAD_DIM, dtype=BF)


def _tile_gain(gain, width, scale=1.0):
    return (jnp.tile(gain.astype(F32), width // HEAD_DIM) * scale).reshape(1, width)


def _even_layer(x, xc, mods, ctx_row, norm_mix, norm_ffn, w_in, q_gain, k_gain, rpb, w_out, w1, w3, w2):
    d = x.shape[2]
    bd = _head_blockdiag(NA_WIDTH)
    w_in = w_in.astype(BF)
    qg = _tile_gain(q_gain, NA_WIDTH, HEAD_DIM ** -0.5)
    kg = _tile_gain(k_gain, NA_WIDTH)
    ng_mix, ng_ffn = norm_mix.reshape(1, d), norm_ffn.reshape(1, d)
    u, q, k, v = _in_even(x, mods, None, ng_mix, w_in, qg, kg, bd)
    uc, qc, kc, vc = _in_even(xc, mods, ctx_row, ng_mix, w_in, qg, kg, bd)
    o_na = _neighbourhood_attention(q, k, v, kc, vc, rpb)
    o_ctx = _context_attention(qc, kc, vc)
    w_out, w1, w3, w2 = (t.astype(BF) for t in (w_out, w1, w3, w2))
    x = _out_ffn_even(x, _fourier(u), o_na, mods, None, ng_ffn, w_out, w1, w3, w2)
    xc = _out_ffn_even(xc, _fourier(uc), o_ctx, mods, ctx_row, ng_ffn, w_out, w1, w3, w2)
    return x, xc


def _odd_layer(x, xc, mods, ctx_row, norm_mix, norm_ffn, w_in, q_gain, k_gain, s5, d_skip, glu_w, glu_b, w_out,
               router_w, router_b, ew1, ew3, ew2):
    b, n, d = x.shape
    kvw = GKV_HEADS * HEAD_DIM
    wq, wk, wv, wu = jnp.split(w_in, [GQ_WIDTH, GQ_WIDTH + kvw, GQ_WIDTH + 2 * kvw], axis=1)

    def twice(w):
        return jnp.repeat(w.reshape(d, GKV_HEADS, 1, HEAD_DIM), 2, axis=2).reshape(d, GKV_PAD)

    w_lat = jnp.concatenate([wq, twice(wk), twice(wv), wu], axis=1).astype(BF)
    w_ctx = jnp.concatenate([twice(wk), twice(wv), wu], axis=1).astype(BF)
    qg = _tile_gain(q_gain, GQ_WIDTH, HEAD_DIM ** -0.5)
    kg = _tile_gain(k_gain, GKV_PAD)
    bdq, bdk = _head_blockdiag(GQ_WIDTH), _head_blockdiag(GKV_PAD)
    cos, sin_signed = _rope_tables(n)
    ng_mix, ng_ffn = norm_mix.reshape(1, d), norm_ffn.reshape(1, d)
    q, kk, vv, u = _in_odd(x, mods, None, ng_mix, w_lat, qg, kg, bdq, bdk, cos, sin_signed, True)
    kkc, vvc, uc = _in_odd(xc, mods, ctx_row, ng_mix, w_ctx, None, kg, None, bdk, None, None, False)
    o_attn = _gqa(q, jnp.concatenate([kkc, kk], axis=1), jnp.concatenate([vvc, vv], axis=1))
    y_ssm = _s5_mixer_latent(u, uc, b, _s5_params(*s5))
    rw = jnp.pad(router_w, ((0, 0), (0, LANES - N_EXPERTS))).astype(BF)
    rb = jnp.pad(router_b.astype(F32), (0, LANES - N_EXPERTS), constant_values=NEG_INF).reshape(1, LANES)
    x1, h, gates = _out_odd(x, o_attn, y_ssm, u, mods, ng_ffn, d_skip.reshape(1, S5_WIDTH).astype(F32),
                            glu_w.astype(BF), glu_b.reshape(1, S5_WIDTH).astype(F32), w_out.astype(BF), rw, rb)
    return _moe(h, gates, x1, mods, ew1.astype(BF), ew3.astype(BF), ew2.astype(BF))


def kernel(x, c, ctx, c_ctx, ev_mod_w, ev_mod_b, ev_norm_mix, ev_norm_ffn, ev_w_in, ev_q_gain, ev_k_gain, ev_rpb,
           ev_w_out, ev_ffn_w1, ev_ffn_w3, ev_ffn_w2, od_mod_w, od_mod_b, od_norm_mix, od_norm_ffn, od_w_in,
           od_q_gain, od_k_gain, od_s5_a_re, od_s5_a_im, od_s5_log_step, od_s5_b_re, od_s5_b_im, od_s5_c_re,
           od_s5_c_im, od_s5_d, od_s5_glu_w, od_s5_glu_b, od_w_out, od_router_w, od_router_b, od_exp_w1, od_exp_w3,
           od_exp_w2):
    assert ev_mod_w.shape[0] == 1 and od_mod_w.shape[0] == 1, "one even and one odd layer"
    b, n, d = x.shape
    rows = -(-(b + 1) // 8) * 8
    cond = jnp.zeros((rows, d), F32).at[:b].set(c).at[b].set(c_ctx)
    mods_even = _modulation(cond, ev_mod_w[0], ev_mod_b[0])
    mods_odd = _modulation(cond, od_mod_w[0], od_mod_b[0])
    x, xc = _even_layer(x, ctx, mods_even, b, ev_norm_mix[0], ev_norm_ffn[0], ev_w_in[0], ev_q_gain[0], ev_k_gain[0],
                        ev_rpb[0], ev_w_out[0], ev_ffn_w1[0], ev_ffn_w3[0], ev_ffn_w2[0])
    s5 = (od_s5_a_re[0], od_s5_a_im[0], od_s5_log_step[0], od_s5_b_re[0], od_s5_b_im[0], od_s5_c_re[0], od_s5_c_im[0])
    return _odd_layer(x, xc, mods_odd, b, od_norm_mix[0], od_norm_ffn[0], od_w_in[0], od_q_gain[0], od_k_gain[0],
                      s5, od_s5_d[0], od_s5_glu_w[0], od_s5_glu_b[0], od_w_out[0], od_router_w[0], od_router_b[0],
                      od_exp_w1[0], od_exp_w3[0], od_exp_w2[0])
```

```python
import functools

import numpy as np
import jax
import jax.numpy as jnp
from jax import lax
from jax.experimental import pallas as pl
from jax.experimental.pallas import tpu as pltpu

BF = jnp.bfloat16
F32 = jnp.float32

EPS = 1e-6
NEG_INF = -1e30
GRID_W = 64
LANES = 128
MXU_TILE = 256
HEAD_DIM = 64
LOG2E = float(np.log2(np.e))
Q_SCALE = HEAD_DIM ** -0.5 * LOG2E
FN_GROUPS, FN_GROUP_DIM = 4, 128
FN_WIDTH = FN_GROUPS * FN_GROUP_DIM
NA_HEADS = 8
NA_WIDTH = NA_HEADS * HEAD_DIM
NA_KH, NA_KW = 8, 16
NA_QROWS = 4
NA_WIN = NA_QROWS + NA_KH - 1
GQ_HEADS, GKV_HEADS = 12, 3
GQ_WIDTH = GQ_HEADS * HEAD_DIM
GKV_PAD = GKV_HEADS * LANES
S5_GROUPS, S5_GROUP_DIM, S5_STATE = 16, 16, 64
S5_WIDTH = S5_GROUPS * S5_GROUP_DIM
S5_NP = S5_GROUPS * S5_STATE
N_EXPERTS = 8
N_MOD = 6
ROPE_THETA = 10000.0
VMEM_LIMIT = 56 * 1024 * 1024

NT_DIMS = (((1,), (1,)), ((), ()))


def _cparams(sem):
    return pltpu.CompilerParams(dimension_semantics=sem, vmem_limit_bytes=VMEM_LIMIT)


def _const_spec(shape):
    nd = len(shape)
    return pl.BlockSpec(shape, lambda *_: (0,) * nd, pipeline_mode=pl.Buffered(1))


def _token_tile(n):
    return 512 if n % 512 == 0 else 256


def _mod_spec(d, row):
    if row is None:
        return pl.BlockSpec((1, N_MOD, d), lambda b, i: (b, 0, 0))
    return pl.BlockSpec((1, N_MOD, d), lambda b, i: (row, 0, 0))


def _norm_mod(x, gain, shift, scale):
    ms = jnp.mean(x * x, axis=-1, keepdims=True)
    return (x * lax.rsqrt(ms + EPS) * gain) * (1.0 + scale) + shift


def _head_norm(t, blockdiag, gain):
    ms = jnp.dot((t * t).astype(BF), blockdiag, preferred_element_type=F32)
    return t * lax.rsqrt(ms + EPS) * gain


def _silu(a):
    return a * jax.nn.sigmoid(a)


def _mod_kernel(c_ref, w_ref, b_ref, o_ref):
    a = _silu(c_ref[...]).astype(BF)
    o_ref[...] = jnp.dot(a, w_ref[...].astype(BF), preferred_element_type=F32) + b_ref[...]


def _modulation(cond, w, b):
    r, d = cond.shape
    n = w.shape[1]
    tn = n // 4
    out = pl.pallas_call(
        _mod_kernel,
        grid=(n // tn,),
        in_specs=[pl.BlockSpec((r, d), lambda j: (0, 0)),
                  pl.BlockSpec((d, tn), lambda j: (0, j)),
                  pl.BlockSpec((1, tn), lambda j: (0, j))],
        out_specs=pl.BlockSpec((r, tn), lambda j: (0, j)),
        out_shape=jax.ShapeDtypeStruct((r, n), F32),
        compiler_params=_cparams(("arbitrary",)),
        name="modulation",
    )(cond, w, b.reshape(1, n))
    return out.reshape(r, N_MOD, d)


def _in_even_kernel(x_ref, mod_ref, ng_ref, w_ref, qg_ref, kg_ref, bd_ref, u_ref, q_ref, k_ref, v_ref):
    h = _norm_mod(x_ref[0], ng_ref[...], mod_ref[0, 0:1, :], mod_ref[0, 1:2, :])
    p = jnp.dot(h.astype(BF), w_ref[...], preferred_element_type=F32)
    bd = bd_ref[...]
    w = FN_WIDTH
    u_ref[0] = p[:, 0:w].astype(BF)
    q_ref[0] = _head_norm(p[:, w:2 * w], bd, qg_ref[...]).astype(BF)
    k_ref[0] = _head_norm(p[:, 2 * w:3 * w], bd, kg_ref[...]).astype(BF)
    v_ref[0] = p[:, 3 * w:4 * w].astype(BF)


def _in_even(x, mods, mod_row, norm_gain, w_in, q_gain, k_gain, blockdiag):
    b, n, d = x.shape
    tm = _token_tile(n)
    wout = FN_WIDTH
    tok = lambda width: pl.BlockSpec((1, tm, width), lambda bi, i: (bi, i, 0))
    return pl.pallas_call(
        _in_even_kernel,
        grid=(b, n // tm),
        in_specs=[tok(d), _mod_spec(d, mod_row), _const_spec((1, d)), _const_spec(w_in.shape),
                  _const_spec((1, wout)), _const_spec((1, wout)), _const_spec(blockdiag.shape)],
        out_specs=[tok(wout)] * 4,
        out_shape=[jax.ShapeDtypeStruct((b, n, wout), BF)] * 4,
        compiler_params=_cparams(("parallel", "parallel")),
        name="in_even",
    )(x, mods, norm_gain, w_in, q_gain, k_gain, blockdiag)


def _fourier_kernel(u_ref, cm_ref, ml_ref, o_ref, z_ref):
    n = u_ref.shape[1]
    gd = FN_GROUP_DIM
    for g in range(FN_GROUPS):
        z = jnp.dot(u_ref[0, :, g * gd:(g + 1) * gd], cm_ref[...], preferred_element_type=F32)
        z_ref[0:n, g * gd:(g + 1) * gd] = z[:, :gd].astype(BF)
        z_ref[n:2 * n, g * gd:(g + 1) * gd] = z[:, gd:].astype(BF)
    o_ref[0] = jnp.dot(ml_ref[...], z_ref[...], preferred_element_type=F32).astype(BF)


def _dft_tables(n):
    def cos_sin(size):
        j = jnp.arange(size, dtype=jnp.int32)
        ang = ((j[:, None] * j[None, :]) % size).astype(F32) * (2.0 * np.pi / size)
        return jnp.cos(ang), jnp.sin(ang)
    cc, sc = cos_sin(FN_GROUP_DIM)
    cl, sl = cos_sin(n)
    scale = 1.0 / np.sqrt(n * FN_GROUP_DIM)
    return (jnp.concatenate([cc, sc], axis=1).astype(BF),
            (jnp.concatenate([cl, -sl], axis=1) * scale).astype(BF))


def _fourier(u):
    b, n, w = u.shape
    cm, ml = _dft_tables(n)
    return pl.pallas_call(
        _fourier_kernel,
        grid=(b,),
        in_specs=[pl.BlockSpec((1, n, w), lambda bi: (bi, 0, 0)), _const_spec(cm.shape), _const_spec(ml.shape)],
        out_specs=pl.BlockSpec((1, n, w), lambda bi: (bi, 0, 0)),
        out_shape=jax.ShapeDtypeStruct((b, n, w), BF),
        scratch_shapes=[pltpu.VMEM((2 * n, w), BF)],
        compiler_params=_cparams(("parallel",)),
        name="fourier",
    )(u, cm, ml)


def _pair_stack(qp, first_half):
    zero = jnp.zeros_like(qp)
    return jnp.concatenate([jnp.where(first_half, qp, zero), jnp.where(first_half, zero, qp)], axis=0)


def _pair_merge(o, nq, first_half):
    return jnp.where(first_half, o[:nq], o[nq:])


def _na_kernel(q_ref, k_ref, v_ref, kc_ref, vc_ref, bias_ref, o_ref, *, rows):
    rb = pl.program_id(1)
    ws = jnp.minimum(jnp.maximum(NA_QROWS * rb - NA_KH // 2, 0), rows - NA_WIN)
    start = pl.multiple_of(ws * GRID_W, GRID_W)
    nq = NA_QROWS * GRID_W
    nk = NA_WIN * GRID_W
    first_half = lax.broadcasted_iota(jnp.int32, (nq, LANES), 1) < HEAD_DIM
    n_pairs = NA_HEADS // 2

    def scores(p):
        sl = slice(p * LANES, (p + 1) * LANES)
        qs = _pair_stack(q_ref[0, :, sl], first_half)
        bias = jnp.concatenate([bias_ref[0, 2 * p], bias_ref[0, 2 * p + 1]], axis=0)
        return (lax.dot_general(qs, k_ref[0, pl.ds(start, nk), sl], NT_DIMS, preferred_element_type=F32) + bias,
                lax.dot_general(qs, kc_ref[0, :, sl], NT_DIMS, preferred_element_type=F32))

    nxt = scores(0)
    for p in range(n_pairs):
        sl = slice(p * LANES, (p + 1) * LANES)
        s_loc, s_ctx = nxt
        if p + 1 < n_pairs:
            nxt = scores(p + 1)
        vw = v_ref[0, pl.ds(start, nk), sl]
        m = jnp.maximum(jnp.max(s_loc, axis=-1, keepdims=True), jnp.max(s_ctx, axis=-1, keepdims=True))
        p_loc = jnp.exp2(s_loc - m)
        p_ctx = jnp.exp2(s_ctx - m)
        denom = jnp.sum(p_loc, axis=-1, keepdims=True) + jnp.sum(p_ctx, axis=-1, keepdims=True)
        o = (jnp.dot(p_loc.astype(BF), vw, preferred_element_type=F32)
             + jnp.dot(p_ctx.astype(BF), vc_ref[0, :, sl], preferred_element_type=F32)) / denom
        o_ref[0, :, sl] = _pair_merge(o, nq, first_half).astype(BF)


def _na_block_class(rb, n_blocks):
    return int(rb > 0) + int(rb == n_blocks - 1)


def _na_bias_tables(rpb, rows):
    n_blocks = rows // NA_QROWS
    col = np.arange(GRID_W)
    col_start = np.clip(col - NA_KW // 2, 0, GRID_W - NA_KW)
    col_in = (col[None, :] >= col_start[:, None]) & (col[None, :] < col_start[:, None] + NA_KW)
    dc_idx = np.clip(col[None, :] - col[:, None] + NA_KW - 1, 0, 2 * NA_KW - 2)
    per_class = {}
    for rb in range(n_blocks):
        ws = min(max(NA_QROWS * rb - NA_KH // 2, 0), rows - NA_WIN)
        r = NA_QROWS * rb + np.arange(NA_QROWS)
        kr = ws + np.arange(NA_WIN)
        r0 = np.clip(r - NA_KH // 2, 0, rows - NA_KH)
        row_in = (kr[None, :] >= r0[:, None]) & (kr[None, :] < r0[:, None] + NA_KH)
        dr_idx = np.clip(kr[None, :] - r[:, None] + NA_KH - 1, 0, 2 * NA_KH - 2)
        valid = row_in[:, None, :, None] & col_in[None, :, None, :]
        entry = (valid, dr_idx)
        cls = _na_block_class(rb, n_blocks)
        if cls in per_class:
            assert all(np.array_equal(a, b) for a, b in zip(per_class[cls], entry)), "row-block classes differ"
        per_class[cls] = entry
    nq, nk = NA_QROWS * GRID_W, NA_WIN * GRID_W
    onehot = (np.arange(2 * NA_KW - 1)[:, None] == dc_idx.reshape(1, -1)).astype(np.float32)
    by_col = jnp.dot(rpb.astype(F32).reshape(-1, 2 * NA_KW - 1), onehot, precision=lax.Precision.HIGHEST)
    by_col = by_col.reshape(NA_HEADS, 2 * NA_KH - 1, GRID_W, GRID_W)
    tables = []
    for cls in range(3):
        valid, dr_idx = per_class[cls]
        blocks = jnp.stack([by_col[:, int(dr)] for dr in dr_idx.reshape(-1)], axis=1)
        t = blocks.reshape(NA_HEADS, NA_QROWS, NA_WIN, GRID_W, GRID_W).transpose(0, 1, 3, 2, 4)
        tables.append(jnp.where(valid.reshape(nq, nk)[None], t.reshape(NA_HEADS, nq, nk) * LOG2E, NEG_INF))
    return jnp.stack(tables)


def _neighbourhood_attention(q, k, v, kc, vc, rpb):
    b, n, w = q.shape
    c = kc.shape[1]
    rows = n // GRID_W
    assert rows % NA_QROWS == 0 and rows >= NA_WIN + 1
    n_blocks = rows // NA_QROWS
    bias = _na_bias_tables(rpb, rows)
    nq, nk = NA_QROWS * GRID_W, NA_WIN * GRID_W
    full = lambda length: pl.BlockSpec((1, length, w), lambda bi, rb: (bi, 0, 0))
    bias_spec = pl.BlockSpec(
        (1, NA_HEADS, nq, nk),
        lambda bi, rb: (jnp.minimum(rb, 1) + (rb == n_blocks - 1).astype(jnp.int32), 0, 0, 0))
    return pl.pallas_call(
        functools.partial(_na_kernel, rows=rows),
        grid=(b, n_blocks),
        in_specs=[pl.BlockSpec((1, nq, w), lambda bi, rb: (bi, rb, 0)), full(n), full(n), full(c), full(c), bias_spec],
        out_specs=pl.BlockSpec((1, nq, w), lambda bi, rb: (bi, rb, 0)),
        out_shape=jax.ShapeDtypeStruct((b, n, w), BF),
        compiler_params=_cparams(("parallel", "arbitrary")),
        name="neighbourhood_attention",
    )(q, k, v, kc, vc, bias)


def _ctx_attn_kernel(q_ref, k_ref, v_ref, o_ref):
    nq = q_ref.shape[1]
    first_half = lax.broadcasted_iota(jnp.int32, (nq, LANES), 1) < HEAD_DIM
    for p in range(NA_HEADS // 2):
        sl = slice(p * LANES, (p + 1) * LANES)
        qs = _pair_stack(q_ref[0, :, sl], first_half)
        s = lax.dot_general(qs, k_ref[0, :, sl], NT_DIMS, preferred_element_type=F32)
        e = jnp.exp2(s - jnp.max(s, axis=-1, keepdims=True))
        o = jnp.dot(e.astype(BF), v_ref[0, :, sl], preferred_element_type=F32) / jnp.sum(e, axis=-1, keepdims=True)
        o_ref[0, :, sl] = _pair_merge(o, nq, first_half).astype(BF)


def _context_attention(q, k, v):
    b, c, w = q.shape
    spec = pl.BlockSpec((1, c, w), lambda bi: (bi, 0, 0))
    return pl.pallas_call(
        _ctx_attn_kernel,
        grid=(b,),
        in_specs=[spec] * 3,
        out_specs=spec,
        out_shape=jax.ShapeDtypeStruct((b, c, w), BF),
        compiler_params=_cparams(("parallel",)),
        name="context_attention",
    )(q, k, v)


def _out_ffn_even_kernel(x_ref, oa_ref, ob_ref, mod_ref, ng_ref, woa_ref, wob_ref, w1_ref, w3_ref, w2_ref, o_ref,
                         *, f_chunks):
    mix = (jnp.dot(oa_ref[0], woa_ref[...], preferred_element_type=F32)
           + jnp.dot(ob_ref[0], wob_ref[...], preferred_element_type=F32))
    x1 = x_ref[0] + mod_ref[0, 2:3, :] * mix
    h = _norm_mod(x1, ng_ref[...], mod_ref[0, 3:4, :], mod_ref[0, 4:5, :]).astype(BF)
    f = w1_ref.shape[1]
    fc = -(-f // (f_chunks * MXU_TILE)) * MXU_TILE
    y = None
    for lo in range(0, f, fc):
        hi = min(lo + fc, f)
        a = jnp.dot(h, w1_ref[:, lo:hi], preferred_element_type=F32)
        g = jnp.dot(h, w3_ref[:, lo:hi], preferred_element_type=F32)
        part = jnp.dot((_silu(a) * g).astype(BF), w2_ref[lo:hi, :], preferred_element_type=F32)
        y = part if y is None else y + part
    o_ref[0] = x1 + mod_ref[0, 5:6, :] * y


def _out_ffn_even(x, oa, ob, mods, mod_row, norm_gain, w_out, w1, w3, w2):
    b, n, d = x.shape
    tm = _token_tile(n)
    wa = oa.shape[2]
    tok = lambda width: pl.BlockSpec((1, tm, width), lambda bi, i: (bi, i, 0))
    return pl.pallas_call(
        functools.partial(_out_ffn_even_kernel, f_chunks=2),
        grid=(b, n // tm),
        in_specs=[tok(d), tok(wa), tok(ob.shape[2]), _mod_spec(d, mod_row), _const_spec((1, d)),
                  _const_spec((wa, d)), _const_spec((w_out.shape[0] - wa, d)),
                  _const_spec(w1.shape), _const_spec(w3.shape), _const_spec(w2.shape)],
        out_specs=tok(d),
        out_shape=jax.ShapeDtypeStruct((b, n, d), F32),
        compiler_params=_cparams(("parallel", "parallel")),
        name="out_ffn_even",
    )(x, oa, ob, mods, norm_gain, w_out[:wa], w_out[wa:], w1, w3, w2)


def _rope(t, cos, sin_signed):
    width = t.shape[1]
    lane = lax.broadcasted_iota(jnp.int32, t.shape, 1)
    partner = jnp.where(lane % 2 == 0, pltpu.roll(t, width - 1, 1), pltpu.roll(t, 1, 1))
    reps = width // LANES
    return t * jnp.concatenate([cos] * reps, axis=1) + partner * jnp.concatenate([sin_signed] * reps, axis=1)


def _in_odd_kernel(*refs, is_latent):
    if is_latent:
        (x_ref, mod_ref, ng_ref, w_ref, qg_ref, kg_ref, bdq_ref, bdk_ref, cos_ref, sin_ref,
         q_ref, kk_ref, vv_ref, u_ref) = refs
    else:
        x_ref, mod_ref, ng_ref, w_ref, kg_ref, bdk_ref, kk_ref, vv_ref, u_ref = refs
    h = _norm_mod(x_ref[0], ng_ref[...], mod_ref[0, 0:1, :], mod_ref[0, 1:2, :])
    p = jnp.dot(h.astype(BF), w_ref[...], preferred_element_type=F32)
    c0 = GQ_WIDTH if is_latent else 0
    kk = _head_norm(p[:, c0:c0 + GKV_PAD], bdk_ref[...], kg_ref[...])
    if is_latent:
        q = _head_norm(p[:, 0:GQ_WIDTH], bdq_ref[...], qg_ref[...])
        q_ref[0] = _rope(q, cos_ref[...], sin_ref[...]).astype(BF)
        kk = _rope(kk, cos_ref[...], sin_ref[...])
    kk_ref[0] = kk.astype(BF)
    vv_ref[0] = p[:, c0 + GKV_PAD:c0 + 2 * GKV_PAD].astype(BF)
    u_ref[...] = p[:, c0 + 2 * GKV_PAD:].astype(BF)


def _in_odd(x, mods, mod_row, norm_gain, w_ext, q_gain, k_gain, bdq, bdk, cos, sin_signed, is_latent):
    b, n, d = x.shape
    tm = _token_tile(n)
    tok = lambda width: pl.BlockSpec((1, tm, width), lambda bi, i: (bi, i, 0))
    vec = lambda width: _const_spec((1, width))
    if is_latent:
        rope_spec = pl.BlockSpec((tm, LANES), lambda bi, i: (i, 0))
        in_specs = [tok(d), _mod_spec(d, mod_row), vec(d), _const_spec(w_ext.shape), vec(GQ_WIDTH), vec(GKV_PAD),
                    _const_spec(bdq.shape), _const_spec(bdk.shape), rope_spec, rope_spec]
        args = (x, mods, norm_gain, w_ext, q_gain, k_gain, bdq, bdk, cos, sin_signed)
        widths = [GQ_WIDTH, GKV_PAD, GKV_PAD]
    else:
        in_specs = [tok(d), _mod_spec(d, mod_row), vec(d), _const_spec(w_ext.shape), vec(GKV_PAD),
                    _const_spec(bdk.shape)]
        args = (x, mods, norm_gain, w_ext, k_gain, bdk)
        widths = [GKV_PAD, GKV_PAD]
    u_spec = pl.BlockSpec((tm, S5_WIDTH), lambda bi, i: (i, bi))
    return pl.pallas_call(
        functools.partial(_in_odd_kernel, is_latent=is_latent),
        grid=(b, n // tm),
        in_specs=in_specs,
        out_specs=[tok(wd) for wd in widths] + [u_spec],
        out_shape=[jax.ShapeDtypeStruct((b, n, wd), BF) for wd in widths]
        + [jax.ShapeDtypeStruct((n, b * S5_WIDTH), BF)],
        compiler_params=_cparams(("parallel", "parallel")),
        name="in_odd_latent" if is_latent else "in_odd_context",
    )(*args)


def _rope_tables(n):
    t = jnp.arange(n)
    row = (t // GRID_W).astype(F32)
    col = (t % GRID_W).astype(F32)
    n_axis = HEAD_DIM // 4
    freqs = ROPE_THETA ** (-jnp.arange(n_axis, dtype=F32) / n_axis)
    ang = jnp.concatenate([row[:, None] * freqs, col[:, None] * freqs], axis=-1)
    ang = jnp.repeat(ang, 2, axis=-1)
    ang = jnp.concatenate([ang, ang], axis=-1)
    sign = jnp.where(jnp.arange(LANES) % 2 == 0, -1.0, 1.0).astype(F32)
    return jnp.cos(ang), jnp.sin(ang) * sign


def _gqa_kernel(q_ref, kk_ref, vvt_ref, o_ref):
    tq = q_ref.shape[1]
    first_half = lax.broadcasted_iota(jnp.int32, (tq, LANES), 1) < HEAD_DIM
    group = GQ_HEADS // GKV_HEADS

    def scores_t(j):
        kj = kk_ref[0, :, j * LANES:(j + 1) * LANES]
        base = j * group * HEAD_DIM
        qs = jnp.concatenate(
            [_pair_stack(q_ref[0, :, base + i * LANES:base + (i + 1) * LANES], first_half)
             for i in range(group // 2)], axis=0)
        return lax.dot_general(kj, qs, NT_DIMS, preferred_element_type=F32)

    def weights_t(st):
        return jnp.exp2(st - jnp.max(st, axis=0, keepdims=True)).astype(BF)

    def emit(j, e):
        vtj = vvt_ref[0, j * LANES:(j + 1) * LANES, :]
        base = j * group * HEAD_DIM
        ot = jnp.dot(vtj, e, preferred_element_type=F32)
        on = ot[:HEAD_DIM] * (1.0 / ot[HEAD_DIM:HEAD_DIM + 1])
        for i in range(group // 2):
            pair = jnp.concatenate([on[:, 2 * i * tq:(2 * i + 1) * tq], on[:, (2 * i + 1) * tq:(2 * i + 2) * tq]],
                                   axis=0)
            o_ref[0, :, base + i * LANES:base + (i + 1) * LANES] = pair.T.astype(BF)

    st = {j: scores_t(j) for j in range(min(2, GKV_HEADS))}
    e = {}
    for j in range(GKV_HEADS):
        e[j] = weights_t(st.pop(j))
        if j + 2 < GKV_HEADS:
            st[j + 2] = scores_t(j + 2)
        if j >= 1:
            emit(j - 1, e.pop(j - 1))
    emit(GKV_HEADS - 1, e.pop(GKV_HEADS - 1))


def _gqa(q, kk, vvt):
    b, n, w = q.shape
    nk = kk.shape[1]
    tq = LANES
    return pl.pallas_call(
        _gqa_kernel,
        grid=(b, n // tq),
        in_specs=[pl.BlockSpec((1, tq, w), lambda bi, i: (bi, i, 0)),
                  pl.BlockSpec((1, nk, GKV_PAD), lambda bi, i: (bi, 0, 0)),
                  pl.BlockSpec((1, GKV_PAD, nk), lambda bi, i: (bi, 0, 0))],
        out_specs=pl.BlockSpec((1, tq, w), lambda bi, i: (bi, i, 0)),
        out_shape=jax.ShapeDtypeStruct((b, n, w), BF),
        compiler_params=_cparams(("parallel", "arbitrary")),
        name="gqa",
    )(q, kk, vvt)


def _s5_kernel(u_ref, bbd_ref, lre_ref, lim_ref, cbd_ref, y_ref, bu_ref, st_ref, *, steps, bp):
    backward = pl.program_id(0) == 1

    @pl.when(pl.program_id(1) == 0)
    def _():
        st_ref[...] = jnp.zeros_like(st_ref)

    bu_ref[...] = jnp.dot(u_ref[...], bbd_ref[0], preferred_element_type=F32)
    lc = 2 * LANES
    for j in range(S5_NP // lc):
        re_sl = slice(j * lc, (j + 1) * lc)
        im_sl = slice(S5_NP + j * lc, S5_NP + (j + 1) * lc)
        lre = jnp.broadcast_to(lre_ref[0, :, re_sl], (bp, lc))
        lim = jnp.broadcast_to(lim_ref[0, :, re_sl], (bp, lc))

        def step(s, carry):
            xre, xim = carry
            t = jnp.where(backward, steps - 1 - s, s)
            r0 = pl.multiple_of(t * bp, bp)
            nre = lre * xre - lim * xim + bu_ref[pl.ds(r0, bp), re_sl]
            nim = lre * xim + lim * xre + bu_ref[pl.ds(r0, bp), im_sl]
            bu_ref[pl.ds(r0, bp), re_sl] = nre
            bu_ref[pl.ds(r0, bp), im_sl] = nim
            return nre, nim

        xre, xim = lax.fori_loop(0, steps, step, (st_ref[:, re_sl], st_ref[:, im_sl]), unroll=2)
        st_ref[:, re_sl] = xre
        st_ref[:, im_sl] = xim
    y_ref[0] = jnp.dot(bu_ref[...].astype(BF), cbd_ref[0], preferred_element_type=F32)


def _s5_params(a_re, a_im, log_step, b_re, b_im, c_re, c_im):
    a_re, a_im = a_re.astype(F32), a_im.astype(F32)
    step = jnp.exp(log_step.astype(F32))[..., None]
    mag = jnp.exp(a_re * step)
    lre, lim = mag * jnp.cos(a_im * step), mag * jnp.sin(a_im * step)
    den = a_re * a_re + a_im * a_im
    kre = ((lre - 1.0) * a_re + lim * a_im) / den
    kim = (lim * a_re - (lre - 1.0) * a_im) / den
    bre = kre[..., None] * b_re - kim[..., None] * b_im
    bim = kre[..., None] * b_im + kim[..., None] * b_re
    eye = jnp.eye(S5_GROUPS, dtype=F32)

    def in_blockdiag(m):
        return jnp.einsum('dgpn,gh->dgnhp', m, eye).reshape(2, S5_WIDTH, S5_NP)

    def out_blockdiag(m):
        return jnp.einsum('dgnp,gh->dgphn', m, eye).reshape(2, S5_NP, S5_WIDTH)

    bbd = jnp.concatenate([in_blockdiag(bre), in_blockdiag(bim)], axis=2).astype(BF)
    cbd = jnp.concatenate([out_blockdiag(c_re.astype(F32)), -out_blockdiag(c_im.astype(F32))], axis=1).astype(BF)
    return bbd, lre.reshape(2, 1, S5_NP), lim.reshape(2, 1, S5_NP), cbd


def _s5_scan(u_seq, n_ctx_rows, bbd, lre, lim, cbd, bp):
    rows = u_seq.shape[0]
    steps = 32
    rb = steps * bp
    assert rows % rb == 0 and n_ctx_rows % rb == 0
    nc = n_ctx_rows // rb
    nl = rows // rb - nc

    def in_block(d, i):
        rev = jnp.where(i < nc, nc - 1 - i, nc + nl - 1 - (i - nc))
        return jnp.where(d == 0, i, rev)

    def out_block(d, i):
        k = jnp.maximum(i - nc, 0)
        return jnp.where(d == 0, k, nl - 1 - k)

    per_dir = lambda shape: pl.BlockSpec((1,) + shape, lambda d, i: (d, 0, 0))
    return pl.pallas_call(
        functools.partial(_s5_kernel, steps=steps, bp=bp),
        grid=(2, nc + nl),
        in_specs=[pl.BlockSpec((rb, S5_WIDTH), lambda d, i: (in_block(d, i), 0)), per_dir((S5_WIDTH, 2 * S5_NP)),
                  per_dir((1, S5_NP)), per_dir((1, S5_NP)), per_dir((2 * S5_NP, S5_WIDTH))],
        out_specs=pl.BlockSpec((1, rb, S5_WIDTH), lambda d, i: (d, out_block(d, i), 0)),
        out_shape=jax.ShapeDtypeStruct((2, nl * rb, S5_WIDTH), F32),
        scratch_shapes=[pltpu.VMEM((rb, 2 * S5_NP), F32), pltpu.VMEM((bp, 2 * S5_NP), F32)],
        compiler_params=_cparams(("arbitrary", "arbitrary")),
        name="s5_scan",
    )(u_seq, bbd, lre, lim, cbd)


def _s5_mixer_latent(u_lat, u_ctx, b, params):
    n, c = u_lat.shape[0], u_ctx.shape[0]
    bp = -(-b // 8) * 8
    seq = jnp.concatenate([u_ctx, u_lat], axis=0).reshape(c + n, b, S5_WIDTH)
    seq = jnp.pad(seq, ((0, 0), (0, bp - b), (0, 0))).reshape((c + n) * bp, S5_WIDTH)
    y = _s5_scan(seq, c * bp, *params, bp)
    return (y[0] + y[1]).reshape(n, bp, S5_WIDTH)[:, :b].reshape(n, b * S5_WIDTH)


def _out_odd_kernel(x_ref, oa_ref, ys_ref, u_ref, mod_ref, ng_ref, d_ref, gw_ref, gb_ref, woa_ref, wob_ref,
                    rw_ref, rb_ref, x1_ref, h_ref, gate_ref):
    y = ys_ref[...] + d_ref[...] * u_ref[...].astype(F32)
    gl = 0.5 * y * (1.0 + jnp.tanh(np.sqrt(2.0 / np.pi) * (y + 0.044715 * (y * y * y))))
    z = gl * jax.nn.sigmoid(jnp.dot(gl.astype(BF), gw_ref[...], preferred_element_type=F32) + gb_ref[...])
    mix = (jnp.dot(oa_ref[0], woa_ref[...], preferred_element_type=F32)
           + jnp.dot(z.astype(BF), wob_ref[...], preferred_element_type=F32))
    x1 = x_ref[0] + mod_ref[0, 2:3, :] * mix
    x1_ref[0] = x1
    h = _norm_mod(x1, ng_ref[...], mod_ref[0, 3:4, :], mod_ref[0, 4:5, :]).astype(BF)
    h_ref[0] = h
    logits = jnp.dot(h, rw_ref[...], preferred_element_type=F32) + rb_ref[...]
    lane = lax.broadcasted_iota(jnp.int32, logits.shape, 1)
    m1 = jnp.max(logits, axis=-1, keepdims=True)
    i1 = jnp.min(jnp.where(logits == m1, lane, LANES), axis=-1, keepdims=True)
    rest = jnp.where(lane == i1, -jnp.inf, logits)
    m2 = jnp.max(rest, axis=-1, keepdims=True)
    i2 = jnp.min(jnp.where(rest == m2, lane, LANES), axis=-1, keepdims=True)
    e2 = jnp.exp(m2 - m1)
    gate_ref[0] = jnp.where(lane == i1, 1.0 / (1.0 + e2), 0.0) + jnp.where(lane == i2, e2 / (1.0 + e2), 0.0)


def _out_odd(x, oa, ys, u, mods, norm_gain, d_skip, glu_w, glu_b, w_out, router_w, router_b):
    b, n, d = x.shape
    tm = _token_tile(n)
    wa = oa.shape[2]
    tok = lambda width: pl.BlockSpec((1, tm, width), lambda bi, i: (bi, i, 0))
    vec = lambda width: _const_spec((1, width))
    tmajor = pl.BlockSpec((tm, S5_WIDTH), lambda bi, i: (i, bi))
    return pl.pallas_call(
        _out_odd_kernel,
        grid=(b, n // tm),
        in_specs=[tok(d), tok(wa), tmajor, tmajor, _mod_spec(d, None), vec(d), vec(S5_WIDTH),
                  _const_spec(glu_w.shape), vec(S5_WIDTH), _const_spec((wa, d)), _const_spec((S5_WIDTH, d)),
                  _const_spec(router_w.shape), vec(LANES)],
        out_specs=[tok(d), tok(d), tok(LANES)],
        out_shape=[jax.ShapeDtypeStruct((b, n, d), F32), jax.ShapeDtypeStruct((b, n, d), BF),
                   jax.ShapeDtypeStruct((b, n, LANES), F32)],
        compiler_params=_cparams(("parallel", "parallel")),
        name="out_odd",
    )(x, oa, ys, u, mods, norm_gain, d_skip, glu_w, glu_b, w_out[:wa], w_out[wa:], router_w, router_b)


MOE_CHUNK = 512
MOE_ROWS = 256
MOE_F_CHUNKS = 4
MOE_ALIGN = 16


def _moe_capacity(tb):
    need = 2 * tb + N_EXPERTS * MOE_ALIGN + MOE_ROWS
    return -(-need // MOE_CHUNK) * MOE_CHUNK


def _moe_kernel(h_ref, gate_ref, x1_ref, mod_ref, w1_ref, w3_ref, w2_ref, o_ref, xy_ref, tok_ref, tokt_ref,
                cnt_ref, seg_ref):
    e = pl.program_id(1)
    q = pl.program_id(2)
    last_e = pl.num_programs(1) - 1
    last_q = pl.num_programs(2) - 1
    tb, d = h_ref.shape[1:]
    cap = xy_ref.shape[0]
    ch = MOE_CHUNK

    @pl.when((e == 0) & (q == 0))
    def _pack():
        lane1 = lax.broadcasted_iota(jnp.int32, (1, LANES), 1)
        total = jnp.zeros((1, LANES), F32)
        for c in range(tb // ch):
            total = total + jnp.sum((gate_ref[0, c * ch:(c + 1) * ch, :] > 0.0).astype(F32), axis=0, keepdims=True)
        seg_vec = jnp.zeros((1, LANES), F32)
        start = jnp.int32(0)
        for k in range(N_EXPERTS):
            nk = jnp.sum(jnp.where(lane1 == k, total, 0.0)).astype(jnp.int32)
            cnt_ref[k] = nk
            seg_ref[k] = start
            seg_vec = jnp.where(lane1 == k, start.astype(F32), seg_vec)
            start = start + (nk + MOE_ALIGN - 1) // MOE_ALIGN * MOE_ALIGN
        tri = (lax.broadcasted_iota(jnp.int32, (ch, ch), 0)
               > lax.broadcasted_iota(jnp.int32, (ch, ch), 1)).astype(BF)
        lane = lax.broadcasted_iota(jnp.int32, (ch, LANES), 1)
        offset = jnp.zeros((1, LANES), F32)
        for c in range(tb // ch):
            g = gate_ref[0, c * ch:(c + 1) * ch, :]
            sel = g > 0.0
            sel_f = sel.astype(F32)
            rank = jnp.dot(tri, sel_f.astype(BF), preferred_element_type=F32) + offset
            pos = jnp.where(sel, rank + seg_vec, -1.0)
            pmax = jnp.max(pos, axis=-1, keepdims=True)
            pmin = jnp.min(jnp.where(sel, pos, 1e9), axis=-1, keepdims=True)
            gmin = jnp.sum(jnp.where(pos == pmin, g, 0.0), axis=-1, keepdims=True)
            gmax = jnp.sum(jnp.where(pos == pmax, g, 0.0), axis=-1, keepdims=True)
            pmax = jnp.where(pmax == pmin, -1.0, pmax)
            tok_ref[c * ch:(c + 1) * ch, :] = jnp.where(
                lane == 0, pmin, jnp.where(lane == 1, pmax, jnp.where(lane == 2, gmin, jnp.where(lane == 3, gmax, 0.0))))
            offset = offset + jnp.sum(sel_f, axis=0, keepdims=True)
        tokt_ref[...] = tok_ref[...].T
        pmin_t, pmax_t = tokt_ref[0:1, :], tokt_ref[1:2, :]
        row_id = lax.broadcasted_iota(jnp.int32, (ch, tb), 0).astype(F32)

        def pack_chunk(c, carry):
            r0 = pl.multiple_of(c * ch, ch)
            rid = row_id + r0.astype(F32)
            onehot = jnp.where(rid == pmin_t, 1.0, jnp.where(rid == pmax_t, 1.0, 0.0)).astype(BF)
            xy_ref[pl.ds(r0, ch), :] = jnp.dot(onehot, h_ref[0], preferred_element_type=F32).astype(BF)
            return carry

        lax.fori_loop(0, cap // ch, pack_chunk, 0)

    n = cnt_ref[e]
    seg = seg_ref[e]

    def ffn_tile(r0, size):
        src = pl.multiple_of(seg + r0, MOE_ALIGN)
        x = xy_ref[pl.ds(src, size), :]
        f = w1_ref.shape[2]
        split = -(-f // (2 * MXU_TILE)) * MXU_TILE
        subs = ((0, split), (split, f))
        proj = [(jnp.dot(x, w1_ref[0, :, lo:hi], preferred_element_type=F32),
                 jnp.dot(x, w3_ref[0, :, lo:hi], preferred_element_type=F32)) for lo, hi in subs]
        y = None
        for (lo, hi), (a, g) in zip(subs, proj):
            part = jnp.dot((_silu(a) * g).astype(BF), w2_ref[0, lo:hi, :], preferred_element_type=F32)
            y = part if y is None else y + part

        @pl.when(q == 0)
        def _():
            o_ref[0, pl.ds(r0, size), :] = y

        @pl.when((q > 0) & (q < last_q))
        def _():
            o_ref[0, pl.ds(r0, size), :] += y

        @pl.when(q == last_q)
        def _():
            rid = lax.broadcasted_iota(jnp.int32, (size, tb), 0).astype(F32) + src.astype(F32)
            gate_rows = jnp.sum(jnp.where(rid == tokt_ref[0:1, :], tokt_ref[2:3, :], 0.0)
                                + jnp.where(rid == tokt_ref[1:2, :], tokt_ref[3:4, :], 0.0), axis=-1, keepdims=True)
            mine = lax.broadcasted_iota(jnp.int32, (size, 1), 0) < n - r0
            res = gate_rows * (o_ref[0, pl.ds(r0, size), :] + y)
            xy_ref[pl.ds(src, size), :] = jnp.where(mine, res.astype(BF), x)

    rows = MOE_ROWS
    n_full = n // rows
    tail = n - n_full * rows

    def full_tile(i, carry):
        ffn_tile(pl.multiple_of(i * rows, rows), rows)
        return carry

    lax.fori_loop(0, n_full, full_tile, 0)
    tail_r0 = pl.multiple_of(n_full * rows, rows)

    @pl.when(tail > rows // 2)
    def _():
        ffn_tile(tail_r0, rows)

    @pl.when((tail > 0) & (tail <= rows // 2))
    def _():
        ffn_tile(tail_r0, rows // 2)

    @pl.when((e == last_e) & (q == last_q))
    def _combine():
        used = min(cap, -(-(2 * tb + N_EXPERTS * MOE_ALIGN) // MXU_TILE) * MXU_TILE)
        col_id = lax.broadcasted_iota(jnp.int32, (ch, used), 1).astype(F32)

        def combine_chunk(c, carry):
            tok_rows = pl.ds(pl.multiple_of(c * ch, ch), ch)
            pmin, pmax = tok_ref[tok_rows, 0:1], tok_ref[tok_rows, 1:2]
            onehot = jnp.where(col_id == pmin, 1.0, jnp.where(col_id == pmax, 1.0, 0.0)).astype(BF)
            mix = jnp.dot(onehot, xy_ref[0:used, :], preferred_element_type=F32)
            o_ref[0, tok_rows, :] = x1_ref[0, tok_rows, :] + mod_ref[0, 5:6, :] * mix
            return carry

        lax.fori_loop(0, tb // ch, combine_chunk, 0)


def _moe(h, gates, x1, mods, w1, w3, w2):
    b, n, d = h.shape
    n_exp, _, f = w1.shape
    assert n_exp == N_EXPERTS and n % MOE_CHUNK == 0 and MOE_F_CHUNKS > 1
    fc = f // MOE_F_CHUNKS
    per_batch = lambda width: pl.BlockSpec((1, n, width), lambda bi, e, q: (bi, 0, 0), pipeline_mode=pl.Buffered(1))
    return pl.pallas_call(
        _moe_kernel,
        grid=(b, n_exp, MOE_F_CHUNKS),
        in_specs=[per_batch(d), per_batch(LANES), per_batch(d),
                  pl.BlockSpec((1, N_MOD, d), lambda bi, e, q: (bi, 0, 0)),
                  pl.BlockSpec((1, d, fc), lambda bi, e, q: (e, 0, q)),
                  pl.BlockSpec((1, d, fc), lambda bi, e, q: (e, 0, q)),
                  pl.BlockSpec((1, fc, d), lambda bi, e, q: (e, q, 0))],
        out_specs=per_batch(d),
        out_shape=jax.ShapeDtypeStruct((b, n, d), F32),
        scratch_shapes=[pltpu.VMEM((_moe_capacity(n), d), BF), pltpu.VMEM((n, LANES), F32),
                        pltpu.VMEM((LANES, n), F32), pltpu.SMEM((N_EXPERTS,), jnp.int32),
                        pltpu.SMEM((N_EXPERTS,), jnp.int32)],
        compiler_params=_cparams(("arbitrary", "arbitrary", "arbitrary")),
        name="moe",
    )(h, gates, x1, mods, w1, w3, w2)


def _head_blockdiag(width):
    head = np.arange(width) // HEAD_DIM
    return jnp.asarray((head[:, None] == head[None, :]).astype(np.float32) / HEAD_DIM, dtype=BF)


def _tile_gain(gain, width, scale=1.0):
    return (jnp.tile(gain.astype(F32), width // HEAD_DIM) * scale).reshape(1, width)


def _even_layer(x, xc, mods, ctx_row, norm_mix, norm_ffn, w_in, q_gain, k_gain, rpb, w_out, w1, w3, w2):
    d = x.shape[2]
    bd = _head_blockdiag(NA_WIDTH)
    w_in = w_in.astype(BF)
    qg = _tile_gain(q_gain, NA_WIDTH, Q_SCALE)
    kg = _tile_gain(k_gain, NA_WIDTH)
    ng_mix, ng_ffn = norm_mix.reshape(1, d), norm_ffn.reshape(1, d)
    u, q, k, v = _in_even(x, mods, None, ng_mix, w_in, qg, kg, bd)
    uc, qc, kc, vc = _in_even(xc, mods, ctx_row, ng_mix, w_in, qg, kg, bd)
    o_na = _neighbourhood_attention(q, k, v, kc, vc, rpb)
    o_ctx = _context_attention(qc, kc, vc)
    w_out, w1, w3, w2 = (t.astype(BF) for t in (w_out, w1, w3, w2))
    x = _out_ffn_even(x, _fourier(u), o_na, mods, None, ng_ffn, w_out, w1, w3, w2)
    xc = _out_ffn_even(xc, _fourier(uc), o_ctx, mods, ctx_row, ng_ffn, w_out, w1, w3, w2)
    return x, xc


def _odd_layer(x, xc, mods, ctx_row, norm_mix, norm_ffn, w_in, q_gain, k_gain, s5, d_skip, glu_w, glu_b, w_out,
               router_w, router_b, ew1, ew3, ew2):
    b, n, d = x.shape
    kvw = GKV_HEADS * HEAD_DIM
    wq, wk, wv, wu = jnp.split(w_in, [GQ_WIDTH, GQ_WIDTH + kvw, GQ_WIDTH + 2 * kvw], axis=1)

    def twice(w):
        return jnp.repeat(w.reshape(d, GKV_HEADS, 1, HEAD_DIM), 2, axis=2).reshape(d, GKV_PAD)

    w_lat = jnp.concatenate([wq, twice(wk), twice(wv), wu], axis=1).astype(BF)
    w_ctx = jnp.concatenate([twice(wk), twice(wv), wu], axis=1).astype(BF)
    qg = _tile_gain(q_gain, GQ_WIDTH, Q_SCALE)
    kg = _tile_gain(k_gain, GKV_PAD)
    bdq, bdk = _head_blockdiag(GQ_WIDTH), _head_blockdiag(GKV_PAD)
    cos, sin_signed = _rope_tables(n)
    ng_mix, ng_ffn = norm_mix.reshape(1, d), norm_ffn.reshape(1, d)
    q, kk, vv, u = _in_odd(x, mods, None, ng_mix, w_lat, qg, kg, bdq, bdk, cos, sin_signed, True)
    kkc, vvc, uc = _in_odd(xc, mods, ctx_row, ng_mix, w_ctx, None, kg, None, bdk, None, None, False)
    vvt = jnp.concatenate([vvc, vv], axis=1).transpose(0, 2, 1)
    dup_rows = (jnp.arange(GKV_PAD) % LANES >= HEAD_DIM)[None, :, None]
    vvt = jnp.where(dup_rows, jnp.ones_like(vvt), vvt)
    o_attn = _gqa(q, jnp.concatenate([kkc, kk], axis=1), vvt)
    y_ssm = _s5_mixer_latent(u, uc, b, _s5_params(*s5))
    rw = jnp.pad(router_w, ((0, 0), (0, LANES - N_EXPERTS))).astype(BF)
    rb = jnp.pad(router_b.astype(F32), (0, LANES - N_EXPERTS), constant_values=NEG_INF).reshape(1, LANES)
    x1, h, gates = _out_odd(x, o_attn, y_ssm, u, mods, ng_ffn, d_skip.reshape(1, S5_WIDTH).astype(F32),
                            glu_w.astype(BF), glu_b.reshape(1, S5_WIDTH).astype(F32), w_out.astype(BF), rw, rb)
    return _moe(h, gates, x1, mods, ew1.astype(BF), ew3.astype(BF), ew2.astype(BF))


def kernel(x, c, ctx, c_ctx, ev_mod_w, ev_mod_b, ev_norm_mix, ev_norm_ffn, ev_w_in, ev_q_gain, ev_k_gain, ev_rpb,
           ev_w_out, ev_ffn_w1, ev_ffn_w3, ev_ffn_w2, od_mod_w, od_mod_b, od_norm_mix, od_norm_ffn, od_w_in,
           od_q_gain, od_k_gain, od_s5_a_re, od_s5_a_im, od_s5_log_step, od_s5_b_re, od_s5_b_im, od_s5_c_re,
           od_s5_c_im, od_s5_d, od_s5_glu_w, od_s5_glu_b, od_w_out, od_router_w, od_router_b, od_exp_w1, od_exp_w3,
           od_exp_w2):
    assert ev_mod_w.shape[0] == 1 and od_mod_w.shape[0] == 1, "one even and one odd layer"
    b, n, d = x.shape
    rows = -(-(b + 1) // 8) * 8
    cond = jnp.zeros((rows, d), F32).at[:b].set(c).at[b].set(c_ctx)
    mods_even = _modulation(cond, ev_mod_w[0], ev_mod_b[0])
    mods_odd = _modulation(cond, od_mod_w[0], od_mod_b[0])
    x, xc = _even_layer(x, ctx, mods_even, b, ev_norm_mix[0], ev_norm_ffn[0], ev_w_in[0], ev_q_gain[0], ev_k_gain[0],
                        ev_rpb[0], ev_w_out[0], ev_ffn_w1[0], ev_ffn_w3[0], ev_ffn_w2[0])
    s5 = (od_s5_a_re[0], od_s5_a_im[0], od_s5_log_step[0], od_s5_b_re[0], od_s5_b_im[0], od_s5_c_re[0], od_s5_c_im[0])
    return _odd_layer(x, xc, mods_odd, b, od_norm_mix[0], od_norm_ffn[0], od_w_in[0], od_q_gain[0], od_k_gain[0],
                      s5, od_s5_d[0], od_s5_glu_w[0], od_s5_glu_b[0], od_w_out[0], od_router_w[0], od_router_b[0],
                      od_exp_w1[0], od_exp_w3[0], od_exp_w2[0])
```

```python
import functools

import numpy as np
import jax
import jax.numpy as jnp
from jax import lax
from jax.experimental import pallas as pl
from jax.experimental.pallas import tpu as pltpu

BF = jnp.bfloat16
F32 = jnp.float32

EPS = 1e-6
NEG_INF = -1e30
GRID_W = 64
LANES = 128
MXU_TILE = 256
HEAD_DIM = 64
LOG2E = float(np.log2(np.e))
Q_SCALE = HEAD_DIM ** -0.5 * LOG2E
FN_GROUPS, FN_GROUP_DIM = 4, 128
FN_WIDTH = FN_GROUPS * FN_GROUP_DIM
NA_HEADS = 8
NA_WIDTH = NA_HEADS * HEAD_DIM
NA_KH, NA_KW = 8, 16
NA_QROWS = 4
NA_WIN = NA_QROWS + NA_KH - 1
GQ_HEADS, GKV_HEADS = 12, 3
GQ_WIDTH = GQ_HEADS * HEAD_DIM
GKV_PAD = GKV_HEADS * LANES
S5_GROUPS, S5_GROUP_DIM, S5_STATE = 16, 16, 64
S5_WIDTH = S5_GROUPS * S5_GROUP_DIM
S5_NP = S5_GROUPS * S5_STATE
N_EXPERTS = 8
N_MOD = 6
ROPE_THETA = 10000.0
VMEM_LIMIT = 56 * 1024 * 1024

NT_DIMS = (((1,), (1,)), ((), ()))


def _cparams(sem):
    return pltpu.CompilerParams(dimension_semantics=sem, vmem_limit_bytes=VMEM_LIMIT)


def _const_spec(shape):
    nd = len(shape)
    return pl.BlockSpec(shape, lambda *_: (0,) * nd, pipeline_mode=pl.Buffered(1))


def _token_tile(n):
    return 512 if n % 512 == 0 else 256


def _mod_spec(d, row):
    if row is None:
        return pl.BlockSpec((1, N_MOD, d), lambda b, i: (b, 0, 0))
    return pl.BlockSpec((1, N_MOD, d), lambda b, i: (row, 0, 0))


def _norm_mod(x, gain, shift, scale):
    ms = jnp.mean(x * x, axis=-1, keepdims=True)
    return (x * lax.rsqrt(ms + EPS) * gain) * (1.0 + scale) + shift


def _head_norm(t, blockdiag, gain):
    ms = jnp.dot((t * t).astype(BF), blockdiag, preferred_element_type=F32)
    return t * lax.rsqrt(ms + EPS) * gain


def _silu(a):
    return a * jax.nn.sigmoid(a)


def _mod_kernel(c_ref, w_ref, b_ref, o_ref):
    a = _silu(c_ref[...]).astype(BF)
    o_ref[...] = jnp.dot(a, w_ref[...].astype(BF), preferred_element_type=F32) + b_ref[...]


def _modulation(cond, w, b):
    r, d = cond.shape
    n = w.shape[1]
    tn = n // 4
    out = pl.pallas_call(
        _mod_kernel,
        grid=(n // tn,),
        in_specs=[pl.BlockSpec((r, d), lambda j: (0, 0)),
                  pl.BlockSpec((d, tn), lambda j: (0, j)),
                  pl.BlockSpec((1, tn), lambda j: (0, j))],
        out_specs=pl.BlockSpec((r, tn), lambda j: (0, j)),
        out_shape=jax.ShapeDtypeStruct((r, n), F32),
        compiler_params=_cparams(("arbitrary",)),
        name="modulation",
    )(cond, w, b.reshape(1, n))
    return out.reshape(r, N_MOD, d)


def _in_even_kernel(x_ref, mod_ref, ng_ref, w_ref, qg_ref, kg_ref, bd_ref, u_ref, q_ref, k_ref, v_ref):
    h = _norm_mod(x_ref[0], ng_ref[...], mod_ref[0, 0:1, :], mod_ref[0, 1:2, :])
    p = jnp.dot(h.astype(BF), w_ref[...], preferred_element_type=F32)
    bd = bd_ref[...]
    w = FN_WIDTH
    u_ref[0] = p[:, 0:w].astype(BF)
    q_ref[0] = _head_norm(p[:, w:2 * w], bd, qg_ref[...]).astype(BF)
    k_ref[0] = _head_norm(p[:, 2 * w:3 * w], bd, kg_ref[...]).astype(BF)
    v_ref[0] = p[:, 3 * w:4 * w].astype(BF)


def _in_even(x, mods, mod_row, norm_gain, w_in, q_gain, k_gain, blockdiag):
    b, n, d = x.shape
    tm = _token_tile(n)
    wout = FN_WIDTH
    tok = lambda width: pl.BlockSpec((1, tm, width), lambda bi, i: (bi, i, 0))
    return pl.pallas_call(
        _in_even_kernel,
        grid=(b, n // tm),
        in_specs=[tok(d), _mod_spec(d, mod_row), _const_spec((1, d)), _const_spec(w_in.shape),
                  _const_spec((1, wout)), _const_spec((1, wout)), _const_spec(blockdiag.shape)],
        out_specs=[tok(wout)] * 4,
        out_shape=[jax.ShapeDtypeStruct((b, n, wout), BF)] * 4,
        compiler_params=_cparams(("parallel", "parallel")),
        name="in_even",
    )(x, mods, norm_gain, w_in, q_gain, k_gain, blockdiag)


def _fourier_kernel(u_ref, cm_ref, ml_ref, o_ref, z_ref):
    n = u_ref.shape[1]
    gd = FN_GROUP_DIM
    for g in range(FN_GROUPS):
        z = jnp.dot(u_ref[0, :, g * gd:(g + 1) * gd], cm_ref[...], preferred_element_type=F32)
        z_ref[0:n, g * gd:(g + 1) * gd] = z[:, :gd].astype(BF)
        z_ref[n:2 * n, g * gd:(g + 1) * gd] = z[:, gd:].astype(BF)
    o_ref[0] = jnp.dot(ml_ref[...], z_ref[...], preferred_element_type=F32).astype(BF)


def _dft_tables(n):
    def cos_sin(size):
        j = jnp.arange(size, dtype=jnp.int32)
        ang = ((j[:, None] * j[None, :]) % size).astype(F32) * (2.0 * np.pi / size)
        return jnp.cos(ang), jnp.sin(ang)
    cc, sc = cos_sin(FN_GROUP_DIM)
    cl, sl = cos_sin(n)
    scale = 1.0 / np.sqrt(n * FN_GROUP_DIM)
    return (jnp.concatenate([cc, sc], axis=1).astype(BF),
            (jnp.concatenate([cl, -sl], axis=1) * scale).astype(BF))


def _fourier(u):
    b, n, w = u.shape
    cm, ml = _dft_tables(n)
    return pl.pallas_call(
        _fourier_kernel,
        grid=(b,),
        in_specs=[pl.BlockSpec((1, n, w), lambda bi: (bi, 0, 0)), _const_spec(cm.shape), _const_spec(ml.shape)],
        out_specs=pl.BlockSpec((1, n, w), lambda bi: (bi, 0, 0)),
        out_shape=jax.ShapeDtypeStruct((b, n, w), BF),
        scratch_shapes=[pltpu.VMEM((2 * n, w), BF)],
        compiler_params=_cparams(("parallel",)),
        name="fourier",
    )(u, cm, ml)


def _pair_stack(qp, first_half):
    zero = jnp.zeros_like(qp)
    return jnp.concatenate([jnp.where(first_half, qp, zero), jnp.where(first_half, zero, qp)], axis=0)


def _pair_merge(o, nq, first_half):
    return jnp.where(first_half, o[:nq], o[nq:])


def _na_kernel(q_ref, k_ref, v_ref, kc_ref, vc_ref, bias_ref, o_ref, *, rows):
    rb = pl.program_id(1)
    ws = jnp.minimum(jnp.maximum(NA_QROWS * rb - NA_KH // 2, 0), rows - NA_WIN)
    start = pl.multiple_of(ws * GRID_W, GRID_W)
    nq = NA_QROWS * GRID_W
    nk = NA_WIN * GRID_W
    first_half = lax.broadcasted_iota(jnp.int32, (nq, LANES), 1) < HEAD_DIM
    n_pairs = NA_HEADS // 2

    def scores(p):
        sl = slice(p * LANES, (p + 1) * LANES)
        qs = _pair_stack(q_ref[0, :, sl], first_half)
        bias = jnp.concatenate([bias_ref[0, 2 * p], bias_ref[0, 2 * p + 1]], axis=0)
        return (lax.dot_general(qs, k_ref[0, pl.ds(start, nk), sl], NT_DIMS, preferred_element_type=F32) + bias,
                lax.dot_general(qs, kc_ref[0, :, sl], NT_DIMS, preferred_element_type=F32))

    nxt = scores(0)
    for p in range(n_pairs):
        sl = slice(p * LANES, (p + 1) * LANES)
        s_loc, s_ctx = nxt
        if p + 1 < n_pairs:
            nxt = scores(p + 1)
        vw = v_ref[0, pl.ds(start, nk), sl]
        m = jnp.maximum(jnp.max(s_loc, axis=-1, keepdims=True), jnp.max(s_ctx, axis=-1, keepdims=True))
        p_loc = jnp.exp2(s_loc - m)
        p_ctx = jnp.exp2(s_ctx - m)
        denom = jnp.sum(p_loc, axis=-1, keepdims=True) + jnp.sum(p_ctx, axis=-1, keepdims=True)
        o = (jnp.dot(p_loc.astype(BF), vw, preferred_element_type=F32)
             + jnp.dot(p_ctx.astype(BF), vc_ref[0, :, sl], preferred_element_type=F32)) / denom
        o_ref[0, :, sl] = _pair_merge(o, nq, first_half).astype(BF)


def _na_block_class(rb, n_blocks):
    return int(rb > 0) + int(rb == n_blocks - 1)


def _na_bias_tables(rpb, rows):
    n_blocks = rows // NA_QROWS
    col = np.arange(GRID_W)
    col_start = np.clip(col - NA_KW // 2, 0, GRID_W - NA_KW)
    col_in = (col[None, :] >= col_start[:, None]) & (col[None, :] < col_start[:, None] + NA_KW)
    dc_idx = np.clip(col[None, :] - col[:, None] + NA_KW - 1, 0, 2 * NA_KW - 2)
    per_class = {}
    for rb in range(n_blocks):
        ws = min(max(NA_QROWS * rb - NA_KH // 2, 0), rows - NA_WIN)
        r = NA_QROWS * rb + np.arange(NA_QROWS)
        kr = ws + np.arange(NA_WIN)
        r0 = np.clip(r - NA_KH // 2, 0, rows - NA_KH)
        row_in = (kr[None, :] >= r0[:, None]) & (kr[None, :] < r0[:, None] + NA_KH)
        dr_idx = np.clip(kr[None, :] - r[:, None] + NA_KH - 1, 0, 2 * NA_KH - 2)
        valid = row_in[:, None, :, None] & col_in[None, :, None, :]
        entry = (valid, dr_idx)
        cls = _na_block_class(rb, n_blocks)
        if cls in per_class:
            assert all(np.array_equal(a, b) for a, b in zip(per_class[cls], entry)), "row-block classes differ"
        per_class[cls] = entry
    nq, nk = NA_QROWS * GRID_W, NA_WIN * GRID_W
    onehot = (np.arange(2 * NA_KW - 1)[:, None] == dc_idx.reshape(1, -1)).astype(np.float32)
    by_col = jnp.dot(rpb.astype(F32).reshape(-1, 2 * NA_KW - 1), onehot, precision=lax.Precision.HIGHEST)
    by_col = by_col.reshape(NA_HEADS, 2 * NA_KH - 1, GRID_W, GRID_W)
    tables = []
    for cls in range(3):
        valid, dr_idx = per_class[cls]
        blocks = jnp.stack([by_col[:, int(dr)] for dr in dr_idx.reshape(-1)], axis=1)
        t = blocks.reshape(NA_HEADS, NA_QROWS, NA_WIN, GRID_W, GRID_W).transpose(0, 1, 3, 2, 4)
        tables.append(jnp.where(valid.reshape(nq, nk)[None], t.reshape(NA_HEADS, nq, nk) * LOG2E, NEG_INF))
    return jnp.stack(tables)


def _neighbourhood_attention(q, k, v, kc, vc, rpb):
    b, n, w = q.shape
    c = kc.shape[1]
    rows = n // GRID_W
    assert rows % NA_QROWS == 0 and rows >= NA_WIN + 1
    n_blocks = rows // NA_QROWS
    bias = _na_bias_tables(rpb, rows)
    nq, nk = NA_QROWS * GRID_W, NA_WIN * GRID_W
    full = lambda length: pl.BlockSpec((1, length, w), lambda bi, rb: (bi, 0, 0))
    bias_spec = pl.BlockSpec(
        (1, NA_HEADS, nq, nk),
        lambda bi, rb: (jnp.minimum(rb, 1) + (rb == n_blocks - 1).astype(jnp.int32), 0, 0, 0))
    return pl.pallas_call(
        functools.partial(_na_kernel, rows=rows),
        grid=(b, n_blocks),
        in_specs=[pl.BlockSpec((1, nq, w), lambda bi, rb: (bi, rb, 0)), full(n), full(n), full(c), full(c), bias_spec],
        out_specs=pl.BlockSpec((1, nq, w), lambda bi, rb: (bi, rb, 0)),
        out_shape=jax.ShapeDtypeStruct((b, n, w), BF),
        compiler_params=_cparams(("parallel", "arbitrary")),
        name="neighbourhood_attention",
    )(q, k, v, kc, vc, bias)


def _ctx_attn_kernel(q_ref, k_ref, v_ref, o_ref):
    nq = q_ref.shape[1]
    first_half = lax.broadcasted_iota(jnp.int32, (nq, LANES), 1) < HEAD_DIM
    for p in range(NA_HEADS // 2):
        sl = slice(p * LANES, (p + 1) * LANES)
        qs = _pair_stack(q_ref[0, :, sl], first_half)
        s = lax.dot_general(qs, k_ref[0, :, sl], NT_DIMS, preferred_element_type=F32)
        e = jnp.exp2(s - jnp.max(s, axis=-1, keepdims=True))
        o = jnp.dot(e.astype(BF), v_ref[0, :, sl], preferred_element_type=F32) / jnp.sum(e, axis=-1, keepdims=True)
        o_ref[0, :, sl] = _pair_merge(o, nq, first_half).astype(BF)


def _context_attention(q, k, v):
    b, c, w = q.shape
    spec = pl.BlockSpec((1, c, w), lambda bi: (bi, 0, 0))
    return pl.pallas_call(
        _ctx_attn_kernel,
        grid=(b,),
        in_specs=[spec] * 3,
        out_specs=spec,
        out_shape=jax.ShapeDtypeStruct((b, c, w), BF),
        compiler_params=_cparams(("parallel",)),
        name="context_attention",
    )(q, k, v)


def _out_ffn_even_kernel(x_ref, oa_ref, ob_ref, mod_ref, ng_ref, woa_ref, wob_ref, w1_ref, w3_ref, w2_ref, o_ref,
                         *, f_chunks):
    mix = (jnp.dot(oa_ref[0], woa_ref[...], preferred_element_type=F32)
           + jnp.dot(ob_ref[0], wob_ref[...], preferred_element_type=F32))
    x1 = x_ref[0] + mod_ref[0, 2:3, :] * mix
    h = _norm_mod(x1, ng_ref[...], mod_ref[0, 3:4, :], mod_ref[0, 4:5, :]).astype(BF)
    f = w1_ref.shape[1]
    fc = -(-f // (f_chunks * MXU_TILE)) * MXU_TILE
    y = None
    for lo in range(0, f, fc):
        hi = min(lo + fc, f)
        a = jnp.dot(h, w1_ref[:, lo:hi], preferred_element_type=F32)
        g = jnp.dot(h, w3_ref[:, lo:hi], preferred_element_type=F32)
        part = jnp.dot((_silu(a) * g).astype(BF), w2_ref[lo:hi, :], preferred_element_type=F32)
        y = part if y is None else y + part
    o_ref[0] = x1 + mod_ref[0, 5:6, :] * y


def _out_ffn_even(x, oa, ob, mods, mod_row, norm_gain, w_out, w1, w3, w2):
    b, n, d = x.shape
    tm = _token_tile(n)
    wa = oa.shape[2]
    tok = lambda width: pl.BlockSpec((1, tm, width), lambda bi, i: (bi, i, 0))
    return pl.pallas_call(
        functools.partial(_out_ffn_even_kernel, f_chunks=2),
        grid=(b, n // tm),
        in_specs=[tok(d), tok(wa), tok(ob.shape[2]), _mod_spec(d, mod_row), _const_spec((1, d)),
                  _const_spec((wa, d)), _const_spec((w_out.shape[0] - wa, d)),
                  _const_spec(w1.shape), _const_spec(w3.shape), _const_spec(w2.shape)],
        out_specs=tok(d),
        out_shape=jax.ShapeDtypeStruct((b, n, d), F32),
        compiler_params=_cparams(("parallel", "parallel")),
        name="out_ffn_even",
    )(x, oa, ob, mods, norm_gain, w_out[:wa], w_out[wa:], w1, w3, w2)


def _rope(t, cos, sin_signed):
    width = t.shape[1]
    lane = lax.broadcasted_iota(jnp.int32, t.shape, 1)
    partner = jnp.where(lane % 2 == 0, pltpu.roll(t, width - 1, 1), pltpu.roll(t, 1, 1))
    reps = width // LANES
    return t * jnp.concatenate([cos] * reps, axis=1) + partner * jnp.concatenate([sin_signed] * reps, axis=1)


def _in_odd_kernel(*refs, is_latent):
    if is_latent:
        (x_ref, mod_ref, ng_ref, w_ref, qg_ref, kg_ref, bdq_ref, bdk_ref, cos_ref, sin_ref,
         q_ref, kk_ref, vv_ref, u_ref) = refs
    else:
        x_ref, mod_ref, ng_ref, w_ref, kg_ref, bdk_ref, kk_ref, vv_ref, u_ref = refs
    c0 = GQ_WIDTH if is_latent else 0
    tm = x_ref.shape[1]
    sub = min(tm, 256)
    for r in range(0, tm, sub):
        rs = slice(r, r + sub)
        h = _norm_mod(x_ref[0, rs, :], ng_ref[...], mod_ref[0, 0:1, :], mod_ref[0, 1:2, :])
        p = jnp.dot(h.astype(BF), w_ref[...], preferred_element_type=F32)
        kk = _head_norm(p[:, c0:c0 + GKV_PAD], bdk_ref[...], kg_ref[...])
        if is_latent:
            q = _head_norm(p[:, 0:GQ_WIDTH], bdq_ref[...], qg_ref[...])
            q_ref[0, rs, :] = _rope(q, cos_ref[rs, :], sin_ref[rs, :]).astype(BF)
            kk = _rope(kk, cos_ref[rs, :], sin_ref[rs, :])
        kk_ref[0, rs, :] = kk.astype(BF)
        vv_ref[0, rs, :] = p[:, c0 + GKV_PAD:c0 + 2 * GKV_PAD].astype(BF)
        u_ref[rs, :] = p[:, c0 + 2 * GKV_PAD:].astype(BF)


def _in_odd(x, mods, mod_row, norm_gain, w_ext, q_gain, k_gain, bdq, bdk, cos, sin_signed, is_latent):
    b, n, d = x.shape
    tm = _token_tile(n)
    tok = lambda width: pl.BlockSpec((1, tm, width), lambda bi, i: (bi, i, 0))
    vec = lambda width: _const_spec((1, width))
    if is_latent:
        rope_spec = pl.BlockSpec((tm, LANES), lambda bi, i: (i, 0))
        in_specs = [tok(d), _mod_spec(d, mod_row), vec(d), _const_spec(w_ext.shape), vec(GQ_WIDTH), vec(GKV_PAD),
                    _const_spec(bdq.shape), _const_spec(bdk.shape), rope_spec, rope_spec]
        args = (x, mods, norm_gain, w_ext, q_gain, k_gain, bdq, bdk, cos, sin_signed)
        widths = [GQ_WIDTH, GKV_PAD, GKV_PAD]
    else:
        in_specs = [tok(d), _mod_spec(d, mod_row), vec(d), _const_spec(w_ext.shape), vec(GKV_PAD),
                    _const_spec(bdk.shape)]
        args = (x, mods, norm_gain, w_ext, k_gain, bdk)
        widths = [GKV_PAD, GKV_PAD]
    u_spec = pl.BlockSpec((tm, S5_WIDTH), lambda bi, i: (i, bi))
    return pl.pallas_call(
        functools.partial(_in_odd_kernel, is_latent=is_latent),
        grid=(b, n // tm),
        in_specs=in_specs,
        out_specs=[tok(wd) for wd in widths] + [u_spec],
        out_shape=[jax.ShapeDtypeStruct((b, n, wd), BF) for wd in widths]
        + [jax.ShapeDtypeStruct((n, b * S5_WIDTH), BF)],
        compiler_params=_cparams(("parallel", "parallel")),
        name="in_odd_latent" if is_latent else "in_odd_context",
    )(*args)


def _rope_tables(n):
    t = jnp.arange(n)
    row = (t // GRID_W).astype(F32)
    col = (t % GRID_W).astype(F32)
    n_axis = HEAD_DIM // 4
    freqs = ROPE_THETA ** (-jnp.arange(n_axis, dtype=F32) / n_axis)
    ang = jnp.concatenate([row[:, None] * freqs, col[:, None] * freqs], axis=-1)
    ang = jnp.repeat(ang, 2, axis=-1)
    ang = jnp.concatenate([ang, ang], axis=-1)
    sign = jnp.where(jnp.arange(LANES) % 2 == 0, -1.0, 1.0).astype(F32)
    return jnp.cos(ang), jnp.sin(ang) * sign


def _gqa_kernel(q_ref, kk_ref, vvt_ref, o_ref):
    tq = q_ref.shape[1]
    first_half = lax.broadcasted_iota(jnp.int32, (tq, LANES), 1) < HEAD_DIM
    group = GQ_HEADS // GKV_HEADS

    def scores_t(j):
        kj = kk_ref[0, :, j * LANES:(j + 1) * LANES]
        base = j * group * HEAD_DIM
        qs = jnp.concatenate(
            [_pair_stack(q_ref[0, :, base + i * LANES:base + (i + 1) * LANES], first_half)
             for i in range(group // 2)], axis=0)
        return lax.dot_general(kj, qs, NT_DIMS, preferred_element_type=F32)

    def weights_t(st):
        return jnp.exp2(st - jnp.max(st, axis=0, keepdims=True)).astype(BF)

    def emit(j, e):
        vtj = vvt_ref[0, j * LANES:(j + 1) * LANES, :]
        base = j * group * HEAD_DIM
        ot = jnp.dot(vtj, e, preferred_element_type=F32)
        on = ot[:HEAD_DIM] * (1.0 / ot[HEAD_DIM:HEAD_DIM + 1])
        for i in range(group // 2):
            pair = jnp.concatenate([on[:, 2 * i * tq:(2 * i + 1) * tq], on[:, (2 * i + 1) * tq:(2 * i + 2) * tq]],
                                   axis=0)
            o_ref[0, :, base + i * LANES:base + (i + 1) * LANES] = pair.T.astype(BF)

    st = {j: scores_t(j) for j in range(min(2, GKV_HEADS))}
    e = {}
    for j in range(GKV_HEADS):
        e[j] = weights_t(st.pop(j))
        if j + 2 < GKV_HEADS:
            st[j + 2] = scores_t(j + 2)
        if j >= 1:
            emit(j - 1, e.pop(j - 1))
    emit(GKV_HEADS - 1, e.pop(GKV_HEADS - 1))


def _gqa(q, kk, vvt):
    b, n, w = q.shape
    nk = kk.shape[1]
    tq = 2 * LANES
    return pl.pallas_call(
        _gqa_kernel,
        grid=(b, n // tq),
        in_specs=[pl.BlockSpec((1, tq, w), lambda bi, i: (bi, i, 0)),
                  pl.BlockSpec((1, nk, GKV_PAD), lambda bi, i: (bi, 0, 0)),
                  pl.BlockSpec((1, GKV_PAD, nk), lambda bi, i: (bi, 0, 0))],
        out_specs=pl.BlockSpec((1, tq, w), lambda bi, i: (bi, i, 0)),
        out_shape=jax.ShapeDtypeStruct((b, n, w), BF),
        compiler_params=_cparams(("parallel", "arbitrary")),
        name="gqa",
    )(q, kk, vvt)


def _s5_kernel(u_ref, bbd_ref, lre_ref, lim_ref, cbd_ref, y_ref, bu_ref, st_ref, *, steps, bp):
    backward = pl.program_id(0) == 1

    @pl.when(pl.program_id(1) == 0)
    def _():
        st_ref[...] = jnp.zeros_like(st_ref)

    bu_ref[...] = jnp.dot(u_ref[...], bbd_ref[0], preferred_element_type=F32)
    lc = 2 * LANES
    for j in range(S5_NP // lc):
        re_sl = slice(j * lc, (j + 1) * lc)
        im_sl = slice(S5_NP + j * lc, S5_NP + (j + 1) * lc)
        lre = jnp.broadcast_to(lre_ref[0, :, re_sl], (bp, lc))
        lim = jnp.broadcast_to(lim_ref[0, :, re_sl], (bp, lc))

        def step(s, carry):
            xre, xim = carry
            t = jnp.where(backward, steps - 1 - s, s)
            r0 = pl.multiple_of(t * bp, bp)
            nre = lre * xre - lim * xim + bu_ref[pl.ds(r0, bp), re_sl]
            nim = lre * xim + lim * xre + bu_ref[pl.ds(r0, bp), im_sl]
            bu_ref[pl.ds(r0, bp), re_sl] = nre
            bu_ref[pl.ds(r0, bp), im_sl] = nim
            return nre, nim

        xre, xim = lax.fori_loop(0, steps, step, (st_ref[:, re_sl], st_ref[:, im_sl]), unroll=2)
        st_ref[:, re_sl] = xre
        st_ref[:, im_sl] = xim
    y_ref[0] = jnp.dot(bu_ref[...].astype(BF), cbd_ref[0], preferred_element_type=F32)


def _s5_params(a_re, a_im, log_step, b_re, b_im, c_re, c_im):
    a_re, a_im = a_re.astype(F32), a_im.astype(F32)
    step = jnp.exp(log_step.astype(F32))[..., None]
    mag = jnp.exp(a_re * step)
    lre, lim = mag * jnp.cos(a_im * step), mag * jnp.sin(a_im * step)
    den = a_re * a_re + a_im * a_im
    kre = ((lre - 1.0) * a_re + lim * a_im) / den
    kim = (lim * a_re - (lre - 1.0) * a_im) / den
    bre = kre[..., None] * b_re - kim[..., None] * b_im
    bim = kre[..., None] * b_im + kim[..., None] * b_re
    eye = jnp.eye(S5_GROUPS, dtype=F32)

    def in_blockdiag(m):
        return jnp.einsum('dgpn,gh->dgnhp', m, eye).reshape(2, S5_WIDTH, S5_NP)

    def out_blockdiag(m):
        return jnp.einsum('dgnp,gh->dgphn', m, eye).reshape(2, S5_NP, S5_WIDTH)

    bbd = jnp.concatenate([in_blockdiag(bre), in_blockdiag(bim)], axis=2).astype(BF)
    cbd = jnp.concatenate([out_blockdiag(c_re.astype(F32)), -out_blockdiag(c_im.astype(F32))], axis=1).astype(BF)
    return bbd, lre.reshape(2, 1, S5_NP), lim.reshape(2, 1, S5_NP), cbd


def _s5_scan(u_seq, n_ctx_rows, bbd, lre, lim, cbd, bp):
    rows = u_seq.shape[0]
    steps = 32
    rb = steps * bp
    assert rows % rb == 0 and n_ctx_rows % rb == 0
    nc = n_ctx_rows // rb
    nl = rows // rb - nc

    def in_block(d, i):
        rev = jnp.where(i < nc, nc - 1 - i, nc + nl - 1 - (i - nc))
        return jnp.where(d == 0, i, rev)

    def out_block(d, i):
        k = jnp.maximum(i - nc, 0)
        return jnp.where(d == 0, k, nl - 1 - k)

    per_dir = lambda shape: pl.BlockSpec((1,) + shape, lambda d, i: (d, 0, 0))
    return pl.pallas_call(
        functools.partial(_s5_kernel, steps=steps, bp=bp),
        grid=(2, nc + nl),
        in_specs=[pl.BlockSpec((rb, S5_WIDTH), lambda d, i: (in_block(d, i), 0)), per_dir((S5_WIDTH, 2 * S5_NP)),
                  per_dir((1, S5_NP)), per_dir((1, S5_NP)), per_dir((2 * S5_NP, S5_WIDTH))],
        out_specs=pl.BlockSpec((1, rb, S5_WIDTH), lambda d, i: (d, out_block(d, i), 0)),
        out_shape=jax.ShapeDtypeStruct((2, nl * rb, S5_WIDTH), F32),
        scratch_shapes=[pltpu.VMEM((rb, 2 * S5_NP), F32), pltpu.VMEM((bp, 2 * S5_NP), F32)],
        compiler_params=_cparams(("arbitrary", "arbitrary")),
        name="s5_scan",
    )(u_seq, bbd, lre, lim, cbd)


def _s5_mixer_latent(u_lat, u_ctx, b, params):
    n, c = u_lat.shape[0], u_ctx.shape[0]
    bp = -(-b // 8) * 8
    seq = jnp.concatenate([u_ctx, u_lat], axis=0).reshape(c + n, b, S5_WIDTH)
    seq = jnp.pad(seq, ((0, 0), (0, bp - b), (0, 0))).reshape((c + n) * bp, S5_WIDTH)
    y = _s5_scan(seq, c * bp, *params, bp)
    return (y[0] + y[1]).reshape(n, bp, S5_WIDTH)[:, :b].reshape(n, b * S5_WIDTH)


def _out_odd_kernel(x_ref, oa_ref, ys_ref, u_ref, mod_ref, ng_ref, d_ref, gw_ref, gb_ref, woa_ref, wob_ref,
                    rw_ref, rb_ref, x1_ref, h_ref, gate_ref):
    y = ys_ref[...] + d_ref[...] * u_ref[...].astype(F32)
    gl = 0.5 * y * (1.0 + jnp.tanh(np.sqrt(2.0 / np.pi) * (y + 0.044715 * (y * y * y))))
    z = gl * jax.nn.sigmoid(jnp.dot(gl.astype(BF), gw_ref[...], preferred_element_type=F32) + gb_ref[...])
    mix = (jnp.dot(oa_ref[0], woa_ref[...], preferred_element_type=F32)
           + jnp.dot(z.astype(BF), wob_ref[...], preferred_element_type=F32))
    x1 = x_ref[0] + mod_ref[0, 2:3, :] * mix
    x1_ref[0] = x1
    h = _norm_mod(x1, ng_ref[...], mod_ref[0, 3:4, :], mod_ref[0, 4:5, :]).astype(BF)
    h_ref[0] = h
    logits = jnp.dot(h, rw_ref[...], preferred_element_type=F32) + rb_ref[...]
    lane = lax.broadcasted_iota(jnp.int32, logits.shape, 1)
    m1 = jnp.max(logits, axis=-1, keepdims=True)
    i1 = jnp.min(jnp.where(logits == m1, lane, LANES), axis=-1, keepdims=True)
    rest = jnp.where(lane == i1, -jnp.inf, logits)
    m2 = jnp.max(rest, axis=-1, keepdims=True)
    i2 = jnp.min(jnp.where(rest == m2, lane, LANES), axis=-1, keepdims=True)
    e2 = jnp.exp(m2 - m1)
    gate_ref[0] = jnp.where(lane == i1, 1.0 / (1.0 + e2), 0.0) + jnp.where(lane == i2, e2 / (1.0 + e2), 0.0)


def _out_odd(x, oa, ys, u, mods, norm_gain, d_skip, glu_w, glu_b, w_out, router_w, router_b):
    b, n, d = x.shape
    tm = _token_tile(n)
    wa = oa.shape[2]
    tok = lambda width: pl.BlockSpec((1, tm, width), lambda bi, i: (bi, i, 0))
    vec = lambda width: _const_spec((1, width))
    tmajor = pl.BlockSpec((tm, S5_WIDTH), lambda bi, i: (i, bi))
    return pl.pallas_call(
        _out_odd_kernel,
        grid=(b, n // tm),
        in_specs=[tok(d), tok(wa), tmajor, tmajor, _mod_spec(d, None), vec(d), vec(S5_WIDTH),
                  _const_spec(glu_w.shape), vec(S5_WIDTH), _const_spec((wa, d)), _const_spec((S5_WIDTH, d)),
                  _const_spec(router_w.shape), vec(LANES)],
        out_specs=[tok(d), tok(d), tok(LANES)],
        out_shape=[jax.ShapeDtypeStruct((b, n, d), F32), jax.ShapeDtypeStruct((b, n, d), BF),
                   jax.ShapeDtypeStruct((b, n, LANES), F32)],
        compiler_params=_cparams(("parallel", "parallel")),
        name="out_odd",
    )(x, oa, ys, u, mods, norm_gain, d_skip, glu_w, glu_b, w_out[:wa], w_out[wa:], router_w, router_b)


MOE_CHUNK = 512
MOE_ROWS = 256
MOE_F_CHUNKS = 4
MOE_ALIGN = 16


def _moe_capacity(tb):
    need = 2 * tb + N_EXPERTS * MOE_ALIGN + MOE_ROWS
    return -(-need // MOE_CHUNK) * MOE_CHUNK


def _moe_kernel(h_ref, gate_ref, x1_ref, mod_ref, w13_ref, w2_ref, o_ref, xy_ref, tok_ref, tokt_ref,
                cnt_ref, seg_ref):
    e = pl.program_id(1)
    q = pl.program_id(2)
    last_e = pl.num_programs(1) - 1
    last_q = pl.num_programs(2) - 1
    tb, d = h_ref.shape[1:]
    cap = xy_ref.shape[0]
    ch = MOE_CHUNK

    @pl.when((e == 0) & (q == 0))
    def _pack():
        lane1 = lax.broadcasted_iota(jnp.int32, (1, LANES), 1)
        total = jnp.zeros((1, LANES), F32)
        for c in range(tb // ch):
            total = total + jnp.sum((gate_ref[0, c * ch:(c + 1) * ch, :] > 0.0).astype(F32), axis=0, keepdims=True)
        seg_vec = jnp.zeros((1, LANES), F32)
        start = jnp.int32(0)
        for k in range(N_EXPERTS):
            nk = jnp.sum(jnp.where(lane1 == k, total, 0.0)).astype(jnp.int32)
            cnt_ref[k] = nk
            seg_ref[k] = start
            seg_vec = jnp.where(lane1 == k, start.astype(F32), seg_vec)
            start = start + (nk + MOE_ALIGN - 1) // MOE_ALIGN * MOE_ALIGN
        tri = (lax.broadcasted_iota(jnp.int32, (ch, ch), 0)
               > lax.broadcasted_iota(jnp.int32, (ch, ch), 1)).astype(BF)
        lane = lax.broadcasted_iota(jnp.int32, (ch, LANES), 1)
        offset = jnp.zeros((1, LANES), F32)
        for c in range(tb // ch):
            g = gate_ref[0, c * ch:(c + 1) * ch, :]
            sel = g > 0.0
            sel_f = sel.astype(F32)
            rank = jnp.dot(tri, sel_f.astype(BF), preferred_element_type=F32) + offset
            pos = jnp.where(sel, rank + seg_vec, -1.0)
            pmax = jnp.max(pos, axis=-1, keepdims=True)
            pmin = jnp.min(jnp.where(sel, pos, 1e9), axis=-1, keepdims=True)
            gmin = jnp.sum(jnp.where(pos == pmin, g, 0.0), axis=-1, keepdims=True)
            gmax = jnp.sum(jnp.where(pos == pmax, g, 0.0), axis=-1, keepdims=True)
            pmax = jnp.where(pmax == pmin, -1.0, pmax)
            tok_ref[c * ch:(c + 1) * ch, :] = jnp.where(
                lane == 0, pmin, jnp.where(lane == 1, pmax, jnp.where(lane == 2, gmin, jnp.where(lane == 3, gmax, 0.0))))
            offset = offset + jnp.sum(sel_f, axis=0, keepdims=True)
        tokt_ref[...] = tok_ref[...].T
        pmin_t, pmax_t = tokt_ref[0:1, :], tokt_ref[1:2, :]
        row_id = lax.broadcasted_iota(jnp.int32, (ch, tb), 0).astype(F32)

        def pack_chunk(c, carry):
            r0 = pl.multiple_of(c * ch, ch)
            rid = row_id + r0.astype(F32)
            onehot = jnp.where(rid == pmin_t, 1.0, jnp.where(rid == pmax_t, 1.0, 0.0)).astype(BF)
            xy_ref[pl.ds(r0, ch), :] = jnp.dot(onehot, h_ref[0], preferred_element_type=F32).astype(BF)
            return carry

        lax.fori_loop(0, cap // ch, pack_chunk, 0)

    n = cnt_ref[e]
    seg = seg_ref[e]

    def ffn_tiles(starts, size):
        srcs = [pl.multiple_of(seg + r0, MOE_ALIGN) for r0 in starts]
        xs = [xy_ref[pl.ds(src, size), :] for src in srcs]
        fc = w2_ref.shape[1]
        proj = [jnp.dot(x, w13_ref[0], preferred_element_type=F32) for x in xs]
        ys = [jnp.dot((_silu(ag[:, :fc]) * ag[:, fc:]).astype(BF), w2_ref[0], preferred_element_type=F32)
              for ag in proj]

        @pl.when(q == 0)
        def _():
            for r0, y in zip(starts, ys):
                o_ref[0, pl.ds(r0, size), :] = y

        @pl.when((q > 0) & (q < last_q))
        def _():
            for r0, y in zip(starts, ys):
                o_ref[0, pl.ds(r0, size), :] += y

        @pl.when(q == last_q)
        def _():
            for r0, src, x, y in zip(starts, srcs, xs, ys):
                rid = lax.broadcasted_iota(jnp.int32, (size, tb), 0).astype(F32) + src.astype(F32)
                gate_rows = jnp.sum(jnp.where(rid == tokt_ref[0:1, :], tokt_ref[2:3, :], 0.0)
                                    + jnp.where(rid == tokt_ref[1:2, :], tokt_ref[3:4, :], 0.0),
                                    axis=-1, keepdims=True)
                mine = lax.broadcasted_iota(jnp.int32, (size, 1), 0) < n - r0
                res = gate_rows * (o_ref[0, pl.ds(r0, size), :] + y)
                xy_ref[pl.ds(src, size), :] = jnp.where(mine, res.astype(BF), x)

    rows = MOE_ROWS
    n_full = n // rows
    n_pairs = n_full // 2
    tail = n - n_full * rows

    def tile_pair(i, carry):
        r0 = pl.multiple_of(i * 2 * rows, 2 * rows)
        ffn_tiles([r0, pl.multiple_of(r0 + rows, rows)], rows)
        return carry

    lax.fori_loop(0, n_pairs, tile_pair, 0)
    odd_r0 = pl.multiple_of(n_pairs * 2 * rows, rows)
    tail_r0 = pl.multiple_of(n_full * rows, rows)

    @pl.when(n_full > 2 * n_pairs)
    def _():
        ffn_tiles([odd_r0], rows)

    @pl.when(tail > rows // 2)
    def _():
        ffn_tiles([tail_r0], rows)

    @pl.when((tail > 0) & (tail <= rows // 2))
    def _():
        ffn_tiles([tail_r0], rows // 2)

    @pl.when((e == last_e) & (q == last_q))
    def _combine():
        used = min(cap, -(-(2 * tb + N_EXPERTS * MOE_ALIGN) // MXU_TILE) * MXU_TILE)
        col_id = lax.broadcasted_iota(jnp.int32, (ch, used), 1).astype(F32)

        def combine_chunk(c, carry):
            tok_rows = pl.ds(pl.multiple_of(c * ch, ch), ch)
            pmin, pmax = tok_ref[tok_rows, 0:1], tok_ref[tok_rows, 1:2]
            onehot = jnp.where(col_id == pmin, 1.0, jnp.where(col_id == pmax, 1.0, 0.0)).astype(BF)
            mix = jnp.dot(onehot, xy_ref[0:used, :], preferred_element_type=F32)
            o_ref[0, tok_rows, :] = x1_ref[0, tok_rows, :] + mod_ref[0, 5:6, :] * mix
            return carry

        lax.fori_loop(0, tb // ch, combine_chunk, 0)


def _moe(h, gates, x1, mods, w1, w3, w2):
    b, n, d = h.shape
    n_exp, _, f = w1.shape
    assert n_exp == N_EXPERTS and n % MOE_CHUNK == 0 and MOE_F_CHUNKS > 1
    fc = f // MOE_F_CHUNKS
    w13 = jnp.stack([w1.reshape(n_exp, d, MOE_F_CHUNKS, fc), w3.reshape(n_exp, d, MOE_F_CHUNKS, fc)], axis=3)
    w13 = w13.reshape(n_exp, d, 2 * f)
    per_batch = lambda width: pl.BlockSpec((1, n, width), lambda bi, e, q: (bi, 0, 0), pipeline_mode=pl.Buffered(1))
    return pl.pallas_call(
        _moe_kernel,
        grid=(b, n_exp, MOE_F_CHUNKS),
        in_specs=[per_batch(d), per_batch(LANES), per_batch(d),
                  pl.BlockSpec((1, N_MOD, d), lambda bi, e, q: (bi, 0, 0)),
                  pl.BlockSpec((1, d, 2 * fc), lambda bi, e, q: (e, 0, q)),
                  pl.BlockSpec((1, fc, d), lambda bi, e, q: (e, q, 0))],
        out_specs=per_batch(d),
        out_shape=jax.ShapeDtypeStruct((b, n, d), F32),
        scratch_shapes=[pltpu.VMEM((_moe_capacity(n), d), BF), pltpu.VMEM((n, LANES), F32),
                        pltpu.VMEM((LANES, n), F32), pltpu.SMEM((N_EXPERTS,), jnp.int32),
                        pltpu.SMEM((N_EXPERTS,), jnp.int32)],
        compiler_params=_cparams(("arbitrary", "arbitrary", "arbitrary")),
        name="moe",
    )(h, gates, x1, mods, w13, w2)


def _head_blockdiag(width):
    head = np.arange(width) // HEAD_DIM
    return jnp.asarray((head[:, None] == head[None, :]).astype(np.float32) / HEAD_DIM, dtype=BF)


def _tile_gain(gain, width, scale=1.0):
    return (jnp.tile(gain.astype(F32), width // HEAD_DIM) * scale).reshape(1, width)


def _even_layer(x, xc, mods, ctx_row, norm_mix, norm_ffn, w_in, q_gain, k_gain, rpb, w_out, w1, w3, w2):
    d = x.shape[2]
    bd = _head_blockdiag(NA_WIDTH)
    w_in = w_in.astype(BF)
    qg = _tile_gain(q_gain, NA_WIDTH, Q_SCALE)
    kg = _tile_gain(k_gain, NA_WIDTH)
    ng_mix, ng_ffn = norm_mix.reshape(1, d), norm_ffn.reshape(1, d)
    u, q, k, v = _in_even(x, mods, None, ng_mix, w_in, qg, kg, bd)
    uc, qc, kc, vc = _in_even(xc, mods, ctx_row, ng_mix, w_in, qg, kg, bd)
    o_na = _neighbourhood_attention(q, k, v, kc, vc, rpb)
    o_ctx = _context_attention(qc, kc, vc)
    w_out, w1, w3, w2 = (t.astype(BF) for t in (w_out, w1, w3, w2))
    x = _out_ffn_even(x, _fourier(u), o_na, mods, None, ng_ffn, w_out, w1, w3, w2)
    xc = _out_ffn_even(xc, _fourier(uc), o_ctx, mods, ctx_row, ng_ffn, w_out, w1, w3, w2)
    return x, xc


def _odd_layer(x, xc, mods, ctx_row, norm_mix, norm_ffn, w_in, q_gain, k_gain, s5, d_skip, glu_w, glu_b, w_out,
               router_w, router_b, ew1, ew3, ew2):
    b, n, d = x.shape
    kvw = GKV_HEADS * HEAD_DIM
    wq, wk, wv, wu = jnp.split(w_in, [GQ_WIDTH, GQ_WIDTH + kvw, GQ_WIDTH + 2 * kvw], axis=1)

    def twice(w):
        return jnp.repeat(w.reshape(d, GKV_HEADS, 1, HEAD_DIM), 2, axis=2).reshape(d, GKV_PAD)

    w_lat = jnp.concatenate([wq, twice(wk), twice(wv), wu], axis=1).astype(BF)
    w_ctx = jnp.concatenate([twice(wk), twice(wv), wu], axis=1).astype(BF)
    qg = _tile_gain(q_gain, GQ_WIDTH, Q_SCALE)
    kg = _tile_gain(k_gain, GKV_PAD)
    bdq, bdk = _head_blockdiag(GQ_WIDTH), _head_blockdiag(GKV_PAD)
    cos, sin_signed = _rope_tables(n)
    ng_mix, ng_ffn = norm_mix.reshape(1, d), norm_ffn.reshape(1, d)
    q, kk, vv, u = _in_odd(x, mods, None, ng_mix, w_lat, qg, kg, bdq, bdk, cos, sin_signed, True)
    kkc, vvc, uc = _in_odd(xc, mods, ctx_row, ng_mix, w_ctx, None, kg, None, bdk, None, None, False)
    vvt = jnp.concatenate([vvc, vv], axis=1).transpose(0, 2, 1)
    dup_rows = (jnp.arange(GKV_PAD) % LANES >= HEAD_DIM)[None, :, None]
    vvt = jnp.where(dup_rows, jnp.ones_like(vvt), vvt)
    o_attn = _gqa(q, jnp.concatenate([kkc, kk], axis=1), vvt)
    y_ssm = _s5_mixer_latent(u, uc, b, _s5_params(*s5))
    rw = jnp.pad(router_w, ((0, 0), (0, LANES - N_EXPERTS))).astype(BF)
    rb = jnp.pad(router_b.astype(F32), (0, LANES - N_EXPERTS), constant_values=NEG_INF).reshape(1, LANES)
    x1, h, gates = _out_odd(x, o_attn, y_ssm, u, mods, ng_ffn, d_skip.reshape(1, S5_WIDTH).astype(F32),
                            glu_w.astype(BF), glu_b.reshape(1, S5_WIDTH).astype(F32), w_out.astype(BF), rw, rb)
    return _moe(h, gates, x1, mods, ew1.astype(BF), ew3.astype(BF), ew2.astype(BF))


def kernel(x, c, ctx, c_ctx, ev_mod_w, ev_mod_b, ev_norm_mix, ev_norm_ffn, ev_w_in, ev_q_gain, ev_k_gain, ev_rpb,
           ev_w_out, ev_ffn_w1, ev_ffn_w3, ev_ffn_w2, od_mod_w, od_mod_b, od_norm_mix, od_norm_ffn, od_w_in,
           od_q_gain, od_k_gain, od_s5_a_re, od_s5_a_im, od_s5_log_step, od_s5_b_re, od_s5_b_im, od_s5_c_re,
           od_s5_c_im, od_s5_d, od_s5_glu_w, od_s5_glu_b, od_w_out, od_router_w, od_router_b, od_exp_w1, od_exp_w3,
           od_exp_w2):
    assert ev_mod_w.shape[0] == 1 and od_mod_w.shape[0] == 1, "one even and one odd layer"
    b, n, d = x.shape
    rows = -(-(b + 1) // 8) * 8
    cond = jnp.zeros((rows, d), F32).at[:b].set(c).at[b].set(c_ctx)
    mods_even = _modulation(cond, ev_mod_w[0], ev_mod_b[0])
    mods_odd = _modulation(cond, od_mod_w[0], od_mod_b[0])
    x, xc = _even_layer(x, ctx, mods_even, b, ev_norm_mix[0], ev_norm_ffn[0], ev_w_in[0], ev_q_gain[0], ev_k_gain[0],
                        ev_rpb[0], ev_w_out[0], ev_ffn_w1[0], ev_ffn_w3[0], ev_ffn_w2[0])
    s5 = (od_s5_a_re[0], od_s5_a_im[0], od_s5_log_step[0], od_s5_b_re[0], od_s5_b_im[0], od_s5_c_re[0], od_s5_c_im[0])
    return _odd_layer(x, xc, mods_odd, b, od_norm_mix[0], od_norm_ffn[0], od_w_in[0], od_q_gain[0], od_k_gain[0],
                      s5, od_s5_d[0], od_s5_glu_w[0], od_s5_glu_b[0], od_w_out[0], od_router_w[0], od_router_b[0],
                      od_exp_w1[0], od_exp_w3[0], od_exp_w2[0])
```

```python
import functools

import numpy as np
import jax
import jax.numpy as jnp
from jax import lax
from jax.experimental import pallas as pl
from jax.experimental.pallas import tpu as pltpu

BF = jnp.bfloat16
F32 = jnp.float32

EPS = 1e-6
NEG_INF = -1e30
GRID_W = 64
LANES = 128
MXU_TILE = 256
HEAD_DIM = 64
LOG2E = float(np.log2(np.e))
Q_SCALE = HEAD_DIM ** -0.5 * LOG2E
FN_GROUPS, FN_GROUP_DIM = 4, 128
FN_WIDTH = FN_GROUPS * FN_GROUP_DIM
NA_HEADS = 8
NA_WIDTH = NA_HEADS * HEAD_DIM
NA_KH, NA_KW = 8, 16
NA_QROWS = 4
NA_WIN = NA_QROWS + NA_KH - 1
GQ_HEADS, GKV_HEADS = 12, 3
GQ_WIDTH = GQ_HEADS * HEAD_DIM
GKV_PAD = GKV_HEADS * LANES
S5_GROUPS, S5_GROUP_DIM, S5_STATE = 16, 16, 64
S5_WIDTH = S5_GROUPS * S5_GROUP_DIM
S5_NP = S5_GROUPS * S5_STATE
N_EXPERTS = 8
N_MOD = 6
ROPE_THETA = 10000.0
VMEM_LIMIT = 56 * 1024 * 1024

NT_DIMS = (((1,), (1,)), ((), ()))


def _cparams(sem):
    return pltpu.CompilerParams(dimension_semantics=sem, vmem_limit_bytes=VMEM_LIMIT)


def _const_spec(shape):
    nd = len(shape)
    return pl.BlockSpec(shape, lambda *_: (0,) * nd, pipeline_mode=pl.Buffered(1))


def _token_tile(n):
    return 512 if n % 512 == 0 else 256


def _mod_spec(d, row):
    if row is None:
        return pl.BlockSpec((1, N_MOD, d), lambda b, i: (b, 0, 0))
    return pl.BlockSpec((1, N_MOD, d), lambda b, i: (row, 0, 0))


def _norm_mod(x, gain, shift, scale):
    ms = jnp.mean(x * x, axis=-1, keepdims=True)
    return (x * lax.rsqrt(ms + EPS) * gain) * (1.0 + scale) + shift


def _head_norm(t, blockdiag, gain):
    ms = jnp.dot((t * t).astype(BF), blockdiag, preferred_element_type=F32)
    return t * lax.rsqrt(ms + EPS) * gain


def _silu(a):
    return a * jax.nn.sigmoid(a)


def _mod_kernel(c_ref, w_ref, b_ref, o_ref):
    a = _silu(c_ref[...]).astype(BF)
    o_ref[...] = jnp.dot(a, w_ref[...].astype(BF), preferred_element_type=F32) + b_ref[...]


def _modulation(cond, w, b):
    r, d = cond.shape
    n = w.shape[1]
    tn = n // 4
    out = pl.pallas_call(
        _mod_kernel,
        grid=(n // tn,),
        in_specs=[pl.BlockSpec((r, d), lambda j: (0, 0)),
                  pl.BlockSpec((d, tn), lambda j: (0, j)),
                  pl.BlockSpec((1, tn), lambda j: (0, j))],
        out_specs=pl.BlockSpec((r, tn), lambda j: (0, j)),
        out_shape=jax.ShapeDtypeStruct((r, n), F32),
        compiler_params=_cparams(("arbitrary",)),
        name="modulation",
    )(cond, w, b.reshape(1, n))
    return out.reshape(r, N_MOD, d)


def _in_even_kernel(x_ref, mod_ref, ng_ref, w_ref, qg_ref, kg_ref, bd_ref, u_ref, q_ref, k_ref, v_ref):
    h = _norm_mod(x_ref[0], ng_ref[...], mod_ref[0, 0:1, :], mod_ref[0, 1:2, :])
    p = jnp.dot(h.astype(BF), w_ref[...], preferred_element_type=F32)
    bd = bd_ref[...]
    w = FN_WIDTH
    u_ref[0] = p[:, 0:w].astype(BF)
    q_ref[0] = _head_norm(p[:, w:2 * w], bd, qg_ref[...]).astype(BF)
    k_ref[0] = _head_norm(p[:, 2 * w:3 * w], bd, kg_ref[...]).astype(BF)
    v_ref[0] = p[:, 3 * w:4 * w].astype(BF)


def _in_even(x, mods, mod_row, norm_gain, w_in, q_gain, k_gain, blockdiag):
    b, n, d = x.shape
    tm = _token_tile(n)
    wout = FN_WIDTH
    tok = lambda width: pl.BlockSpec((1, tm, width), lambda bi, i: (bi, i, 0))
    return pl.pallas_call(
        _in_even_kernel,
        grid=(b, n // tm),
        in_specs=[tok(d), _mod_spec(d, mod_row), _const_spec((1, d)), _const_spec(w_in.shape),
                  _const_spec((1, wout)), _const_spec((1, wout)), _const_spec(blockdiag.shape)],
        out_specs=[tok(wout)] * 4,
        out_shape=[jax.ShapeDtypeStruct((b, n, wout), BF)] * 4,
        compiler_params=_cparams(("parallel", "parallel")),
        name="in_even",
    )(x, mods, norm_gain, w_in, q_gain, k_gain, blockdiag)


def _fourier_kernel(u_ref, cm_ref, ml_ref, o_ref, z_ref):
    n = u_ref.shape[1]
    gd = FN_GROUP_DIM
    for g in range(FN_GROUPS):
        z = jnp.dot(u_ref[0, :, g * gd:(g + 1) * gd], cm_ref[...], preferred_element_type=F32)
        z_ref[0:n, g * gd:(g + 1) * gd] = z[:, :gd].astype(BF)
        z_ref[n:2 * n, g * gd:(g + 1) * gd] = z[:, gd:].astype(BF)
    o_ref[0] = jnp.dot(ml_ref[...], z_ref[...], preferred_element_type=F32).astype(BF)


def _dft_tables(n):
    def cos_sin(size):
        j = jnp.arange(size, dtype=jnp.int32)
        ang = ((j[:, None] * j[None, :]) % size).astype(F32) * (2.0 * np.pi / size)
        return jnp.cos(ang), jnp.sin(ang)
    cc, sc = cos_sin(FN_GROUP_DIM)
    cl, sl = cos_sin(n)
    scale = 1.0 / np.sqrt(n * FN_GROUP_DIM)
    return (jnp.concatenate([cc, sc], axis=1).astype(BF),
            (jnp.concatenate([cl, -sl], axis=1) * scale).astype(BF))


def _fourier(u):
    b, n, w = u.shape
    cm, ml = _dft_tables(n)
    return pl.pallas_call(
        _fourier_kernel,
        grid=(b,),
        in_specs=[pl.BlockSpec((1, n, w), lambda bi: (bi, 0, 0)), _const_spec(cm.shape), _const_spec(ml.shape)],
        out_specs=pl.BlockSpec((1, n, w), lambda bi: (bi, 0, 0)),
        out_shape=jax.ShapeDtypeStruct((b, n, w), BF),
        scratch_shapes=[pltpu.VMEM((2 * n, w), BF)],
        compiler_params=_cparams(("parallel",)),
        name="fourier",
    )(u, cm, ml)


def _pair_stack(qp, first_half):
    zero = jnp.zeros_like(qp)
    return jnp.concatenate([jnp.where(first_half, qp, zero), jnp.where(first_half, zero, qp)], axis=0)


def _pair_merge(o, nq, first_half):
    return jnp.where(first_half, o[:nq], o[nq:])


def _na_kernel(q_ref, k_ref, v_ref, kc_ref, vc_ref, bias_ref, o_ref, *, rows):
    rb = pl.program_id(1)
    ws = jnp.minimum(jnp.maximum(NA_QROWS * rb - NA_KH // 2, 0), rows - NA_WIN)
    start = pl.multiple_of(ws * GRID_W, GRID_W)
    nq = NA_QROWS * GRID_W
    nk = NA_WIN * GRID_W
    first_half = lax.broadcasted_iota(jnp.int32, (nq, LANES), 1) < HEAD_DIM
    n_pairs = NA_HEADS // 2

    def scores(p):
        sl = slice(p * LANES, (p + 1) * LANES)
        qs = _pair_stack(q_ref[0, :, sl], first_half)
        bias = jnp.concatenate([bias_ref[0, 2 * p], bias_ref[0, 2 * p + 1]], axis=0)
        return (lax.dot_general(qs, k_ref[0, pl.ds(start, nk), sl], NT_DIMS, preferred_element_type=F32) + bias,
                lax.dot_general(qs, kc_ref[0, :, sl], NT_DIMS, preferred_element_type=F32))

    nxt = scores(0)
    for p in range(n_pairs):
        sl = slice(p * LANES, (p + 1) * LANES)
        s_loc, s_ctx = nxt
        if p + 1 < n_pairs:
            nxt = scores(p + 1)
        vw = v_ref[0, pl.ds(start, nk), sl]
        m = jnp.maximum(jnp.max(s_loc, axis=-1, keepdims=True), jnp.max(s_ctx, axis=-1, keepdims=True))
        p_loc = jnp.exp2(s_loc - m)
        p_ctx = jnp.exp2(s_ctx - m)
        denom = jnp.sum(p_loc, axis=-1, keepdims=True) + jnp.sum(p_ctx, axis=-1, keepdims=True)
        o = (jnp.dot(p_loc.astype(BF), vw, preferred_element_type=F32)
             + jnp.dot(p_ctx.astype(BF), vc_ref[0, :, sl], preferred_element_type=F32)) / denom
        o_ref[0, :, sl] = _pair_merge(o, nq, first_half).astype(BF)


def _na_block_class(rb, n_blocks):
    return int(rb > 0) + int(rb == n_blocks - 1)


def _na_bias_tables(rpb, rows):
    n_blocks = rows // NA_QROWS
    col = np.arange(GRID_W)
    col_start = np.clip(col - NA_KW // 2, 0, GRID_W - NA_KW)
    col_in = (col[None, :] >= col_start[:, None]) & (col[None, :] < col_start[:, None] + NA_KW)
    dc_idx = np.clip(col[None, :] - col[:, None] + NA_KW - 1, 0, 2 * NA_KW - 2)
    per_class = {}
    for rb in range(n_blocks):
        ws = min(max(NA_QROWS * rb - NA_KH // 2, 0), rows - NA_WIN)
        r = NA_QROWS * rb + np.arange(NA_QROWS)
        kr = ws + np.arange(NA_WIN)
        r0 = np.clip(r - NA_KH // 2, 0, rows - NA_KH)
        row_in = (kr[None, :] >= r0[:, None]) & (kr[None, :] < r0[:, None] + NA_KH)
        dr_idx = np.clip(kr[None, :] - r[:, None] + NA_KH - 1, 0, 2 * NA_KH - 2)
        valid = row_in[:, None, :, None] & col_in[None, :, None, :]
        entry = (valid, dr_idx)
        cls = _na_block_class(rb, n_blocks)
        if cls in per_class:
            assert all(np.array_equal(a, b) for a, b in zip(per_class[cls], entry)), "row-block classes differ"
        per_class[cls] = entry
    nq, nk = NA_QROWS * GRID_W, NA_WIN * GRID_W
    onehot = (np.arange(2 * NA_KW - 1)[:, None] == dc_idx.reshape(1, -1)).astype(np.float32)
    by_col = jnp.dot(rpb.astype(F32).reshape(-1, 2 * NA_KW - 1), onehot, precision=lax.Precision.HIGHEST)
    by_col = by_col.reshape(NA_HEADS, 2 * NA_KH - 1, GRID_W, GRID_W)
    tables = []
    for cls in range(3):
        valid, dr_idx = per_class[cls]
        blocks = jnp.stack([by_col[:, int(dr)] for dr in dr_idx.reshape(-1)], axis=1)
        t = blocks.reshape(NA_HEADS, NA_QROWS, NA_WIN, GRID_W, GRID_W).transpose(0, 1, 3, 2, 4)
        tables.append(jnp.where(valid.reshape(nq, nk)[None], t.reshape(NA_HEADS, nq, nk) * LOG2E, NEG_INF))
    return jnp.stack(tables)


def _neighbourhood_attention(q, k, v, kc, vc, rpb):
    b, n, w = q.shape
    c = kc.shape[1]
    rows = n // GRID_W
    assert rows % NA_QROWS == 0 and rows >= NA_WIN + 1
    n_blocks = rows // NA_QROWS
    bias = _na_bias_tables(rpb, rows)
    nq, nk = NA_QROWS * GRID_W, NA_WIN * GRID_W
    full = lambda length: pl.BlockSpec((1, length, w), lambda bi, rb: (bi, 0, 0))
    bias_spec = pl.BlockSpec(
        (1, NA_HEADS, nq, nk),
        lambda bi, rb: (jnp.minimum(rb, 1) + (rb == n_blocks - 1).astype(jnp.int32), 0, 0, 0))
    return pl.pallas_call(
        functools.partial(_na_kernel, rows=rows),
        grid=(b, n_blocks),
        in_specs=[pl.BlockSpec((1, nq, w), lambda bi, rb: (bi, rb, 0)), full(n), full(n), full(c), full(c), bias_spec],
        out_specs=pl.BlockSpec((1, nq, w), lambda bi, rb: (bi, rb, 0)),
        out_shape=jax.ShapeDtypeStruct((b, n, w), BF),
        compiler_params=_cparams(("parallel", "arbitrary")),
        name="neighbourhood_attention",
    )(q, k, v, kc, vc, bias)


def _ctx_attn_kernel(q_ref, k_ref, v_ref, o_ref):
    nq = q_ref.shape[1]
    first_half = lax.broadcasted_iota(jnp.int32, (nq, LANES), 1) < HEAD_DIM
    for p in range(NA_HEADS // 2):
        sl = slice(p * LANES, (p + 1) * LANES)
        qs = _pair_stack(q_ref[0, :, sl], first_half)
        s = lax.dot_general(qs, k_ref[0, :, sl], NT_DIMS, preferred_element_type=F32)
        e = jnp.exp2(s - jnp.max(s, axis=-1, keepdims=True))
        o = jnp.dot(e.astype(BF), v_ref[0, :, sl], preferred_element_type=F32) / jnp.sum(e, axis=-1, keepdims=True)
        o_ref[0, :, sl] = _pair_merge(o, nq, first_half).astype(BF)


def _context_attention(q, k, v):
    b, c, w = q.shape
    spec = pl.BlockSpec((1, c, w), lambda bi: (bi, 0, 0))
    return pl.pallas_call(
        _ctx_attn_kernel,
        grid=(b,),
        in_specs=[spec] * 3,
        out_specs=spec,
        out_shape=jax.ShapeDtypeStruct((b, c, w), BF),
        compiler_params=_cparams(("parallel",)),
        name="context_attention",
    )(q, k, v)


def _out_ffn_even_kernel(x_ref, oa_ref, ob_ref, mod_ref, ng_ref, woa_ref, wob_ref, w1_ref, w3_ref, w2_ref, o_ref,
                         *, f_chunks):
    mix = (jnp.dot(oa_ref[0], woa_ref[...], preferred_element_type=F32)
           + jnp.dot(ob_ref[0], wob_ref[...], preferred_element_type=F32))
    x1 = x_ref[0] + mod_ref[0, 2:3, :] * mix
    h = _norm_mod(x1, ng_ref[...], mod_ref[0, 3:4, :], mod_ref[0, 4:5, :]).astype(BF)
    f = w1_ref.shape[1]
    fc = -(-f // (f_chunks * MXU_TILE)) * MXU_TILE
    y = None
    for lo in range(0, f, fc):
        hi = min(lo + fc, f)
        a = jnp.dot(h, w1_ref[:, lo:hi], preferred_element_type=F32)
        g = jnp.dot(h, w3_ref[:, lo:hi], preferred_element_type=F32)
        part = jnp.dot((_silu(a) * g).astype(BF), w2_ref[lo:hi, :], preferred_element_type=F32)
        y = part if y is None else y + part
    o_ref[0] = x1 + mod_ref[0, 5:6, :] * y


def _out_ffn_even(x, oa, ob, mods, mod_row, norm_gain, w_out, w1, w3, w2):
    b, n, d = x.shape
    tm = _token_tile(n)
    wa = oa.shape[2]
    tok = lambda width: pl.BlockSpec((1, tm, width), lambda bi, i: (bi, i, 0))
    return pl.pallas_call(
        functools.partial(_out_ffn_even_kernel, f_chunks=2),
        grid=(b, n // tm),
        in_specs=[tok(d), tok(wa), tok(ob.shape[2]), _mod_spec(d, mod_row), _const_spec((1, d)),
                  _const_spec((wa, d)), _const_spec((w_out.shape[0] - wa, d)),
                  _const_spec(w1.shape), _const_spec(w3.shape), _const_spec(w2.shape)],
        out_specs=tok(d),
        out_shape=jax.ShapeDtypeStruct((b, n, d), F32),
        compiler_params=_cparams(("parallel", "parallel")),
        name="out_ffn_even",
    )(x, oa, ob, mods, norm_gain, w_out[:wa], w_out[wa:], w1, w3, w2)


def _rope(t, cos, sin_signed):
    width = t.shape[1]
    lane = lax.broadcasted_iota(jnp.int32, t.shape, 1)
    partner = jnp.where(lane % 2 == 0, pltpu.roll(t, width - 1, 1), pltpu.roll(t, 1, 1))
    reps = width // LANES
    return t * jnp.concatenate([cos] * reps, axis=1) + partner * jnp.concatenate([sin_signed] * reps, axis=1)


def _in_odd_kernel(*refs, is_latent):
    if is_latent:
        (x_ref, mod_ref, ng_ref, w_ref, qg_ref, kg_ref, bdq_ref, bdk_ref, cos_ref, sin_ref,
         q_ref, kk_ref, vv_ref, u_ref) = refs
    else:
        x_ref, mod_ref, ng_ref, w_ref, kg_ref, bdk_ref, kk_ref, vv_ref, u_ref = refs
    c0 = GQ_WIDTH if is_latent else 0
    tm = x_ref.shape[1]
    sub = min(tm, 256)
    for r in range(0, tm, sub):
        rs = slice(r, r + sub)
        h = _norm_mod(x_ref[0, rs, :], ng_ref[...], mod_ref[0, 0:1, :], mod_ref[0, 1:2, :])
        p = jnp.dot(h.astype(BF), w_ref[...], preferred_element_type=F32)
        kk = _head_norm(p[:, c0:c0 + GKV_PAD], bdk_ref[...], kg_ref[...])
        if is_latent:
            q = _head_norm(p[:, 0:GQ_WIDTH], bdq_ref[...], qg_ref[...])
            q_ref[0, rs, :] = _rope(q, cos_ref[rs, :], sin_ref[rs, :]).astype(BF)
            kk = _rope(kk, cos_ref[rs, :], sin_ref[rs, :])
        kk_ref[0, rs, :] = kk.astype(BF)
        vv_ref[0, rs, :] = p[:, c0 + GKV_PAD:c0 + 2 * GKV_PAD].astype(BF)
        u_ref[rs, :] = p[:, c0 + 2 * GKV_PAD:].astype(BF)


def _in_odd(x, mods, mod_row, norm_gain, w_ext, q_gain, k_gain, bdq, bdk, cos, sin_signed, is_latent):
    b, n, d = x.shape
    tm = _token_tile(n)
    tok = lambda width: pl.BlockSpec((1, tm, width), lambda bi, i: (bi, i, 0))
    vec = lambda width: _const_spec((1, width))
    if is_latent:
        rope_spec = pl.BlockSpec((tm, LANES), lambda bi, i: (i, 0))
        in_specs = [tok(d), _mod_spec(d, mod_row), vec(d), _const_spec(w_ext.shape), vec(GQ_WIDTH), vec(GKV_PAD),
                    _const_spec(bdq.shape), _const_spec(bdk.shape), rope_spec, rope_spec]
        args = (x, mods, norm_gain, w_ext, q_gain, k_gain, bdq, bdk, cos, sin_signed)
        widths = [GQ_WIDTH, GKV_PAD, GKV_PAD]
    else:
        in_specs = [tok(d), _mod_spec(d, mod_row), vec(d), _const_spec(w_ext.shape), vec(GKV_PAD),
                    _const_spec(bdk.shape)]
        args = (x, mods, norm_gain, w_ext, k_gain, bdk)
        widths = [GKV_PAD, GKV_PAD]
    u_spec = pl.BlockSpec((tm, S5_WIDTH), lambda bi, i: (i, bi))
    return pl.pallas_call(
        functools.partial(_in_odd_kernel, is_latent=is_latent),
        grid=(b, n // tm),
        in_specs=in_specs,
        out_specs=[tok(wd) for wd in widths] + [u_spec],
        out_shape=[jax.ShapeDtypeStruct((b, n, wd), BF) for wd in widths]
        + [jax.ShapeDtypeStruct((n, b * S5_WIDTH), BF)],
        compiler_params=_cparams(("parallel", "parallel")),
        name="in_odd_latent" if is_latent else "in_odd_context",
    )(*args)


def _rope_tables(n):
    t = jnp.arange(n)
    row = (t // GRID_W).astype(F32)
    col = (t % GRID_W).astype(F32)
    n_axis = HEAD_DIM // 4
    freqs = ROPE_THETA ** (-jnp.arange(n_axis, dtype=F32) / n_axis)
    ang = jnp.concatenate([row[:, None] * freqs, col[:, None] * freqs], axis=-1)
    ang = jnp.repeat(ang, 2, axis=-1)
    ang = jnp.concatenate([ang, ang], axis=-1)
    sign = jnp.where(jnp.arange(LANES) % 2 == 0, -1.0, 1.0).astype(F32)
    return jnp.cos(ang), jnp.sin(ang) * sign


def _gqa_kernel(q_ref, kk_ref, vvt_ref, o_ref):
    tq = q_ref.shape[1]
    first_half = lax.broadcasted_iota(jnp.int32, (tq, LANES), 1) < HEAD_DIM
    group = GQ_HEADS // GKV_HEADS

    def scores_t(j):
        kj = kk_ref[0, :, j * LANES:(j + 1) * LANES]
        base = j * group * HEAD_DIM
        qs = jnp.concatenate(
            [_pair_stack(q_ref[0, :, base + i * LANES:base + (i + 1) * LANES], first_half)
             for i in range(group // 2)], axis=0)
        return lax.dot_general(kj, qs, NT_DIMS, preferred_element_type=F32)

    def weights_t(st):
        return jnp.exp2(st - jnp.max(st, axis=0, keepdims=True)).astype(BF)

    def emit(j, e):
        vtj = vvt_ref[0, j * LANES:(j + 1) * LANES, :]
        base = j * group * HEAD_DIM
        ot = jnp.dot(vtj, e, preferred_element_type=F32)
        on = ot[:HEAD_DIM] * (1.0 / ot[HEAD_DIM:HEAD_DIM + 1])
        for i in range(group // 2):
            pair = jnp.concatenate([on[:, 2 * i * tq:(2 * i + 1) * tq], on[:, (2 * i + 1) * tq:(2 * i + 2) * tq]],
                                   axis=0)
            o_ref[0, :, base + i * LANES:base + (i + 1) * LANES] = pair.T.astype(BF)

    st = {j: scores_t(j) for j in range(min(2, GKV_HEADS))}
    e = {}
    for j in range(GKV_HEADS):
        e[j] = weights_t(st.pop(j))
        if j + 2 < GKV_HEADS:
            st[j + 2] = scores_t(j + 2)
        if j >= 1:
            emit(j - 1, e.pop(j - 1))
    emit(GKV_HEADS - 1, e.pop(GKV_HEADS - 1))


def _gqa(q, kk, vvt):
    b, n, w = q.shape
    nk = kk.shape[1]
    tq = 2 * LANES
    return pl.pallas_call(
        _gqa_kernel,
        grid=(b, n // tq),
        in_specs=[pl.BlockSpec((1, tq, w), lambda bi, i: (bi, i, 0)),
                  pl.BlockSpec((1, nk, GKV_PAD), lambda bi, i: (bi, 0, 0)),
                  pl.BlockSpec((1, GKV_PAD, nk), lambda bi, i: (bi, 0, 0))],
        out_specs=pl.BlockSpec((1, tq, w), lambda bi, i: (bi, i, 0)),
        out_shape=jax.ShapeDtypeStruct((b, n, w), BF),
        compiler_params=_cparams(("parallel", "arbitrary")),
        name="gqa",
    )(q, kk, vvt)


def _s5_kernel(u_ref, bbd_ref, lre_ref, lim_ref, cbd_ref, y_ref, bu_ref, st_ref, *, steps, bp):
    backward = pl.program_id(0) == 1

    @pl.when(pl.program_id(1) == 0)
    def _():
        st_ref[...] = jnp.zeros_like(st_ref)

    bu_ref[...] = jnp.dot(u_ref[...], bbd_ref[0], preferred_element_type=F32)
    lc = 2 * LANES
    for j in range(S5_NP // lc):
        re_sl = slice(j * lc, (j + 1) * lc)
        im_sl = slice(S5_NP + j * lc, S5_NP + (j + 1) * lc)
        lre = jnp.broadcast_to(lre_ref[0, :, re_sl], (bp, lc))
        lim = jnp.broadcast_to(lim_ref[0, :, re_sl], (bp, lc))

        def step(s, carry):
            xre, xim = carry
            t = jnp.where(backward, steps - 1 - s, s)
            r0 = pl.multiple_of(t * bp, bp)
            nre = lre * xre - lim * xim + bu_ref[pl.ds(r0, bp), re_sl]
            nim = lre * xim + lim * xre + bu_ref[pl.ds(r0, bp), im_sl]
            bu_ref[pl.ds(r0, bp), re_sl] = nre
            bu_ref[pl.ds(r0, bp), im_sl] = nim
            return nre, nim

        xre, xim = lax.fori_loop(0, steps, step, (st_ref[:, re_sl], st_ref[:, im_sl]), unroll=2)
        st_ref[:, re_sl] = xre
        st_ref[:, im_sl] = xim
    y_ref[0] = jnp.dot(bu_ref[...].astype(BF), cbd_ref[0], preferred_element_type=F32)


def _s5_params(a_re, a_im, log_step, b_re, b_im, c_re, c_im):
    a_re, a_im = a_re.astype(F32), a_im.astype(F32)
    step = jnp.exp(log_step.astype(F32))[..., None]
    mag = jnp.exp(a_re * step)
    lre, lim = mag * jnp.cos(a_im * step), mag * jnp.sin(a_im * step)
    den = a_re * a_re + a_im * a_im
    kre = ((lre - 1.0) * a_re + lim * a_im) / den
    kim = (lim * a_re - (lre - 1.0) * a_im) / den
    bre = kre[..., None] * b_re - kim[..., None] * b_im
    bim = kre[..., None] * b_im + kim[..., None] * b_re
    eye = jnp.eye(S5_GROUPS, dtype=F32)

    def in_blockdiag(m):
        return jnp.einsum('dgpn,gh->dgnhp', m, eye).reshape(2, S5_WIDTH, S5_NP)

    def out_blockdiag(m):
        return jnp.einsum('dgnp,gh->dgphn', m, eye).reshape(2, S5_NP, S5_WIDTH)

    bbd = jnp.concatenate([in_blockdiag(bre), in_blockdiag(bim)], axis=2).astype(BF)
    cbd = jnp.concatenate([out_blockdiag(c_re.astype(F32)), -out_blockdiag(c_im.astype(F32))], axis=1).astype(BF)
    return bbd, lre.reshape(2, 1, S5_NP), lim.reshape(2, 1, S5_NP), cbd


def _s5_scan(u_seq, n_ctx_rows, bbd, lre, lim, cbd, bp):
    rows = u_seq.shape[0]
    steps = 32
    rb = steps * bp
    assert rows % rb == 0 and n_ctx_rows % rb == 0
    nc = n_ctx_rows // rb
    nl = rows // rb - nc

    def in_block(d, i):
        rev = jnp.where(i < nc, nc - 1 - i, nc + nl - 1 - (i - nc))
        return jnp.where(d == 0, i, rev)

    def out_block(d, i):
        k = jnp.maximum(i - nc, 0)
        return jnp.where(d == 0, k, nl - 1 - k)

    per_dir = lambda shape: pl.BlockSpec((1,) + shape, lambda d, i: (d, 0, 0))
    return pl.pallas_call(
        functools.partial(_s5_kernel, steps=steps, bp=bp),
        grid=(2, nc + nl),
        in_specs=[pl.BlockSpec((rb, S5_WIDTH), lambda d, i: (in_block(d, i), 0)), per_dir((S5_WIDTH, 2 * S5_NP)),
                  per_dir((1, S5_NP)), per_dir((1, S5_NP)), per_dir((2 * S5_NP, S5_WIDTH))],
        out_specs=pl.BlockSpec((1, rb, S5_WIDTH), lambda d, i: (d, out_block(d, i), 0)),
        out_shape=jax.ShapeDtypeStruct((2, nl * rb, S5_WIDTH), F32),
        scratch_shapes=[pltpu.VMEM((rb, 2 * S5_NP), F32), pltpu.VMEM((bp, 2 * S5_NP), F32)],
        compiler_params=_cparams(("arbitrary", "arbitrary")),
        name="s5_scan",
    )(u_seq, bbd, lre, lim, cbd)


def _s5_mixer_latent(u_lat, u_ctx, b, params):
    n, c = u_lat.shape[0], u_ctx.shape[0]
    bp = -(-b // 8) * 8
    seq = jnp.concatenate([u_ctx, u_lat], axis=0).reshape(c + n, b, S5_WIDTH)
    seq = jnp.pad(seq, ((0, 0), (0, bp - b), (0, 0))).reshape((c + n) * bp, S5_WIDTH)
    y = _s5_scan(seq, c * bp, *params, bp)
    return (y[0] + y[1]).reshape(n, bp, S5_WIDTH)[:, :b].reshape(n, b * S5_WIDTH)


def _out_odd_kernel(x_ref, oa_ref, ys_ref, u_ref, mod_ref, ng_ref, d_ref, gw_ref, gb_ref, woa_ref, wob_ref,
                    rw_ref, rb_ref, x1_ref, h_ref, gate_ref):
    y = ys_ref[...] + d_ref[...] * u_ref[...].astype(F32)
    gl = 0.5 * y * (1.0 + jnp.tanh(np.sqrt(2.0 / np.pi) * (y + 0.044715 * (y * y * y))))
    z = gl * jax.nn.sigmoid(jnp.dot(gl.astype(BF), gw_ref[...], preferred_element_type=F32) + gb_ref[...])
    mix = (jnp.dot(oa_ref[0], woa_ref[...], preferred_element_type=F32)
           + jnp.dot(z.astype(BF), wob_ref[...], preferred_element_type=F32))
    x1 = x_ref[0] + mod_ref[0, 2:3, :] * mix
    x1_ref[0] = x1
    h = _norm_mod(x1, ng_ref[...], mod_ref[0, 3:4, :], mod_ref[0, 4:5, :]).astype(BF)
    h_ref[0] = h
    logits = jnp.dot(h, rw_ref[...], preferred_element_type=F32) + rb_ref[...]
    lane = lax.broadcasted_iota(jnp.int32, logits.shape, 1)
    m1 = jnp.max(logits, axis=-1, keepdims=True)
    i1 = jnp.min(jnp.where(logits == m1, lane, LANES), axis=-1, keepdims=True)
    rest = jnp.where(lane == i1, -jnp.inf, logits)
    m2 = jnp.max(rest, axis=-1, keepdims=True)
    i2 = jnp.min(jnp.where(rest == m2, lane, LANES), axis=-1, keepdims=True)
    e2 = jnp.exp(m2 - m1)
    gate_ref[0] = jnp.where(lane == i1, 1.0 / (1.0 + e2), 0.0) + jnp.where(lane == i2, e2 / (1.0 + e2), 0.0)


def _out_odd(x, oa, ys, u, mods, norm_gain, d_skip, glu_w, glu_b, w_out, router_w, router_b):
    b, n, d = x.shape
    tm = _token_tile(n)
    wa = oa.shape[2]
    tok = lambda width: pl.BlockSpec((1, tm, width), lambda bi, i: (bi, i, 0))
    vec = lambda width: _const_spec((1, width))
    tmajor = pl.BlockSpec((tm, S5_WIDTH), lambda bi, i: (i, bi))
    return pl.pallas_call(
        _out_odd_kernel,
        grid=(b, n // tm),
        in_specs=[tok(d), tok(wa), tmajor, tmajor, _mod_spec(d, None), vec(d), vec(S5_WIDTH),
                  _const_spec(glu_w.shape), vec(S5_WIDTH), _const_spec((wa, d)), _const_spec((S5_WIDTH, d)),
                  _const_spec(router_w.shape), vec(LANES)],
        out_specs=[tok(d), tok(d), tok(LANES)],
        out_shape=[jax.ShapeDtypeStruct((b, n, d), F32), jax.ShapeDtypeStruct((b, n, d), BF),
                   jax.ShapeDtypeStruct((b, n, LANES), F32)],
        compiler_params=_cparams(("parallel", "parallel")),
        name="out_odd",
    )(x, oa, ys, u, mods, norm_gain, d_skip, glu_w, glu_b, w_out[:wa], w_out[wa:], router_w, router_b)


MOE_CHUNK = 512
MOE_ROWS = 256
MOE_F_CHUNKS = 4
MOE_ALIGN = 16


def _moe_capacity(tb):
    need = 2 * tb + N_EXPERTS * MOE_ALIGN + MOE_ROWS
    return -(-need // MOE_CHUNK) * MOE_CHUNK


def _moe_kernel(h_ref, gate_ref, x1_ref, mod_ref, w13_ref, w2_ref, o_ref, xy_ref, tok_ref, tokt_ref,
                cnt_ref, seg_ref):
    e = pl.program_id(1)
    q = pl.program_id(2)
    last_e = pl.num_programs(1) - 1
    last_q = pl.num_programs(2) - 1
    tb, d = h_ref.shape[1:]
    cap = xy_ref.shape[0]
    ch = MOE_CHUNK

    @pl.when((e == 0) & (q == 0))
    def _pack():
        lane1 = lax.broadcasted_iota(jnp.int32, (1, LANES), 1)
        total = jnp.zeros((1, LANES), F32)
        for c in range(tb // ch):
            total = total + jnp.sum((gate_ref[0, c * ch:(c + 1) * ch, :] > 0.0).astype(F32), axis=0, keepdims=True)
        seg_vec = jnp.zeros((1, LANES), F32)
        start = jnp.int32(0)
        for k in range(N_EXPERTS):
            nk = jnp.sum(jnp.where(lane1 == k, total, 0.0)).astype(jnp.int32)
            cnt_ref[k] = nk
            seg_ref[k] = start
            seg_vec = jnp.where(lane1 == k, start.astype(F32), seg_vec)
            start = start + (nk + MOE_ALIGN - 1) // MOE_ALIGN * MOE_ALIGN
        tri = (lax.broadcasted_iota(jnp.int32, (ch, ch), 0)
               > lax.broadcasted_iota(jnp.int32, (ch, ch), 1)).astype(BF)
        lane = lax.broadcasted_iota(jnp.int32, (ch, LANES), 1)
        offset = jnp.zeros((1, LANES), F32)
        for c in range(tb // ch):
            g = gate_ref[0, c * ch:(c + 1) * ch, :]
            sel = g > 0.0
            sel_f = sel.astype(F32)
            rank = jnp.dot(tri, sel_f.astype(BF), preferred_element_type=F32) + offset
            pos = jnp.where(sel, rank + seg_vec, -1.0)
            pmax = jnp.max(pos, axis=-1, keepdims=True)
            pmin = jnp.min(jnp.where(sel, pos, 1e9), axis=-1, keepdims=True)
            gmin = jnp.sum(jnp.where(pos == pmin, g, 0.0), axis=-1, keepdims=True)
            gmax = jnp.sum(jnp.where(pos == pmax, g, 0.0), axis=-1, keepdims=True)
            pmax = jnp.where(pmax == pmin, -1.0, pmax)
            tok_ref[c * ch:(c + 1) * ch, :] = jnp.where(
                lane == 0, pmin, jnp.where(lane == 1, pmax, jnp.where(lane == 2, gmin, jnp.where(lane == 3, gmax, 0.0))))
            offset = offset + jnp.sum(sel_f, axis=0, keepdims=True)
        tokt_ref[...] = tok_ref[...].T
        pmin_t, pmax_t = tokt_ref[0:1, :], tokt_ref[1:2, :]
        row_id = lax.broadcasted_iota(jnp.int32, (ch, tb), 0).astype(F32)

        def pack_chunk(c, carry):
            r0 = pl.multiple_of(c * ch, ch)
            rid = row_id + r0.astype(F32)
            onehot = jnp.where(rid == pmin_t, 1.0, jnp.where(rid == pmax_t, 1.0, 0.0)).astype(BF)
            xy_ref[pl.ds(r0, ch), :] = jnp.dot(onehot, h_ref[0], preferred_element_type=F32).astype(BF)
            return carry

        lax.fori_loop(0, cap // ch, pack_chunk, 0)

    n = cnt_ref[e]
    seg = seg_ref[e]

    def ffn_tiles(starts, size):
        srcs = [pl.multiple_of(seg + r0, MOE_ALIGN) for r0 in starts]
        xs = [xy_ref[pl.ds(src, size), :] for src in srcs]
        fc = w2_ref.shape[1]
        proj = [jnp.dot(x, w13_ref[0], preferred_element_type=F32) for x in xs]
        ys = [jnp.dot((_silu(ag[:, :fc]) * ag[:, fc:]).astype(BF), w2_ref[0], preferred_element_type=F32)
              for ag in proj]

        @pl.when(q == 0)
        def _():
            for r0, y in zip(starts, ys):
                o_ref[0, pl.ds(r0, size), :] = y

        @pl.when((q > 0) & (q < last_q))
        def _():
            for r0, y in zip(starts, ys):
                o_ref[0, pl.ds(r0, size), :] += y

        @pl.when(q == last_q)
        def _():
            for r0, src, x, y in zip(starts, srcs, xs, ys):
                rid = lax.broadcasted_iota(jnp.int32, (size, tb), 0).astype(F32) + src.astype(F32)
                gate_rows = jnp.sum(jnp.where(rid == tokt_ref[0:1, :], tokt_ref[2:3, :], 0.0)
                                    + jnp.where(rid == tokt_ref[1:2, :], tokt_ref[3:4, :], 0.0),
                                    axis=-1, keepdims=True)
                mine = lax.broadcasted_iota(jnp.int32, (size, 1), 0) < n - r0
                res = gate_rows * (o_ref[0, pl.ds(r0, size), :] + y)
                xy_ref[pl.ds(src, size), :] = jnp.where(mine, res.astype(BF), x)

    rows = MOE_ROWS
    n_full = n // rows
    n_pairs = n_full // 2
    tail = n - n_full * rows

    def tile_pair(i, carry):
        r0 = pl.multiple_of(i * 2 * rows, 2 * rows)
        ffn_tiles([r0, pl.multiple_of(r0 + rows, rows)], rows)
        return carry

    lax.fori_loop(0, n_pairs, tile_pair, 0)
    odd_r0 = pl.multiple_of(n_pairs * 2 * rows, rows)
    tail_r0 = pl.multiple_of(n_full * rows, rows)

    @pl.when(n_full > 2 * n_pairs)
    def _():
        ffn_tiles([odd_r0], rows)

    @pl.when(tail > rows // 2)
    def _():
        ffn_tiles([tail_r0], rows)

    @pl.when((tail > 0) & (tail <= rows // 2))
    def _():
        ffn_tiles([tail_r0], rows // 2)

    @pl.when((e == last_e) & (q == last_q))
    def _combine():
        used = min(cap, -(-(2 * tb + N_EXPERTS * MOE_ALIGN) // MXU_TILE) * MXU_TILE)
        col_id = lax.broadcasted_iota(jnp.int32, (ch, used), 1).astype(F32)

        def combine_chunk(c, carry):
            tok_rows = pl.ds(pl.multiple_of(c * ch, ch), ch)
            pmin, pmax = tok_ref[tok_rows, 0:1], tok_ref[tok_rows, 1:2]
            onehot = jnp.where(col_id == pmin, 1.0, jnp.where(col_id == pmax, 1.0, 0.0)).astype(BF)
            mix = jnp.dot(onehot, xy_ref[0:used, :], preferred_element_type=F32)
            o_ref[0, tok_rows, :] = x1_ref[0, tok_rows, :] + mod_ref[0, 5:6, :] * mix
            return carry

        lax.fori_loop(0, tb // ch, combine_chunk, 0)


def _moe(h, gates, x1, mods, w1, w3, w2):
    b, n, d = h.shape
    n_exp, _, f = w1.shape
    assert n_exp == N_EXPERTS and n % MOE_CHUNK == 0 and MOE_F_CHUNKS > 1
    fc = f // MOE_F_CHUNKS
    w13 = jnp.concatenate([w[:, :, c * fc:(c + 1) * fc].astype(BF) for c in range(MOE_F_CHUNKS) for w in (w1, w3)],
                          axis=2)
    per_batch = lambda width: pl.BlockSpec((1, n, width), lambda bi, e, q: (bi, 0, 0), pipeline_mode=pl.Buffered(1))
    return pl.pallas_call(
        _moe_kernel,
        grid=(b, n_exp, MOE_F_CHUNKS),
        in_specs=[per_batch(d), per_batch(LANES), per_batch(d),
                  pl.BlockSpec((1, N_MOD, d), lambda bi, e, q: (bi, 0, 0)),
                  pl.BlockSpec((1, d, 2 * fc), lambda bi, e, q: (e, 0, q)),
                  pl.BlockSpec((1, fc, d), lambda bi, e, q: (e, q, 0))],
        out_specs=per_batch(d),
        out_shape=jax.ShapeDtypeStruct((b, n, d), F32),
        scratch_shapes=[pltpu.VMEM((_moe_capacity(n), d), BF), pltpu.VMEM((n, LANES), F32),
                        pltpu.VMEM((LANES, n), F32), pltpu.SMEM((N_EXPERTS,), jnp.int32),
                        pltpu.SMEM((N_EXPERTS,), jnp.int32)],
        compiler_params=_cparams(("arbitrary", "arbitrary", "arbitrary")),
        name="moe",
    )(h, gates, x1, mods, w13, w2)


def _head_blockdiag(width):
    head = np.arange(width) // HEAD_DIM
    return jnp.asarray((head[:, None] == head[None, :]).astype(np.float32) / HEAD_DIM, dtype=BF)


def _tile_gain(gain, width, scale=1.0):
    return (jnp.tile(gain.astype(F32), width // HEAD_DIM) * scale).reshape(1, width)


def _even_layer(x, xc, mods, ctx_row, norm_mix, norm_ffn, w_in, q_gain, k_gain, rpb, w_out, w1, w3, w2):
    d = x.shape[2]
    bd = _head_blockdiag(NA_WIDTH)
    w_in = w_in.astype(BF)
    qg = _tile_gain(q_gain, NA_WIDTH, Q_SCALE)
    kg = _tile_gain(k_gain, NA_WIDTH)
    ng_mix, ng_ffn = norm_mix.reshape(1, d), norm_ffn.reshape(1, d)
    u, q, k, v = _in_even(x, mods, None, ng_mix, w_in, qg, kg, bd)
    uc, qc, kc, vc = _in_even(xc, mods, ctx_row, ng_mix, w_in, qg, kg, bd)
    o_na = _neighbourhood_attention(q, k, v, kc, vc, rpb)
    o_ctx = _context_attention(qc, kc, vc)
    w_out, w1, w3, w2 = (t.astype(BF) for t in (w_out, w1, w3, w2))
    x = _out_ffn_even(x, _fourier(u), o_na, mods, None, ng_ffn, w_out, w1, w3, w2)
    xc = _out_ffn_even(xc, _fourier(uc), o_ctx, mods, ctx_row, ng_ffn, w_out, w1, w3, w2)
    return x, xc


def _odd_layer(x, xc, mods, ctx_row, norm_mix, norm_ffn, w_in, q_gain, k_gain, s5, d_skip, glu_w, glu_b, w_out,
               router_w, router_b, ew1, ew3, ew2):
    b, n, d = x.shape
    kvw = GKV_HEADS * HEAD_DIM
    wq, wk, wv, wu = jnp.split(w_in, [GQ_WIDTH, GQ_WIDTH + kvw, GQ_WIDTH + 2 * kvw], axis=1)

    def twice(w):
        return jnp.repeat(w.reshape(d, GKV_HEADS, 1, HEAD_DIM), 2, axis=2).reshape(d, GKV_PAD)

    w_lat = jnp.concatenate([wq, twice(wk), twice(wv), wu], axis=1).astype(BF)
    w_ctx = jnp.concatenate([twice(wk), twice(wv), wu], axis=1).astype(BF)
    qg = _tile_gain(q_gain, GQ_WIDTH, Q_SCALE)
    kg = _tile_gain(k_gain, GKV_PAD)
    bdq, bdk = _head_blockdiag(GQ_WIDTH), _head_blockdiag(GKV_PAD)
    cos, sin_signed = _rope_tables(n)
    ng_mix, ng_ffn = norm_mix.reshape(1, d), norm_ffn.reshape(1, d)
    q, kk, vv, u = _in_odd(x, mods, None, ng_mix, w_lat, qg, kg, bdq, bdk, cos, sin_signed, True)
    kkc, vvc, uc = _in_odd(xc, mods, ctx_row, ng_mix, w_ctx, None, kg, None, bdk, None, None, False)
    vvt = jnp.concatenate([vvc, vv], axis=1).transpose(0, 2, 1)
    dup_rows = (jnp.arange(GKV_PAD) % LANES >= HEAD_DIM)[None, :, None]
    vvt = jnp.where(dup_rows, jnp.ones_like(vvt), vvt)
    o_attn = _gqa(q, jnp.concatenate([kkc, kk], axis=1), vvt)
    y_ssm = _s5_mixer_latent(u, uc, b, _s5_params(*s5))
    rw = jnp.pad(router_w, ((0, 0), (0, LANES - N_EXPERTS))).astype(BF)
    rb = jnp.pad(router_b.astype(F32), (0, LANES - N_EXPERTS), constant_values=NEG_INF).reshape(1, LANES)
    x1, h, gates = _out_odd(x, o_attn, y_ssm, u, mods, ng_ffn, d_skip.reshape(1, S5_WIDTH).astype(F32),
                            glu_w.astype(BF), glu_b.reshape(1, S5_WIDTH).astype(F32), w_out.astype(BF), rw, rb)
    return _moe(h, gates, x1, mods, ew1, ew3, ew2.astype(BF))


def kernel(x, c, ctx, c_ctx, ev_mod_w, ev_mod_b, ev_norm_mix, ev_norm_ffn, ev_w_in, ev_q_gain, ev_k_gain, ev_rpb,
           ev_w_out, ev_ffn_w1, ev_ffn_w3, ev_ffn_w2, od_mod_w, od_mod_b, od_norm_mix, od_norm_ffn, od_w_in,
           od_q_gain, od_k_gain, od_s5_a_re, od_s5_a_im, od_s5_log_step, od_s5_b_re, od_s5_b_im, od_s5_c_re,
           od_s5_c_im, od_s5_d, od_s5_glu_w, od_s5_glu_b, od_w_out, od_router_w, od_router_b, od_exp_w1, od_exp_w3,
           od_exp_w2):
    assert ev_mod_w.shape[0] == 1 and od_mod_w.shape[0] == 1, "one even and one odd layer"
    b, n, d = x.shape
    rows = -(-(b + 1) // 8) * 8
    cond = jnp.zeros((rows, d), F32).at[:b].set(c).at[b].set(c_ctx)
    mods_even = _modulation(cond, ev_mod_w[0], ev_mod_b[0])
    mods_odd = _modulation(cond, od_mod_w[0], od_mod_b[0])
    x, xc = _even_layer(x, ctx, mods_even, b, ev_norm_mix[0], ev_norm_ffn[0], ev_w_in[0], ev_q_gain[0], ev_k_gain[0],
                        ev_rpb[0], ev_w_out[0], ev_ffn_w1[0], ev_ffn_w3[0], ev_ffn_w2[0])
    s5 = (od_s5_a_re[0], od_s5_a_im[0], od_s5_log_step[0], od_s5_b_re[0], od_s5_b_im[0], od_s5_c_re[0], od_s5_c_im[0])
    return _odd_layer(x, xc, mods_odd, b, od_norm_mix[0], od_norm_ffn[0], od_w_in[0], od_q_gain[0], od_k_gain[0],
                      s5, od_s5_d[0], od_s5_glu_w[0], od_s5_glu_b[0], od_w_out[0], od_router_w[0], od_router_b[0],
                      od_exp_w1[0], od_exp_w3[0], od_exp_w2[0])
```

```python
import functools

import numpy as np
import jax
import jax.numpy as jnp
from jax import lax
from jax.experimental import pallas as pl
from jax.experimental.pallas import tpu as pltpu

BF = jnp.bfloat16
F32 = jnp.float32

EPS = 1e-6
NEG_INF = -1e30
GRID_W = 64
LANES = 128
MXU_TILE = 256
HEAD_DIM = 64
LOG2E = float(np.log2(np.e))
Q_SCALE = HEAD_DIM ** -0.5 * LOG2E
FN_GROUPS, FN_GROUP_DIM = 4, 128
FN_WIDTH = FN_GROUPS * FN_GROUP_DIM
NA_HEADS = 8
NA_WIDTH = NA_HEADS * HEAD_DIM
NA_KH, NA_KW = 8, 16
NA_QROWS = 4
NA_WIN = NA_QROWS + NA_KH - 1
GQ_HEADS, GKV_HEADS = 12, 3
GQ_WIDTH = GQ_HEADS * HEAD_DIM
GKV_PAD = GKV_HEADS * LANES
S5_GROUPS, S5_GROUP_DIM, S5_STATE = 16, 16, 64
S5_WIDTH = S5_GROUPS * S5_GROUP_DIM
S5_NP = S5_GROUPS * S5_STATE
N_EXPERTS = 8
N_MOD = 6
ROPE_THETA = 10000.0
VMEM_LIMIT = 56 * 1024 * 1024

NT_DIMS = (((1,), (1,)), ((), ()))


def _cparams(sem):
    return pltpu.CompilerParams(dimension_semantics=sem, vmem_limit_bytes=VMEM_LIMIT)


def _const_spec(shape):
    nd = len(shape)
    return pl.BlockSpec(shape, lambda *_: (0,) * nd, pipeline_mode=pl.Buffered(1))


def _token_tile(n):
    return 512 if n % 512 == 0 else 256


def _mod_spec(d, row):
    if row is None:
        return pl.BlockSpec((1, N_MOD, d), lambda b, i: (b, 0, 0))
    return pl.BlockSpec((1, N_MOD, d), lambda b, i: (row, 0, 0))


def _norm_mod(x, gain, shift, scale):
    ms = jnp.mean(x * x, axis=-1, keepdims=True)
    return (x * lax.rsqrt(ms + EPS) * gain) * (1.0 + scale) + shift


def _head_norm(t, blockdiag, gain):
    ms = jnp.dot((t * t).astype(BF), blockdiag, preferred_element_type=F32)
    return t * lax.rsqrt(ms + EPS) * gain


def _silu(a):
    return a * jax.nn.sigmoid(a)


def _mod_kernel(c_ref, w_ref, b_ref, o_ref):
    a = _silu(c_ref[...]).astype(BF)
    o_ref[...] = jnp.dot(a, w_ref[...].astype(BF), preferred_element_type=F32) + b_ref[...]


def _modulation(cond, w, b):
    r, d = cond.shape
    n = w.shape[1]
    tn = n // 4
    out = pl.pallas_call(
        _mod_kernel,
        grid=(n // tn,),
        in_specs=[pl.BlockSpec((r, d), lambda j: (0, 0)),
                  pl.BlockSpec((d, tn), lambda j: (0, j)),
                  pl.BlockSpec((1, tn), lambda j: (0, j))],
        out_specs=pl.BlockSpec((r, tn), lambda j: (0, j)),
        out_shape=jax.ShapeDtypeStruct((r, n), F32),
        compiler_params=_cparams(("arbitrary",)),
        name="modulation",
    )(cond, w, b.reshape(1, n))
    return out.reshape(r, N_MOD, d)


def _in_even_kernel(x_ref, mod_ref, ng_ref, w_ref, qg_ref, kg_ref, bd_ref, u_ref, q_ref, k_ref, v_ref):
    h = _norm_mod(x_ref[0], ng_ref[...], mod_ref[0, 0:1, :], mod_ref[0, 1:2, :])
    p = jnp.dot(h.astype(BF), w_ref[...], preferred_element_type=F32)
    bd = bd_ref[...]
    w = FN_WIDTH
    u_ref[0] = p[:, 0:w].astype(BF)
    q_ref[0] = _head_norm(p[:, w:2 * w], bd, qg_ref[...]).astype(BF)
    k_ref[0] = _head_norm(p[:, 2 * w:3 * w], bd, kg_ref[...]).astype(BF)
    v_ref[0] = p[:, 3 * w:4 * w].astype(BF)


def _in_even(x, mods, mod_row, norm_gain, w_in, q_gain, k_gain, blockdiag):
    b, n, d = x.shape
    tm = _token_tile(n)
    wout = FN_WIDTH
    tok = lambda width: pl.BlockSpec((1, tm, width), lambda bi, i: (bi, i, 0))
    return pl.pallas_call(
        _in_even_kernel,
        grid=(b, n // tm),
        in_specs=[tok(d), _mod_spec(d, mod_row), _const_spec((1, d)), _const_spec(w_in.shape),
                  _const_spec((1, wout)), _const_spec((1, wout)), _const_spec(blockdiag.shape)],
        out_specs=[tok(wout)] * 4,
        out_shape=[jax.ShapeDtypeStruct((b, n, wout), BF)] * 4,
        compiler_params=_cparams(("parallel", "parallel")),
        name="in_even",
    )(x, mods, norm_gain, w_in, q_gain, k_gain, blockdiag)


def _fourier_kernel(u_ref, cm_ref, ml_ref, o_ref, z_ref):
    n = u_ref.shape[1]
    gd = FN_GROUP_DIM
    for g in range(FN_GROUPS):
        z = jnp.dot(u_ref[0, :, g * gd:(g + 1) * gd], cm_ref[...], preferred_element_type=F32)
        z_ref[0:n, g * gd:(g + 1) * gd] = z[:, :gd].astype(BF)
        z_ref[n:2 * n, g * gd:(g + 1) * gd] = z[:, gd:].astype(BF)
    o_ref[0] = jnp.dot(ml_ref[...], z_ref[...], preferred_element_type=F32).astype(BF)


def _dft_tables(n):
    def cos_sin(size):
        j = jnp.arange(size, dtype=jnp.int32)
        ang = ((j[:, None] * j[None, :]) % size).astype(F32) * (2.0 * np.pi / size)
        return jnp.cos(ang), jnp.sin(ang)
    cc, sc = cos_sin(FN_GROUP_DIM)
    cl, sl = cos_sin(n)
    scale = 1.0 / np.sqrt(n * FN_GROUP_DIM)
    return (jnp.concatenate([cc, sc], axis=1).astype(BF),
            (jnp.concatenate([cl, -sl], axis=1) * scale).astype(BF))


def _fourier(u):
    b, n, w = u.shape
    cm, ml = _dft_tables(n)
    return pl.pallas_call(
        _fourier_kernel,
        grid=(b,),
        in_specs=[pl.BlockSpec((1, n, w), lambda bi: (bi, 0, 0)), _const_spec(cm.shape), _const_spec(ml.shape)],
        out_specs=pl.BlockSpec((1, n, w), lambda bi: (bi, 0, 0)),
        out_shape=jax.ShapeDtypeStruct((b, n, w), BF),
        scratch_shapes=[pltpu.VMEM((2 * n, w), BF)],
        compiler_params=_cparams(("parallel",)),
        name="fourier",
    )(u, cm, ml)


def _pair_stack(qp, first_half):
    zero = jnp.zeros_like(qp)
    return jnp.concatenate([jnp.where(first_half, qp, zero), jnp.where(first_half, zero, qp)], axis=0)


def _pair_merge(o, nq, first_half):
    return jnp.where(first_half, o[:nq], o[nq:])


def _na_kernel(q_ref, k_ref, v_ref, kc_ref, vc_ref, bias_ref, o_ref, *, rows):
    rb = pl.program_id(1)
    ws = jnp.minimum(jnp.maximum(NA_QROWS * rb - NA_KH // 2, 0), rows - NA_WIN)
    start = pl.multiple_of(ws * GRID_W, GRID_W)
    nq = NA_QROWS * GRID_W
    nk = NA_WIN * GRID_W
    first_half = lax.broadcasted_iota(jnp.int32, (nq, LANES), 1) < HEAD_DIM
    n_pairs = NA_HEADS // 2

    def scores(p):
        sl = slice(p * LANES, (p + 1) * LANES)
        qs = _pair_stack(q_ref[0, :, sl], first_half)
        bias = jnp.concatenate([bias_ref[0, 2 * p], bias_ref[0, 2 * p + 1]], axis=0)
        return (lax.dot_general(qs, k_ref[0, pl.ds(start, nk), sl], NT_DIMS, preferred_element_type=F32) + bias,
                lax.dot_general(qs, kc_ref[0, :, sl], NT_DIMS, preferred_element_type=F32))

    nxt = scores(0)
    for p in range(n_pairs):
        sl = slice(p * LANES, (p + 1) * LANES)
        s_loc, s_ctx = nxt
        if p + 1 < n_pairs:
            nxt = scores(p + 1)
        vw = v_ref[0, pl.ds(start, nk), sl]
        m = jnp.maximum(jnp.max(s_loc, axis=-1, keepdims=True), jnp.max(s_ctx, axis=-1, keepdims=True))
        p_loc = jnp.exp2(s_loc - m)
        p_ctx = jnp.exp2(s_ctx - m)
        denom = jnp.sum(p_loc, axis=-1, keepdims=True) + jnp.sum(p_ctx, axis=-1, keepdims=True)
        o = (jnp.dot(p_loc.astype(BF), vw, preferred_element_type=F32)
             + jnp.dot(p_ctx.astype(BF), vc_ref[0, :, sl], preferred_element_type=F32)) / denom
        o_ref[0, :, sl] = _pair_merge(o, nq, first_half).astype(BF)


def _na_block_class(rb, n_blocks):
    return int(rb > 0) + int(rb == n_blocks - 1)


def _na_bias_tables(rpb, rows):
    n_blocks = rows // NA_QROWS
    col = np.arange(GRID_W)
    col_start = np.clip(col - NA_KW // 2, 0, GRID_W - NA_KW)
    col_in = (col[None, :] >= col_start[:, None]) & (col[None, :] < col_start[:, None] + NA_KW)
    dc_idx = np.clip(col[None, :] - col[:, None] + NA_KW - 1, 0, 2 * NA_KW - 2)
    per_class = {}
    for rb in range(n_blocks):
        ws = min(max(NA_QROWS * rb - NA_KH // 2, 0), rows - NA_WIN)
        r = NA_QROWS * rb + np.arange(NA_QROWS)
        kr = ws + np.arange(NA_WIN)
        r0 = np.clip(r - NA_KH // 2, 0, rows - NA_KH)
        row_in = (kr[None, :] >= r0[:, None]) & (kr[None, :] < r0[:, None] + NA_KH)
        dr_idx = np.clip(kr[None, :] - r[:, None] + NA_KH - 1, 0, 2 * NA_KH - 2)
        valid = row_in[:, None, :, None] & col_in[None, :, None, :]
        entry = (valid, dr_idx)
        cls = _na_block_class(rb, n_blocks)
        if cls in per_class:
            assert all(np.array_equal(a, b) for a, b in zip(per_class[cls], entry)), "row-block classes differ"
        per_class[cls] = entry
    nq, nk = NA_QROWS * GRID_W, NA_WIN * GRID_W
    onehot = (np.arange(2 * NA_KW - 1)[:, None] == dc_idx.reshape(1, -1)).astype(np.float32)
    by_col = jnp.dot(rpb.astype(F32).reshape(-1, 2 * NA_KW - 1), onehot, precision=lax.Precision.HIGHEST)
    by_col = by_col.reshape(NA_HEADS, 2 * NA_KH - 1, GRID_W, GRID_W)
    tables = []
    for cls in range(3):
        valid, dr_idx = per_class[cls]
        blocks = jnp.stack([by_col[:, int(dr)] for dr in dr_idx.reshape(-1)], axis=1)
        t = blocks.reshape(NA_HEADS, NA_QROWS, NA_WIN, GRID_W, GRID_W).transpose(0, 1, 3, 2, 4)
        tables.append(jnp.where(valid.reshape(nq, nk)[None], t.reshape(NA_HEADS, nq, nk) * LOG2E, NEG_INF))
    return jnp.stack(tables)


def _neighbourhood_attention(q, k, v, kc, vc, rpb):
    b, n, w = q.shape
    c = kc.shape[1]
    rows = n // GRID_W
    assert rows % NA_QROWS == 0 and rows >= NA_WIN + 1
    n_blocks = rows // NA_QROWS
    bias = _na_bias_tables(rpb, rows)
    nq, nk = NA_QROWS * GRID_W, NA_WIN * GRID_W
    full = lambda length: pl.BlockSpec((1, length, w), lambda bi, rb: (bi, 0, 0))
    bias_spec = pl.BlockSpec(
        (1, NA_HEADS, nq, nk),
        lambda bi, rb: (jnp.minimum(rb, 1) + (rb == n_blocks - 1).astype(jnp.int32), 0, 0, 0))
    return pl.pallas_call(
        functools.partial(_na_kernel, rows=rows),
        grid=(b, n_blocks),
        in_specs=[pl.BlockSpec((1, nq, w), lambda bi, rb: (bi, rb, 0)), full(n), full(n), full(c), full(c), bias_spec],
        out_specs=pl.BlockSpec((1, nq, w), lambda bi, rb: (bi, rb, 0)),
        out_shape=jax.ShapeDtypeStruct((b, n, w), BF),
        compiler_params=_cparams(("parallel", "arbitrary")),
        name="neighbourhood_attention",
    )(q, k, v, kc, vc, bias)


def _ctx_attn_kernel(q_ref, k_ref, v_ref, o_ref):
    nq = q_ref.shape[1]
    first_half = lax.broadcasted_iota(jnp.int32, (nq, LANES), 1) < HEAD_DIM
    for p in range(NA_HEADS // 2):
        sl = slice(p * LANES, (p + 1) * LANES)
        qs = _pair_stack(q_ref[0, :, sl], first_half)
        s = lax.dot_general(qs, k_ref[0, :, sl], NT_DIMS, preferred_element_type=F32)
        e = jnp.exp2(s - jnp.max(s, axis=-1, keepdims=True))
        o = jnp.dot(e.astype(BF), v_ref[0, :, sl], preferred_element_type=F32) / jnp.sum(e, axis=-1, keepdims=True)
        o_ref[0, :, sl] = _pair_merge(o, nq, first_half).astype(BF)


def _context_attention(q, k, v):
    b, c, w = q.shape
    spec = pl.BlockSpec((1, c, w), lambda bi: (bi, 0, 0))
    return pl.pallas_call(
        _ctx_attn_kernel,
        grid=(b,),
        in_specs=[spec] * 3,
        out_specs=spec,
        out_shape=jax.ShapeDtypeStruct((b, c, w), BF),
        compiler_params=_cparams(("parallel",)),
        name="context_attention",
    )(q, k, v)


def _out_ffn_even_kernel(x_ref, oa_ref, ob_ref, mod_ref, ng_ref, woa_ref, wob_ref, w1_ref, w3_ref, w2_ref, o_ref,
                         *, f_chunks):
    mix = (jnp.dot(oa_ref[0], woa_ref[...], preferred_element_type=F32)
           + jnp.dot(ob_ref[0], wob_ref[...], preferred_element_type=F32))
    x1 = x_ref[0] + mod_ref[0, 2:3, :] * mix
    h = _norm_mod(x1, ng_ref[...], mod_ref[0, 3:4, :], mod_ref[0, 4:5, :]).astype(BF)
    f = w1_ref.shape[1]
    fc = -(-f // (f_chunks * MXU_TILE)) * MXU_TILE
    y = None
    for lo in range(0, f, fc):
        hi = min(lo + fc, f)
        a = jnp.dot(h, w1_ref[:, lo:hi], preferred_element_type=F32)
        g = jnp.dot(h, w3_ref[:, lo:hi], preferred_element_type=F32)
        part = jnp.dot((_silu(a) * g).astype(BF), w2_ref[lo:hi, :], preferred_element_type=F32)
        y = part if y is None else y + part
    o_ref[0] = x1 + mod_ref[0, 5:6, :] * y


def _out_ffn_even(x, oa, ob, mods, mod_row, norm_gain, w_out, w1, w3, w2):
    b, n, d = x.shape
    tm = _token_tile(n)
    wa = oa.shape[2]
    tok = lambda width: pl.BlockSpec((1, tm, width), lambda bi, i: (bi, i, 0))
    return pl.pallas_call(
        functools.partial(_out_ffn_even_kernel, f_chunks=2),
        grid=(b, n // tm),
        in_specs=[tok(d), tok(wa), tok(ob.shape[2]), _mod_spec(d, mod_row), _const_spec((1, d)),
                  _const_spec((wa, d)), _const_spec((w_out.shape[0] - wa, d)),
                  _const_spec(w1.shape), _const_spec(w3.shape), _const_spec(w2.shape)],
        out_specs=tok(d),
        out_shape=jax.ShapeDtypeStruct((b, n, d), F32),
        compiler_params=_cparams(("parallel", "parallel")),
        name="out_ffn_even",
    )(x, oa, ob, mods, norm_gain, w_out[:wa], w_out[wa:], w1, w3, w2)


def _rope(t, cos, sin_signed):
    width = t.shape[1]
    lane = lax.broadcasted_iota(jnp.int32, t.shape, 1)
    partner = jnp.where(lane % 2 == 0, pltpu.roll(t, width - 1, 1), pltpu.roll(t, 1, 1))
    reps = width // LANES
    return t * jnp.concatenate([cos] * reps, axis=1) + partner * jnp.concatenate([sin_signed] * reps, axis=1)


def _in_odd_kernel(*refs, is_latent):
    if is_latent:
        (x_ref, mod_ref, ng_ref, w_ref, qg_ref, kg_ref, bdq_ref, bdk_ref, cos_ref, sin_ref,
         q_ref, kk_ref, vv_ref, u_ref) = refs
    else:
        x_ref, mod_ref, ng_ref, w_ref, kg_ref, bdk_ref, kk_ref, vv_ref, u_ref = refs
    c0 = GQ_WIDTH if is_latent else 0
    tm = x_ref.shape[1]
    sub = min(tm, 256)
    for r in range(0, tm, sub):
        rs = slice(r, r + sub)
        h = _norm_mod(x_ref[0, rs, :], ng_ref[...], mod_ref[0, 0:1, :], mod_ref[0, 1:2, :])
        p = jnp.dot(h.astype(BF), w_ref[...], preferred_element_type=F32)
        kk = _head_norm(p[:, c0:c0 + GKV_PAD], bdk_ref[...], kg_ref[...])
        if is_latent:
            q = _head_norm(p[:, 0:GQ_WIDTH], bdq_ref[...], qg_ref[...])
            q_ref[0, rs, :] = _rope(q, cos_ref[rs, :], sin_ref[rs, :]).astype(BF)
            kk = _rope(kk, cos_ref[rs, :], sin_ref[rs, :])
        kk_ref[0, rs, :] = kk.astype(BF)
        vv_ref[0, rs, :] = p[:, c0 + GKV_PAD:c0 + 2 * GKV_PAD].astype(BF)
        u_ref[rs, :] = p[:, c0 + 2 * GKV_PAD:].astype(BF)


def _in_odd(x, mods, mod_row, norm_gain, w_ext, q_gain, k_gain, bdq, bdk, cos, sin_signed, is_latent):
    b, n, d = x.shape
    tm = _token_tile(n)
    tok = lambda width: pl.BlockSpec((1, tm, width), lambda bi, i: (bi, i, 0))
    vec = lambda width: _const_spec((1, width))
    if is_latent:
        rope_spec = pl.BlockSpec((tm, LANES), lambda bi, i: (i, 0))
        in_specs = [tok(d), _mod_spec(d, mod_row), vec(d), _const_spec(w_ext.shape), vec(GQ_WIDTH), vec(GKV_PAD),
                    _const_spec(bdq.shape), _const_spec(bdk.shape), rope_spec, rope_spec]
        args = (x, mods, norm_gain, w_ext, q_gain, k_gain, bdq, bdk, cos, sin_signed)
        widths = [GQ_WIDTH, GKV_PAD, GKV_PAD]
    else:
        in_specs = [tok(d), _mod_spec(d, mod_row), vec(d), _const_spec(w_ext.shape), vec(GKV_PAD),
                    _const_spec(bdk.shape)]
        args = (x, mods, norm_gain, w_ext, k_gain, bdk)
        widths = [GKV_PAD, GKV_PAD]
    u_spec = pl.BlockSpec((tm, S5_WIDTH), lambda bi, i: (i, bi))
    return pl.pallas_call(
        functools.partial(_in_odd_kernel, is_latent=is_latent),
        grid=(b, n // tm),
        in_specs=in_specs,
        out_specs=[tok(wd) for wd in widths] + [u_spec],
        out_shape=[jax.ShapeDtypeStruct((b, n, wd), BF) for wd in widths]
        + [jax.ShapeDtypeStruct((n, b * S5_WIDTH), BF)],
        compiler_params=_cparams(("parallel", "parallel")),
        name="in_odd_latent" if is_latent else "in_odd_context",
    )(*args)


def _rope_tables(n):
    t = jnp.arange(n)
    row = (t // GRID_W).astype(F32)
    col = (t % GRID_W).astype(F32)
    n_axis = HEAD_DIM // 4
    freqs = ROPE_THETA ** (-jnp.arange(n_axis, dtype=F32) / n_axis)
    ang = jnp.concatenate([row[:, None] * freqs, col[:, None] * freqs], axis=-1)
    ang = jnp.repeat(ang, 2, axis=-1)
    ang = jnp.concatenate([ang, ang], axis=-1)
    sign = jnp.where(jnp.arange(LANES) % 2 == 0, -1.0, 1.0).astype(F32)
    return jnp.cos(ang), jnp.sin(ang) * sign


def _gqa_kernel(q_ref, kk_ref, vvt_ref, o_ref):
    tq = q_ref.shape[1]
    first_half = lax.broadcasted_iota(jnp.int32, (tq, LANES), 1) < HEAD_DIM
    group = GQ_HEADS // GKV_HEADS

    def scores_t(j):
        kj = kk_ref[0, :, j * LANES:(j + 1) * LANES]
        base = j * group * HEAD_DIM
        qs = jnp.concatenate(
            [_pair_stack(q_ref[0, :, base + i * LANES:base + (i + 1) * LANES], first_half)
             for i in range(group // 2)], axis=0)
        return lax.dot_general(kj, qs, NT_DIMS, preferred_element_type=F32)

    def weights_t(st):
        return jnp.exp2(st - jnp.max(st, axis=0, keepdims=True)).astype(BF)

    def emit(j, e):
        vtj = vvt_ref[0, j * LANES:(j + 1) * LANES, :]
        base = j * group * HEAD_DIM
        ot = jnp.dot(vtj, e, preferred_element_type=F32)
        on = ot[:HEAD_DIM] * (1.0 / ot[HEAD_DIM:HEAD_DIM + 1])
        for i in range(group // 2):
            pair = jnp.concatenate([on[:, 2 * i * tq:(2 * i + 1) * tq], on[:, (2 * i + 1) * tq:(2 * i + 2) * tq]],
                                   axis=0)
            o_ref[0, :, base + i * LANES:base + (i + 1) * LANES] = pair.T.astype(BF)

    st = {j: scores_t(j) for j in range(min(2, GKV_HEADS))}
    e = {}
    for j in range(GKV_HEADS):
        e[j] = weights_t(st.pop(j))
        if j + 2 < GKV_HEADS:
            st[j + 2] = scores_t(j + 2)
        if j >= 1:
            emit(j - 1, e.pop(j - 1))
    emit(GKV_HEADS - 1, e.pop(GKV_HEADS - 1))


def _gqa(q, kk, vvt):
    b, n, w = q.shape
    nk = kk.shape[1]
    tq = 2 * LANES
    return pl.pallas_call(
        _gqa_kernel,
        grid=(b, n // tq),
        in_specs=[pl.BlockSpec((1, tq, w), lambda bi, i: (bi, i, 0)),
                  pl.BlockSpec((1, nk, GKV_PAD), lambda bi, i: (bi, 0, 0)),
                  pl.BlockSpec((1, GKV_PAD, nk), lambda bi, i: (bi, 0, 0))],
        out_specs=pl.BlockSpec((1, tq, w), lambda bi, i: (bi, i, 0)),
        out_shape=jax.ShapeDtypeStruct((b, n, w), BF),
        compiler_params=_cparams(("parallel", "arbitrary")),
        name="gqa",
    )(q, kk, vvt)


def _s5_kernel(u_ref, bbd_ref, lre_ref, lim_ref, cbd_ref, y_ref, bu_ref, st_ref, *, steps, bp):
    backward = pl.program_id(0) == 1

    @pl.when(pl.program_id(1) == 0)
    def _():
        st_ref[...] = jnp.zeros_like(st_ref)

    bu_ref[...] = jnp.dot(u_ref[...], bbd_ref[0], preferred_element_type=F32)
    lc = 2 * LANES
    for j in range(S5_NP // lc):
        re_sl = slice(j * lc, (j + 1) * lc)
        im_sl = slice(S5_NP + j * lc, S5_NP + (j + 1) * lc)
        lre = jnp.broadcast_to(lre_ref[0, :, re_sl], (bp, lc))
        lim = jnp.broadcast_to(lim_ref[0, :, re_sl], (bp, lc))

        def step(s, carry):
            xre, xim = carry
            t = jnp.where(backward, steps - 1 - s, s)
            r0 = pl.multiple_of(t * bp, bp)
            nre = lre * xre - lim * xim + bu_ref[pl.ds(r0, bp), re_sl]
            nim = lre * xim + lim * xre + bu_ref[pl.ds(r0, bp), im_sl]
            bu_ref[pl.ds(r0, bp), re_sl] = nre
            bu_ref[pl.ds(r0, bp), im_sl] = nim
            return nre, nim

        xre, xim = lax.fori_loop(0, steps, step, (st_ref[:, re_sl], st_ref[:, im_sl]), unroll=2)
        st_ref[:, re_sl] = xre
        st_ref[:, im_sl] = xim
    y_ref[0] = jnp.dot(bu_ref[...].astype(BF), cbd_ref[0], preferred_element_type=F32)


def _s5_params(a_re, a_im, log_step, b_re, b_im, c_re, c_im):
    a_re, a_im = a_re.astype(F32), a_im.astype(F32)
    step = jnp.exp(log_step.astype(F32))[..., None]
    mag = jnp.exp(a_re * step)
    lre, lim = mag * jnp.cos(a_im * step), mag * jnp.sin(a_im * step)
    den = a_re * a_re + a_im * a_im
    kre = ((lre - 1.0) * a_re + lim * a_im) / den
    kim = (lim * a_re - (lre - 1.0) * a_im) / den
    bre = kre[..., None] * b_re - kim[..., None] * b_im
    bim = kre[..., None] * b_im + kim[..., None] * b_re
    eye = jnp.eye(S5_GROUPS, dtype=F32)

    def in_blockdiag(m):
        return jnp.einsum('dgpn,gh->dgnhp', m, eye).reshape(2, S5_WIDTH, S5_NP)

    def out_blockdiag(m):
        return jnp.einsum('dgnp,gh->dgphn', m, eye).reshape(2, S5_NP, S5_WIDTH)

    bbd = jnp.concatenate([in_blockdiag(bre), in_blockdiag(bim)], axis=2).astype(BF)
    cbd = jnp.concatenate([out_blockdiag(c_re.astype(F32)), -out_blockdiag(c_im.astype(F32))], axis=1).astype(BF)
    return bbd, lre.reshape(2, 1, S5_NP), lim.reshape(2, 1, S5_NP), cbd


def _s5_scan(u_seq, n_ctx_rows, bbd, lre, lim, cbd, bp):
    rows = u_seq.shape[0]
    steps = 64
    rb = steps * bp
    assert rows % rb == 0 and n_ctx_rows % rb == 0
    nc = n_ctx_rows // rb
    nl = rows // rb - nc

    def in_block(d, i):
        rev = jnp.where(i < nc, nc - 1 - i, nc + nl - 1 - (i - nc))
        return jnp.where(d == 0, i, rev)

    def out_block(d, i):
        k = jnp.maximum(i - nc, 0)
        return jnp.where(d == 0, k, nl - 1 - k)

    per_dir = lambda shape: pl.BlockSpec((1,) + shape, lambda d, i: (d, 0, 0))
    return pl.pallas_call(
        functools.partial(_s5_kernel, steps=steps, bp=bp),
        grid=(2, nc + nl),
        in_specs=[pl.BlockSpec((rb, S5_WIDTH), lambda d, i: (in_block(d, i), 0)), per_dir((S5_WIDTH, 2 * S5_NP)),
                  per_dir((1, S5_NP)), per_dir((1, S5_NP)), per_dir((2 * S5_NP, S5_WIDTH))],
        out_specs=pl.BlockSpec((1, rb, S5_WIDTH), lambda d, i: (d, out_block(d, i), 0)),
        out_shape=jax.ShapeDtypeStruct((2, nl * rb, S5_WIDTH), F32),
        scratch_shapes=[pltpu.VMEM((rb, 2 * S5_NP), F32), pltpu.VMEM((bp, 2 * S5_NP), F32)],
        compiler_params=_cparams(("arbitrary", "arbitrary")),
        name="s5_scan",
    )(u_seq, bbd, lre, lim, cbd)


def _s5_mixer_latent(u_lat, u_ctx, b, params):
    n, c = u_lat.shape[0], u_ctx.shape[0]
    bp = -(-b // 8) * 8
    seq = jnp.concatenate([u_ctx, u_lat], axis=0).reshape(c + n, b, S5_WIDTH)
    seq = jnp.pad(seq, ((0, 0), (0, bp - b), (0, 0))).reshape((c + n) * bp, S5_WIDTH)
    y = _s5_scan(seq, c * bp, *params, bp)
    return (y[0] + y[1]).reshape(n, bp, S5_WIDTH)[:, :b].reshape(n, b * S5_WIDTH)


def _out_odd_kernel(x_ref, oa_ref, ys_ref, u_ref, mod_ref, ng_ref, d_ref, gw_ref, gb_ref, woa_ref, wob_ref,
                    rw_ref, rb_ref, x1_ref, h_ref, gate_ref):
    y = ys_ref[...] + d_ref[...] * u_ref[...].astype(F32)
    gl = 0.5 * y * (1.0 + jnp.tanh(np.sqrt(2.0 / np.pi) * (y + 0.044715 * (y * y * y))))
    z = gl * jax.nn.sigmoid(jnp.dot(gl.astype(BF), gw_ref[...], preferred_element_type=F32) + gb_ref[...])
    mix = (jnp.dot(oa_ref[0], woa_ref[...], preferred_element_type=F32)
           + jnp.dot(z.astype(BF), wob_ref[...], preferred_element_type=F32))
    x1 = x_ref[0] + mod_ref[0, 2:3, :] * mix
    x1_ref[0] = x1
    h = _norm_mod(x1, ng_ref[...], mod_ref[0, 3:4, :], mod_ref[0, 4:5, :]).astype(BF)
    h_ref[0] = h
    logits = jnp.dot(h, rw_ref[...], preferred_element_type=F32) + rb_ref[...]
    lane = lax.broadcasted_iota(jnp.int32, logits.shape, 1)
    m1 = jnp.max(logits, axis=-1, keepdims=True)
    i1 = jnp.min(jnp.where(logits == m1, lane, LANES), axis=-1, keepdims=True)
    rest = jnp.where(lane == i1, -jnp.inf, logits)
    m2 = jnp.max(rest, axis=-1, keepdims=True)
    i2 = jnp.min(jnp.where(rest == m2, lane, LANES), axis=-1, keepdims=True)
    e2 = jnp.exp(m2 - m1)
    gate_ref[0] = jnp.where(lane == i1, 1.0 / (1.0 + e2), 0.0) + jnp.where(lane == i2, e2 / (1.0 + e2), 0.0)


def _out_odd(x, oa, ys, u, mods, norm_gain, d_skip, glu_w, glu_b, w_out, router_w, router_b):
    b, n, d = x.shape
    tm = _token_tile(n)
    wa = oa.shape[2]
    tok = lambda width: pl.BlockSpec((1, tm, width), lambda bi, i: (bi, i, 0))
    vec = lambda width: _const_spec((1, width))
    tmajor = pl.BlockSpec((tm, S5_WIDTH), lambda bi, i: (i, bi))
    return pl.pallas_call(
        _out_odd_kernel,
        grid=(b, n // tm),
        in_specs=[tok(d), tok(wa), tmajor, tmajor, _mod_spec(d, None), vec(d), vec(S5_WIDTH),
                  _const_spec(glu_w.shape), vec(S5_WIDTH), _const_spec((wa, d)), _const_spec((S5_WIDTH, d)),
                  _const_spec(router_w.shape), vec(LANES)],
        out_specs=[tok(d), tok(d), tok(LANES)],
        out_shape=[jax.ShapeDtypeStruct((b, n, d), F32), jax.ShapeDtypeStruct((b, n, d), BF),
                   jax.ShapeDtypeStruct((b, n, LANES), F32)],
        compiler_params=_cparams(("parallel", "parallel")),
        name="out_odd",
    )(x, oa, ys, u, mods, norm_gain, d_skip, glu_w, glu_b, w_out[:wa], w_out[wa:], router_w, router_b)


MOE_CHUNK = 512
MOE_ROWS = 256
MOE_F_CHUNKS = 4
MOE_ALIGN = 16


def _moe_capacity(tb):
    need = 2 * tb + N_EXPERTS * MOE_ALIGN + MOE_ROWS
    return -(-need // MOE_CHUNK) * MOE_CHUNK


def _moe_kernel(h_ref, gate_ref, x1_ref, mod_ref, w13_ref, w2_ref, o_ref, xy_ref, tok_ref, tokt_ref,
                cnt_ref, seg_ref):
    e = pl.program_id(1)
    q = pl.program_id(2)
    last_e = pl.num_programs(1) - 1
    last_q = pl.num_programs(2) - 1
    tb, d = h_ref.shape[1:]
    cap = xy_ref.shape[0]
    ch = MOE_CHUNK

    @pl.when((e == 0) & (q == 0))
    def _pack():
        o_ref[...] = jnp.zeros_like(o_ref)
        lane1 = lax.broadcasted_iota(jnp.int32, (1, LANES), 1)
        total = jnp.zeros((1, LANES), F32)
        for c in range(tb // ch):
            total = total + jnp.sum((gate_ref[0, c * ch:(c + 1) * ch, :] > 0.0).astype(F32), axis=0, keepdims=True)
        seg_vec = jnp.zeros((1, LANES), F32)
        start = jnp.int32(0)
        for k in range(N_EXPERTS):
            nk = jnp.sum(jnp.where(lane1 == k, total, 0.0)).astype(jnp.int32)
            cnt_ref[k] = nk
            seg_ref[k] = start
            seg_vec = jnp.where(lane1 == k, start.astype(F32), seg_vec)
            start = start + (nk + MOE_ALIGN - 1) // MOE_ALIGN * MOE_ALIGN
        tri = (lax.broadcasted_iota(jnp.int32, (ch, ch), 0)
               > lax.broadcasted_iota(jnp.int32, (ch, ch), 1)).astype(BF)
        lane = lax.broadcasted_iota(jnp.int32, (ch, LANES), 1)
        offset = jnp.zeros((1, LANES), F32)
        for c in range(tb // ch):
            g = gate_ref[0, c * ch:(c + 1) * ch, :]
            sel = g > 0.0
            sel_f = sel.astype(F32)
            rank = jnp.dot(tri, sel_f.astype(BF), preferred_element_type=F32) + offset
            pos = jnp.where(sel, rank + seg_vec, -1.0)
            pmax = jnp.max(pos, axis=-1, keepdims=True)
            pmin = jnp.min(jnp.where(sel, pos, 1e9), axis=-1, keepdims=True)
            gmin = jnp.sum(jnp.where(pos == pmin, g, 0.0), axis=-1, keepdims=True)
            gmax = jnp.sum(jnp.where(pos == pmax, g, 0.0), axis=-1, keepdims=True)
            pmax = jnp.where(pmax == pmin, -1.0, pmax)
            tok_ref[c * ch:(c + 1) * ch, :] = jnp.where(
                lane == 0, pmin, jnp.where(lane == 1, pmax, jnp.where(lane == 2, gmin, jnp.where(lane == 3, gmax, 0.0))))
            offset = offset + jnp.sum(sel_f, axis=0, keepdims=True)
        tokt_ref[...] = tok_ref[...].T
        pmin_t, pmax_t = tokt_ref[0:1, :], tokt_ref[1:2, :]
        row_id = lax.broadcasted_iota(jnp.int32, (ch, tb), 0).astype(F32)

        def pack_chunk(c, carry):
            r0 = pl.multiple_of(c * ch, ch)
            rid = row_id + r0.astype(F32)
            onehot = jnp.where(rid == pmin_t, 1.0, jnp.where(rid == pmax_t, 1.0, 0.0)).astype(BF)
            xy_ref[pl.ds(r0, ch), :] = jnp.dot(onehot, h_ref[0], preferred_element_type=F32).astype(BF)
            return carry

        lax.fori_loop(0, cap // ch, pack_chunk, 0)

    n = cnt_ref[e]
    seg = seg_ref[e]

    def ffn_tiles(starts, size):
        srcs = [pl.multiple_of(seg + r0, MOE_ALIGN) for r0 in starts]
        xs = [xy_ref[pl.ds(src, size), :] for src in srcs]
        fc = w2_ref.shape[1]
        proj = [jnp.dot(x, w13_ref[0], preferred_element_type=F32) for x in xs]
        ys = [jnp.dot((_silu(ag[:, :fc]) * ag[:, fc:]).astype(BF), w2_ref[0], preferred_element_type=F32)
              for ag in proj]

        sums = []
        for r0, y in zip(starts, ys):
            total = jnp.where(q == 0, y, o_ref[0, pl.ds(r0, size), :] + y)
            o_ref[0, pl.ds(r0, size), :] = total
            sums.append(total)

        @pl.when(q == last_q)
        def _():
            for r0, src, x, total in zip(starts, srcs, xs, sums):
                rid = lax.broadcasted_iota(jnp.int32, (size, tb), 0).astype(F32) + src.astype(F32)
                gate_rows = jnp.sum(jnp.where(rid == tokt_ref[0:1, :], tokt_ref[2:3, :], 0.0)
                                    + jnp.where(rid == tokt_ref[1:2, :], tokt_ref[3:4, :], 0.0),
                                    axis=-1, keepdims=True)
                mine = lax.broadcasted_iota(jnp.int32, (size, 1), 0) < n - r0
                xy_ref[pl.ds(src, size), :] = jnp.where(mine, (gate_rows * total).astype(BF), x)

    rows = MOE_ROWS
    n_full = n // rows
    tail = n - n_full * rows
    n_tiles = n_full + (tail > rows // 2).astype(jnp.int32)
    n_pairs = n_tiles // 2

    def tile_pair(i, carry):
        r0 = pl.multiple_of(i * 2 * rows, 2 * rows)
        ffn_tiles([r0, pl.multiple_of(r0 + rows, rows)], rows)
        return carry

    lax.fori_loop(0, n_pairs, tile_pair, 0)

    @pl.when(n_tiles > 2 * n_pairs)
    def _():
        ffn_tiles([pl.multiple_of(n_pairs * 2 * rows, rows)], rows)

    @pl.when((tail > 0) & (tail <= rows // 2))
    def _():
        ffn_tiles([pl.multiple_of(n_full * rows, rows)], rows // 2)

    @pl.when((e == last_e) & (q == last_q))
    def _combine():
        used = min(cap, -(-(2 * tb + N_EXPERTS * MOE_ALIGN) // MXU_TILE) * MXU_TILE)
        col_id = lax.broadcasted_iota(jnp.int32, (ch, used), 1).astype(F32)

        def combine_chunk(c, carry):
            tok_rows = pl.ds(pl.multiple_of(c * ch, ch), ch)
            pmin, pmax = tok_ref[tok_rows, 0:1], tok_ref[tok_rows, 1:2]
            onehot = jnp.where(col_id == pmin, 1.0, jnp.where(col_id == pmax, 1.0, 0.0)).astype(BF)
            mix = jnp.dot(onehot, xy_ref[0:used, :], preferred_element_type=F32)
            o_ref[0, tok_rows, :] = x1_ref[0, tok_rows, :] + mod_ref[0, 5:6, :] * mix
            return carry

        lax.fori_loop(0, tb // ch, combine_chunk, 0)


def _moe(h, gates, x1, mods, w1, w3, w2):
    b, n, d = h.shape
    n_exp, _, f = w1.shape
    assert n_exp == N_EXPERTS and n % MOE_CHUNK == 0 and MOE_F_CHUNKS > 1
    fc = f // MOE_F_CHUNKS
    w13 = jnp.concatenate([w[:, :, c * fc:(c + 1) * fc].astype(BF) for c in range(MOE_F_CHUNKS) for w in (w1, w3)],
                          axis=2)
    per_batch = lambda width: pl.BlockSpec((1, n, width), lambda bi, e, q: (bi, 0, 0), pipeline_mode=pl.Buffered(1))
    return pl.pallas_call(
        _moe_kernel,
        grid=(b, n_exp, MOE_F_CHUNKS),
        in_specs=[per_batch(d), per_batch(LANES), per_batch(d),
                  pl.BlockSpec((1, N_MOD, d), lambda bi, e, q: (bi, 0, 0)),
                  pl.BlockSpec((1, d, 2 * fc), lambda bi, e, q: (e, 0, q)),
                  pl.BlockSpec((1, fc, d), lambda bi, e, q: (e, q, 0))],
        out_specs=per_batch(d),
        out_shape=jax.ShapeDtypeStruct((b, n, d), F32),
        scratch_shapes=[pltpu.VMEM((_moe_capacity(n), d), BF), pltpu.VMEM((n, LANES), F32),
                        pltpu.VMEM((LANES, n), F32), pltpu.SMEM((N_EXPERTS,), jnp.int32),
                        pltpu.SMEM((N_EXPERTS,), jnp.int32)],
        compiler_params=_cparams(("arbitrary", "arbitrary", "arbitrary")),
        name="moe",
    )(h, gates, x1, mods, w13, w2)


def _head_blockdiag(width):
    head = np.arange(width) // HEAD_DIM
    return jnp.asarray((head[:, None] == head[None, :]).astype(np.float32) / HEAD_DIM, dtype=BF)


def _tile_gain(gain, width, scale=1.0):
    return (jnp.tile(gain.astype(F32), width // HEAD_DIM) * scale).reshape(1, width)


def _even_layer(x, xc, mods, ctx_row, norm_mix, norm_ffn, w_in, q_gain, k_gain, rpb, w_out, w1, w3, w2):
    d = x.shape[2]
    bd = _head_blockdiag(NA_WIDTH)
    w_in = w_in.astype(BF)
    qg = _tile_gain(q_gain, NA_WIDTH, Q_SCALE)
    kg = _tile_gain(k_gain, NA_WIDTH)
    ng_mix, ng_ffn = norm_mix.reshape(1, d), norm_ffn.reshape(1, d)
    u, q, k, v = _in_even(x, mods, None, ng_mix, w_in, qg, kg, bd)
    uc, qc, kc, vc = _in_even(xc, mods, ctx_row, ng_mix, w_in, qg, kg, bd)
    o_na = _neighbourhood_attention(q, k, v, kc, vc, rpb)
    o_ctx = _context_attention(qc, kc, vc)
    w_out, w1, w3, w2 = (t.astype(BF) for t in (w_out, w1, w3, w2))
    x = _out_ffn_even(x, _fourier(u), o_na, mods, None, ng_ffn, w_out, w1, w3, w2)
    xc = _out_ffn_even(xc, _fourier(uc), o_ctx, mods, ctx_row, ng_ffn, w_out, w1, w3, w2)
    return x, xc


def _odd_layer(x, xc, mods, ctx_row, norm_mix, norm_ffn, w_in, q_gain, k_gain, s5, d_skip, glu_w, glu_b, w_out,
               router_w, router_b, ew1, ew3, ew2):
    b, n, d = x.shape
    kvw = GKV_HEADS * HEAD_DIM
    wq, wk, wv, wu = jnp.split(w_in, [GQ_WIDTH, GQ_WIDTH + kvw, GQ_WIDTH + 2 * kvw], axis=1)

    def twice(w):
        return jnp.repeat(w.reshape(d, GKV_HEADS, 1, HEAD_DIM), 2, axis=2).reshape(d, GKV_PAD)

    w_lat = jnp.concatenate([wq, twice(wk), twice(wv), wu], axis=1).astype(BF)
    w_ctx = jnp.concatenate([twice(wk), twice(wv), wu], axis=1).astype(BF)
    qg = _tile_gain(q_gain, GQ_WIDTH, Q_SCALE)
    kg = _tile_gain(k_gain, GKV_PAD)
    bdq, bdk = _head_blockdiag(GQ_WIDTH), _head_blockdiag(GKV_PAD)
    cos, sin_signed = _rope_tables(n)
    ng_mix, ng_ffn = norm_mix.reshape(1, d), norm_ffn.reshape(1, d)
    q, kk, vv, u = _in_odd(x, mods, None, ng_mix, w_lat, qg, kg, bdq, bdk, cos, sin_signed, True)
    kkc, vvc, uc = _in_odd(xc, mods, ctx_row, ng_mix, w_ctx, None, kg, None, bdk, None, None, False)
    vvt = jnp.concatenate([vvc, vv], axis=1).transpose(0, 2, 1)
    dup_rows = (jnp.arange(GKV_PAD) % LANES >= HEAD_DIM)[None, :, None]
    vvt = jnp.where(dup_rows, jnp.ones_like(vvt), vvt)
    o_attn = _gqa(q, jnp.concatenate([kkc, kk], axis=1), vvt)
    y_ssm = _s5_mixer_latent(u, uc, b, _s5_params(*s5))
    rw = jnp.pad(router_w, ((0, 0), (0, LANES - N_EXPERTS))).astype(BF)
    rb = jnp.pad(router_b.astype(F32), (0, LANES - N_EXPERTS), constant_values=NEG_INF).reshape(1, LANES)
    x1, h, gates = _out_odd(x, o_attn, y_ssm, u, mods, ng_ffn, d_skip.reshape(1, S5_WIDTH).astype(F32),
                            glu_w.astype(BF), glu_b.reshape(1, S5_WIDTH).astype(F32), w_out.astype(BF), rw, rb)
    return _moe(h, gates, x1, mods, ew1, ew3, ew2.astype(BF))


def kernel(x, c, ctx, c_ctx, ev_mod_w, ev_mod_b, ev_norm_mix, ev_norm_ffn, ev_w_in, ev_q_gain, ev_k_gain, ev_rpb,
           ev_w_out, ev_ffn_w1, ev_ffn_w3, ev_ffn_w2, od_mod_w, od_mod_b, od_norm_mix, od_norm_ffn, od_w_in,
           od_q_gain, od_k_gain, od_s5_a_re, od_s5_a_im, od_s5_log_step, od_s5_b_re, od_s5_b_im, od_s5_c_re,
           od_s5_c_im, od_s5_d, od_s5_glu_w, od_s5_glu_b, od_w_out, od_router_w, od_router_b, od_exp_w1, od_exp_w3,
           od_exp_w2):
    assert ev_mod_w.shape[0] == 1 and od_mod_w.shape[0] == 1, "one even and one odd layer"
    b, n, d = x.shape
    rows = -(-(b + 1) // 8) * 8
    cond = jnp.zeros((rows, d), F32).at[:b].set(c).at[b].set(c_ctx)
    mods_even = _modulation(cond, ev_mod_w[0], ev_mod_b[0])
    mods_odd = _modulation(cond, od_mod_w[0], od_mod_b[0])
    x, xc = _even_layer(x, ctx, mods_even, b, ev_norm_mix[0], ev_norm_ffn[0], ev_w_in[0], ev_q_gain[0], ev_k_gain[0],
                        ev_rpb[0], ev_w_out[0], ev_ffn_w1[0], ev_ffn_w3[0], ev_ffn_w2[0])
    s5 = (od_s5_a_re[0], od_s5_a_im[0], od_s5_log_step[0], od_s5_b_re[0], od_s5_b_im[0], od_s5_c_re[0], od_s5_c_im[0])
    return _odd_layer(x, xc, mods_odd, b, od_norm_mix[0], od_norm_ffn[0], od_w_in[0], od_q_gain[0], od_k_gain[0],
                      s5, od_s5_d[0], od_s5_glu_w[0], od_s5_glu_b[0], od_w_out[0], od_router_w[0], od_router_b[0],
                      od_exp_w1[0], od_exp_w3[0], od_exp_w2[0])
```

```python
import functools

import numpy as np
import jax
import jax.numpy as jnp
from jax import lax
from jax.experimental import pallas as pl
from jax.experimental.pallas import tpu as pltpu

BF = jnp.bfloat16
F32 = jnp.float32

EPS = 1e-6
NEG_INF = -1e30
GRID_W = 64
LANES = 128
MXU_TILE = 256
HEAD_DIM = 64
LOG2E = float(np.log2(np.e))
Q_SCALE = HEAD_DIM ** -0.5 * LOG2E
FN_GROUPS, FN_GROUP_DIM = 4, 128
FN_WIDTH = FN_GROUPS * FN_GROUP_DIM
NA_HEADS = 8
NA_WIDTH = NA_HEADS * HEAD_DIM
NA_KH, NA_KW = 8, 16
NA_QROWS = 4
NA_WIN = NA_QROWS + NA_KH - 1
GQ_HEADS, GKV_HEADS = 12, 3
GQ_WIDTH = GQ_HEADS * HEAD_DIM
GKV_PAD = GKV_HEADS * LANES
S5_GROUPS, S5_GROUP_DIM, S5_STATE = 16, 16, 64
S5_WIDTH = S5_GROUPS * S5_GROUP_DIM
S5_NP = S5_GROUPS * S5_STATE
N_EXPERTS = 8
N_MOD = 6
ROPE_THETA = 10000.0
VMEM_LIMIT = 56 * 1024 * 1024

NT_DIMS = (((1,), (1,)), ((), ()))


def _cparams(sem):
    return pltpu.CompilerParams(dimension_semantics=sem, vmem_limit_bytes=VMEM_LIMIT)


def _const_spec(shape):
    nd = len(shape)
    return pl.BlockSpec(shape, lambda *_: (0,) * nd, pipeline_mode=pl.Buffered(1))


def _token_tile(n):
    return 512 if n % 512 == 0 else 256


def _mod_spec(d, row):
    if row is None:
        return pl.BlockSpec((1, N_MOD, d), lambda b, i: (b, 0, 0))
    return pl.BlockSpec((1, N_MOD, d), lambda b, i: (row, 0, 0))


def _norm_mod(x, gain, shift, scale):
    ms = jnp.mean(x * x, axis=-1, keepdims=True)
    return (x * lax.rsqrt(ms + EPS) * gain) * (1.0 + scale) + shift


def _head_norm(t, blockdiag, gain):
    ms = jnp.dot((t * t).astype(BF), blockdiag, preferred_element_type=F32)
    return t * lax.rsqrt(ms + EPS) * gain


def _silu(a):
    return a * jax.nn.sigmoid(a)


def _mod_kernel(c_ref, w_ref, b_ref, o_ref):
    a = _silu(c_ref[...]).astype(BF)
    o_ref[...] = jnp.dot(a, w_ref[...].astype(BF), preferred_element_type=F32) + b_ref[...]


def _modulation(cond, w, b):
    r, d = cond.shape
    n = w.shape[1]
    tn = n // 4
    out = pl.pallas_call(
        _mod_kernel,
        grid=(n // tn,),
        in_specs=[pl.BlockSpec((r, d), lambda j: (0, 0)),
                  pl.BlockSpec((d, tn), lambda j: (0, j)),
                  pl.BlockSpec((1, tn), lambda j: (0, j))],
        out_specs=pl.BlockSpec((r, tn), lambda j: (0, j)),
        out_shape=jax.ShapeDtypeStruct((r, n), F32),
        compiler_params=_cparams(("arbitrary",)),
        name="modulation",
    )(cond, w, b.reshape(1, n))
    return out.reshape(r, N_MOD, d)


def _in_even_kernel(x_ref, mod_ref, ng_ref, w_ref, qg_ref, kg_ref, bd_ref, u_ref, q_ref, k_ref, v_ref):
    h = _norm_mod(x_ref[0], ng_ref[...], mod_ref[0, 0:1, :], mod_ref[0, 1:2, :])
    p = jnp.dot(h.astype(BF), w_ref[...], preferred_element_type=F32)
    bd = bd_ref[...]
    w = FN_WIDTH
    u_ref[0] = p[:, 0:w].astype(BF)
    q_ref[0] = _head_norm(p[:, w:2 * w], bd, qg_ref[...]).astype(BF)
    k_ref[0] = _head_norm(p[:, 2 * w:3 * w], bd, kg_ref[...]).astype(BF)
    v_ref[0] = p[:, 3 * w:4 * w].astype(BF)


def _in_even(x, mods, mod_row, norm_gain, w_in, q_gain, k_gain, blockdiag):
    b, n, d = x.shape
    tm = _token_tile(n)
    wout = FN_WIDTH
    tok = lambda width: pl.BlockSpec((1, tm, width), lambda bi, i: (bi, i, 0))
    return pl.pallas_call(
        _in_even_kernel,
        grid=(b, n // tm),
        in_specs=[tok(d), _mod_spec(d, mod_row), _const_spec((1, d)), _const_spec(w_in.shape),
                  _const_spec((1, wout)), _const_spec((1, wout)), _const_spec(blockdiag.shape)],
        out_specs=[tok(wout)] * 4,
        out_shape=[jax.ShapeDtypeStruct((b, n, wout), BF)] * 4,
        compiler_params=_cparams(("parallel", "parallel")),
        name="in_even",
    )(x, mods, norm_gain, w_in, q_gain, k_gain, blockdiag)


def _fourier_kernel(u_ref, cm_ref, ml_ref, o_ref, z_ref):
    n = u_ref.shape[1]
    gd = FN_GROUP_DIM
    for g in range(FN_GROUPS):
        z = jnp.dot(u_ref[0, :, g * gd:(g + 1) * gd], cm_ref[...], preferred_element_type=F32)
        z_ref[0:n, g * gd:(g + 1) * gd] = z[:, :gd].astype(BF)
        z_ref[n:2 * n, g * gd:(g + 1) * gd] = z[:, gd:].astype(BF)
    o_ref[0] = jnp.dot(ml_ref[...], z_ref[...], preferred_element_type=F32).astype(BF)


def _dft_tables(n):
    def cos_sin(size):
        j = jnp.arange(size, dtype=jnp.int32)
        ang = ((j[:, None] * j[None, :]) % size).astype(F32) * (2.0 * np.pi / size)
        return jnp.cos(ang), jnp.sin(ang)
    cc, sc = cos_sin(FN_GROUP_DIM)
    cl, sl = cos_sin(n)
    scale = 1.0 / np.sqrt(n * FN_GROUP_DIM)
    return (jnp.concatenate([cc, sc], axis=1).astype(BF),
            (jnp.concatenate([cl, -sl], axis=1) * scale).astype(BF))


def _fourier(u):
    b, n, w = u.shape
    cm, ml = _dft_tables(n)
    return pl.pallas_call(
        _fourier_kernel,
        grid=(b,),
        in_specs=[pl.BlockSpec((1, n, w), lambda bi: (bi, 0, 0)), _const_spec(cm.shape), _const_spec(ml.shape)],
        out_specs=pl.BlockSpec((1, n, w), lambda bi: (bi, 0, 0)),
        out_shape=jax.ShapeDtypeStruct((b, n, w), BF),
        scratch_shapes=[pltpu.VMEM((2 * n, w), BF)],
        compiler_params=_cparams(("parallel",)),
        name="fourier",
    )(u, cm, ml)


def _pair_stack(qp, first_half):
    zero = jnp.zeros_like(qp)
    return jnp.concatenate([jnp.where(first_half, qp, zero), jnp.where(first_half, zero, qp)], axis=0)


def _pair_merge(o, nq, first_half):
    return jnp.where(first_half, o[:nq], o[nq:])


def _na_kernel(q_ref, k_ref, v_ref, kc_ref, vc_ref, bias_ref, o_ref, *, rows):
    rb = pl.program_id(1)
    ws = jnp.minimum(jnp.maximum(NA_QROWS * rb - NA_KH // 2, 0), rows - NA_WIN)
    start = pl.multiple_of(ws * GRID_W, GRID_W)
    nq = NA_QROWS * GRID_W
    nk = NA_WIN * GRID_W
    first_half = lax.broadcasted_iota(jnp.int32, (nq, LANES), 1) < HEAD_DIM
    n_pairs = NA_HEADS // 2

    def scores(p):
        sl = slice(p * LANES, (p + 1) * LANES)
        qs = _pair_stack(q_ref[0, :, sl], first_half)
        bias = jnp.concatenate([bias_ref[0, 2 * p], bias_ref[0, 2 * p + 1]], axis=0)
        return (lax.dot_general(qs, k_ref[0, pl.ds(start, nk), sl], NT_DIMS, preferred_element_type=F32) + bias,
                lax.dot_general(qs, kc_ref[0, :, sl], NT_DIMS, preferred_element_type=F32))

    nxt = scores(0)
    for p in range(n_pairs):
        sl = slice(p * LANES, (p + 1) * LANES)
        s_loc, s_ctx = nxt
        if p + 1 < n_pairs:
            nxt = scores(p + 1)
        vw = v_ref[0, pl.ds(start, nk), sl]
        m = jnp.maximum(jnp.max(s_loc, axis=-1, keepdims=True), jnp.max(s_ctx, axis=-1, keepdims=True))
        p_loc = jnp.exp2(s_loc - m)
        p_ctx = jnp.exp2(s_ctx - m)
        denom = jnp.sum(p_loc, axis=-1, keepdims=True) + jnp.sum(p_ctx, axis=-1, keepdims=True)
        o = (jnp.dot(p_loc.astype(BF), vw, preferred_element_type=F32)
             + jnp.dot(p_ctx.astype(BF), vc_ref[0, :, sl], preferred_element_type=F32)) / denom
        o_ref[0, :, sl] = _pair_merge(o, nq, first_half).astype(BF)


def _na_block_class(rb, n_blocks):
    return int(rb > 0) + int(rb == n_blocks - 1)


def _na_bias_tables(rpb, rows):
    n_blocks = rows // NA_QROWS
    col = np.arange(GRID_W)
    col_start = np.clip(col - NA_KW // 2, 0, GRID_W - NA_KW)
    col_in = (col[None, :] >= col_start[:, None]) & (col[None, :] < col_start[:, None] + NA_KW)
    dc_idx = np.clip(col[None, :] - col[:, None] + NA_KW - 1, 0, 2 * NA_KW - 2)
    per_class = {}
    for rb in range(n_blocks):
        ws = min(max(NA_QROWS * rb - NA_KH // 2, 0), rows - NA_WIN)
        r = NA_QROWS * rb + np.arange(NA_QROWS)
        kr = ws + np.arange(NA_WIN)
        r0 = np.clip(r - NA_KH // 2, 0, rows - NA_KH)
        row_in = (kr[None, :] >= r0[:, None]) & (kr[None, :] < r0[:, None] + NA_KH)
        dr_idx = np.clip(kr[None, :] - r[:, None] + NA_KH - 1, 0, 2 * NA_KH - 2)
        valid = row_in[:, None, :, None] & col_in[None, :, None, :]
        entry = (valid, dr_idx)
        cls = _na_block_class(rb, n_blocks)
        if cls in per_class:
            assert all(np.array_equal(a, b) for a, b in zip(per_class[cls], entry)), "row-block classes differ"
        per_class[cls] = entry
    nq, nk = NA_QROWS * GRID_W, NA_WIN * GRID_W
    onehot = (np.arange(2 * NA_KW - 1)[:, None] == dc_idx.reshape(1, -1)).astype(np.float32)
    by_col = jnp.dot(rpb.astype(F32).reshape(-1, 2 * NA_KW - 1), onehot, precision=lax.Precision.HIGHEST)
    by_col = by_col.reshape(NA_HEADS, 2 * NA_KH - 1, GRID_W, GRID_W)
    tables = []
    for cls in range(3):
        valid, dr_idx = per_class[cls]
        blocks = jnp.stack([by_col[:, int(dr)] for dr in dr_idx.reshape(-1)], axis=1)
        t = blocks.reshape(NA_HEADS, NA_QROWS, NA_WIN, GRID_W, GRID_W).transpose(0, 1, 3, 2, 4)
        tables.append(jnp.where(valid.reshape(nq, nk)[None], t.reshape(NA_HEADS, nq, nk) * LOG2E, NEG_INF))
    return jnp.stack(tables)


def _neighbourhood_attention(q, k, v, kc, vc, rpb):
    b, n, w = q.shape
    c = kc.shape[1]
    rows = n // GRID_W
    assert rows % NA_QROWS == 0 and rows >= NA_WIN + 1
    n_blocks = rows // NA_QROWS
    bias = _na_bias_tables(rpb, rows)
    nq, nk = NA_QROWS * GRID_W, NA_WIN * GRID_W
    full = lambda length: pl.BlockSpec((1, length, w), lambda bi, rb: (bi, 0, 0))
    bias_spec = pl.BlockSpec(
        (1, NA_HEADS, nq, nk),
        lambda bi, rb: (jnp.minimum(rb, 1) + (rb == n_blocks - 1).astype(jnp.int32), 0, 0, 0))
    return pl.pallas_call(
        functools.partial(_na_kernel, rows=rows),
        grid=(b, n_blocks),
        in_specs=[pl.BlockSpec((1, nq, w), lambda bi, rb: (bi, rb, 0)), full(n), full(n), full(c), full(c), bias_spec],
        out_specs=pl.BlockSpec((1, nq, w), lambda bi, rb: (bi, rb, 0)),
        out_shape=jax.ShapeDtypeStruct((b, n, w), BF),
        compiler_params=_cparams(("parallel", "arbitrary")),
        name="neighbourhood_attention",
    )(q, k, v, kc, vc, bias)


def _ctx_attn_kernel(q_ref, k_ref, v_ref, o_ref):
    nq = q_ref.shape[1]
    first_half = lax.broadcasted_iota(jnp.int32, (nq, LANES), 1) < HEAD_DIM
    for p in range(NA_HEADS // 2):
        sl = slice(p * LANES, (p + 1) * LANES)
        qs = _pair_stack(q_ref[0, :, sl], first_half)
        s = lax.dot_general(qs, k_ref[0, :, sl], NT_DIMS, preferred_element_type=F32)
        e = jnp.exp2(s - jnp.max(s, axis=-1, keepdims=True))
        o = jnp.dot(e.astype(BF), v_ref[0, :, sl], preferred_element_type=F32) / jnp.sum(e, axis=-1, keepdims=True)
        o_ref[0, :, sl] = _pair_merge(o, nq, first_half).astype(BF)


def _context_attention(q, k, v):
    b, c, w = q.shape
    spec = pl.BlockSpec((1, c, w), lambda bi: (bi, 0, 0))
    return pl.pallas_call(
        _ctx_attn_kernel,
        grid=(b,),
        in_specs=[spec] * 3,
        out_specs=spec,
        out_shape=jax.ShapeDtypeStruct((b, c, w), BF),
        compiler_params=_cparams(("parallel",)),
        name="context_attention",
    )(q, k, v)


def _out_ffn_even_kernel(x_ref, oa_ref, ob_ref, mod_ref, ng_ref, woa_ref, wob_ref, w1_ref, w3_ref, w2_ref, o_ref,
                         *, f_chunks):
    mix = (jnp.dot(oa_ref[0], woa_ref[...], preferred_element_type=F32)
           + jnp.dot(ob_ref[0], wob_ref[...], preferred_element_type=F32))
    x1 = x_ref[0] + mod_ref[0, 2:3, :] * mix
    h = _norm_mod(x1, ng_ref[...], mod_ref[0, 3:4, :], mod_ref[0, 4:5, :]).astype(BF)
    f = w1_ref.shape[1]
    fc = -(-f // (f_chunks * MXU_TILE)) * MXU_TILE
    y = None
    for lo in range(0, f, fc):
        hi = min(lo + fc, f)
        a = jnp.dot(h, w1_ref[:, lo:hi], preferred_element_type=F32)
        g = jnp.dot(h, w3_ref[:, lo:hi], preferred_element_type=F32)
        part = jnp.dot((_silu(a) * g).astype(BF), w2_ref[lo:hi, :], preferred_element_type=F32)
        y = part if y is None else y + part
    o_ref[0] = x1 + mod_ref[0, 5:6, :] * y


def _out_ffn_even(x, oa, ob, mods, mod_row, norm_gain, w_out, w1, w3, w2):
    b, n, d = x.shape
    tm = _token_tile(n)
    wa = oa.shape[2]
    tok = lambda width: pl.BlockSpec((1, tm, width), lambda bi, i: (bi, i, 0))
    return pl.pallas_call(
        functools.partial(_out_ffn_even_kernel, f_chunks=2),
        grid=(b, n // tm),
        in_specs=[tok(d), tok(wa), tok(ob.shape[2]), _mod_spec(d, mod_row), _const_spec((1, d)),
                  _const_spec((wa, d)), _const_spec((w_out.shape[0] - wa, d)),
                  _const_spec(w1.shape), _const_spec(w3.shape), _const_spec(w2.shape)],
        out_specs=tok(d),
        out_shape=jax.ShapeDtypeStruct((b, n, d), F32),
        compiler_params=_cparams(("parallel", "parallel")),
        name="out_ffn_even",
    )(x, oa, ob, mods, norm_gain, w_out[:wa], w_out[wa:], w1, w3, w2)


def _rope(t, cos, sin_signed):
    width = t.shape[1]
    lane = lax.broadcasted_iota(jnp.int32, t.shape, 1)
    partner = jnp.where(lane % 2 == 0, pltpu.roll(t, width - 1, 1), pltpu.roll(t, 1, 1))
    reps = width // LANES
    return t * jnp.concatenate([cos] * reps, axis=1) + partner * jnp.concatenate([sin_signed] * reps, axis=1)


def _in_odd_kernel(*refs, is_latent):
    if is_latent:
        (x_ref, mod_ref, ng_ref, w_ref, qg_ref, kg_ref, bdq_ref, bdk_ref, cos_ref, sin_ref,
         q_ref, kk_ref, vv_ref, u_ref) = refs
    else:
        x_ref, mod_ref, ng_ref, w_ref, kg_ref, bdk_ref, kk_ref, vv_ref, u_ref = refs
    c0 = GQ_WIDTH if is_latent else 0
    tm = x_ref.shape[1]
    sub = min(tm, 256)
    for r in range(0, tm, sub):
        rs = slice(r, r + sub)
        h = _norm_mod(x_ref[0, rs, :], ng_ref[...], mod_ref[0, 0:1, :], mod_ref[0, 1:2, :])
        p = jnp.dot(h.astype(BF), w_ref[...], preferred_element_type=F32)
        kk = _head_norm(p[:, c0:c0 + GKV_PAD], bdk_ref[...], kg_ref[...])
        if is_latent:
            q = _head_norm(p[:, 0:GQ_WIDTH], bdq_ref[...], qg_ref[...])
            q_ref[0, rs, :] = _rope(q, cos_ref[rs, :], sin_ref[rs, :]).astype(BF)
            kk = _rope(kk, cos_ref[rs, :], sin_ref[rs, :])
        kk_ref[0, rs, :] = kk.astype(BF)
        vv_ref[0, rs, :] = p[:, c0 + GKV_PAD:c0 + 2 * GKV_PAD].astype(BF)
        u_ref[rs, :] = p[:, c0 + 2 * GKV_PAD:].astype(BF)


def _in_odd(x, mods, mod_row, norm_gain, w_ext, q_gain, k_gain, bdq, bdk, cos, sin_signed, is_latent):
    b, n, d = x.shape
    tm = _token_tile(n)
    tok = lambda width: pl.BlockSpec((1, tm, width), lambda bi, i: (bi, i, 0))
    vec = lambda width: _const_spec((1, width))
    if is_latent:
        rope_spec = pl.BlockSpec((tm, LANES), lambda bi, i: (i, 0))
        in_specs = [tok(d), _mod_spec(d, mod_row), vec(d), _const_spec(w_ext.shape), vec(GQ_WIDTH), vec(GKV_PAD),
                    _const_spec(bdq.shape), _const_spec(bdk.shape), rope_spec, rope_spec]
        args = (x, mods, norm_gain, w_ext, q_gain, k_gain, bdq, bdk, cos, sin_signed)
        widths = [GQ_WIDTH, GKV_PAD, GKV_PAD]
    else:
        in_specs = [tok(d), _mod_spec(d, mod_row), vec(d), _const_spec(w_ext.shape), vec(GKV_PAD),
                    _const_spec(bdk.shape)]
        args = (x, mods, norm_gain, w_ext, k_gain, bdk)
        widths = [GKV_PAD, GKV_PAD]
    u_spec = pl.BlockSpec((tm, S5_WIDTH), lambda bi, i: (i, bi))
    return pl.pallas_call(
        functools.partial(_in_odd_kernel, is_latent=is_latent),
        grid=(b, n // tm),
        in_specs=in_specs,
        out_specs=[tok(wd) for wd in widths] + [u_spec],
        out_shape=[jax.ShapeDtypeStruct((b, n, wd), BF) for wd in widths]
        + [jax.ShapeDtypeStruct((n, b * S5_WIDTH), BF)],
        compiler_params=_cparams(("parallel", "parallel")),
        name="in_odd_latent" if is_latent else "in_odd_context",
    )(*args)


def _rope_tables(n):
    t = jnp.arange(n)
    row = (t // GRID_W).astype(F32)
    col = (t % GRID_W).astype(F32)
    n_axis = HEAD_DIM // 4
    freqs = ROPE_THETA ** (-jnp.arange(n_axis, dtype=F32) / n_axis)
    ang = jnp.concatenate([row[:, None] * freqs, col[:, None] * freqs], axis=-1)
    ang = jnp.repeat(ang, 2, axis=-1)
    ang = jnp.concatenate([ang, ang], axis=-1)
    sign = jnp.where(jnp.arange(LANES) % 2 == 0, -1.0, 1.0).astype(F32)
    return jnp.cos(ang), jnp.sin(ang) * sign


def _gqa_kernel(q_ref, kk_ref, vvt_ref, o_ref):
    tq = q_ref.shape[1]
    first_half = lax.broadcasted_iota(jnp.int32, (tq, LANES), 1) < HEAD_DIM
    group = GQ_HEADS // GKV_HEADS

    def scores_t(j):
        kj = kk_ref[0, :, j * LANES:(j + 1) * LANES]
        base = j * group * HEAD_DIM
        qs = jnp.concatenate(
            [_pair_stack(q_ref[0, :, base + i * LANES:base + (i + 1) * LANES], first_half)
             for i in range(group // 2)], axis=0)
        return lax.dot_general(kj, qs, NT_DIMS, preferred_element_type=F32)

    def weights_t(st):
        return jnp.exp2(st - jnp.max(st, axis=0, keepdims=True)).astype(BF)

    def emit(j, e):
        vtj = vvt_ref[0, j * LANES:(j + 1) * LANES, :]
        base = j * group * HEAD_DIM
        ot = jnp.dot(vtj, e, preferred_element_type=F32)
        on = ot[:HEAD_DIM] * (1.0 / ot[HEAD_DIM:HEAD_DIM + 1])
        for i in range(group // 2):
            pair = jnp.concatenate([on[:, 2 * i * tq:(2 * i + 1) * tq], on[:, (2 * i + 1) * tq:(2 * i + 2) * tq]],
                                   axis=0)
            o_ref[0, :, base + i * LANES:base + (i + 1) * LANES] = pair.T.astype(BF)

    st = {j: scores_t(j) for j in range(min(2, GKV_HEADS))}
    e = {}
    for j in range(GKV_HEADS):
        e[j] = weights_t(st.pop(j))
        if j + 2 < GKV_HEADS:
            st[j + 2] = scores_t(j + 2)
        if j >= 1:
            emit(j - 1, e.pop(j - 1))
    emit(GKV_HEADS - 1, e.pop(GKV_HEADS - 1))


def _gqa(q, kk, vvt):
    b, n, w = q.shape
    nk = kk.shape[1]
    tq = 2 * LANES
    return pl.pallas_call(
        _gqa_kernel,
        grid=(b, n // tq),
        in_specs=[pl.BlockSpec((1, tq, w), lambda bi, i: (bi, i, 0)),
                  pl.BlockSpec((1, nk, GKV_PAD), lambda bi, i: (bi, 0, 0)),
                  pl.BlockSpec((1, GKV_PAD, nk), lambda bi, i: (bi, 0, 0))],
        out_specs=pl.BlockSpec((1, tq, w), lambda bi, i: (bi, i, 0)),
        out_shape=jax.ShapeDtypeStruct((b, n, w), BF),
        compiler_params=_cparams(("parallel", "arbitrary")),
        name="gqa",
    )(q, kk, vvt)


def _s5_kernel(u_ref, bbd_ref, lre_ref, lim_ref, cbd_ref, y_ref, bu_ref, st_ref, *, steps, bp):
    backward = pl.program_id(0) == 1

    @pl.when(pl.program_id(1) == 0)
    def _():
        st_ref[...] = jnp.zeros_like(st_ref)

    bu_ref[...] = jnp.dot(u_ref[...], bbd_ref[0], preferred_element_type=F32)
    lc = 2 * LANES
    for j in range(S5_NP // lc):
        re_sl = slice(j * lc, (j + 1) * lc)
        im_sl = slice(S5_NP + j * lc, S5_NP + (j + 1) * lc)
        lre = jnp.broadcast_to(lre_ref[0, :, re_sl], (bp, lc))
        lim = jnp.broadcast_to(lim_ref[0, :, re_sl], (bp, lc))

        def step(s, carry):
            xre, xim = carry
            t = jnp.where(backward, steps - 1 - s, s)
            r0 = pl.multiple_of(t * bp, bp)
            nre = lre * xre - lim * xim + bu_ref[pl.ds(r0, bp), re_sl]
            nim = lre * xim + lim * xre + bu_ref[pl.ds(r0, bp), im_sl]
            bu_ref[pl.ds(r0, bp), re_sl] = nre
            bu_ref[pl.ds(r0, bp), im_sl] = nim
            return nre, nim

        xre, xim = lax.fori_loop(0, steps, step, (st_ref[:, re_sl], st_ref[:, im_sl]), unroll=2)
        st_ref[:, re_sl] = xre
        st_ref[:, im_sl] = xim
    y_ref[0] = jnp.dot(bu_ref[...].astype(BF), cbd_ref[0], preferred_element_type=F32)


def _s5_params(a_re, a_im, log_step, b_re, b_im, c_re, c_im):
    a_re, a_im = a_re.astype(F32), a_im.astype(F32)
    step = jnp.exp(log_step.astype(F32))[..., None]
    mag = jnp.exp(a_re * step)
    lre, lim = mag * jnp.cos(a_im * step), mag * jnp.sin(a_im * step)
    den = a_re * a_re + a_im * a_im
    kre = ((lre - 1.0) * a_re + lim * a_im) / den
    kim = (lim * a_re - (lre - 1.0) * a_im) / den
    bre = kre[..., None] * b_re - kim[..., None] * b_im
    bim = kre[..., None] * b_im + kim[..., None] * b_re
    eye = jnp.eye(S5_GROUPS, dtype=F32)

    def in_blockdiag(m):
        return jnp.einsum('dgpn,gh->dgnhp', m, eye).reshape(2, S5_WIDTH, S5_NP)

    def out_blockdiag(m):
        return jnp.einsum('dgnp,gh->dgphn', m, eye).reshape(2, S5_NP, S5_WIDTH)

    bbd = jnp.concatenate([in_blockdiag(bre), in_blockdiag(bim)], axis=2).astype(BF)
    cbd = jnp.concatenate([out_blockdiag(c_re.astype(F32)), -out_blockdiag(c_im.astype(F32))], axis=1).astype(BF)
    return bbd, lre.reshape(2, 1, S5_NP), lim.reshape(2, 1, S5_NP), cbd


def _s5_scan(u_seq, n_ctx_rows, bbd, lre, lim, cbd, bp):
    rows = u_seq.shape[0]
    steps = 64
    rb = steps * bp
    assert rows % rb == 0 and n_ctx_rows % rb == 0
    nc = n_ctx_rows // rb
    nl = rows // rb - nc

    def in_block(d, i):
        rev = jnp.where(i < nc, nc - 1 - i, nc + nl - 1 - (i - nc))
        return jnp.where(d == 0, i, rev)

    def out_block(d, i):
        k = jnp.maximum(i - nc, 0)
        return jnp.where(d == 0, k, nl - 1 - k)

    per_dir = lambda shape: pl.BlockSpec((1,) + shape, lambda d, i: (d, 0, 0))
    return pl.pallas_call(
        functools.partial(_s5_kernel, steps=steps, bp=bp),
        grid=(2, nc + nl),
        in_specs=[pl.BlockSpec((rb, S5_WIDTH), lambda d, i: (in_block(d, i), 0)), per_dir((S5_WIDTH, 2 * S5_NP)),
                  per_dir((1, S5_NP)), per_dir((1, S5_NP)), per_dir((2 * S5_NP, S5_WIDTH))],
        out_specs=pl.BlockSpec((1, rb, S5_WIDTH), lambda d, i: (d, out_block(d, i), 0)),
        out_shape=jax.ShapeDtypeStruct((2, nl * rb, S5_WIDTH), F32),
        scratch_shapes=[pltpu.VMEM((rb, 2 * S5_NP), F32), pltpu.VMEM((bp, 2 * S5_NP), F32)],
        compiler_params=_cparams(("arbitrary", "arbitrary")),
        name="s5_scan",
    )(u_seq, bbd, lre, lim, cbd)


def _s5_mixer_latent(u_lat, u_ctx, b, params):
    n, c = u_lat.shape[0], u_ctx.shape[0]
    bp = -(-b // 8) * 8
    seq = jnp.concatenate([u_ctx, u_lat], axis=0).reshape(c + n, b, S5_WIDTH)
    seq = jnp.pad(seq, ((0, 0), (0, bp - b), (0, 0))).reshape((c + n) * bp, S5_WIDTH)
    y = _s5_scan(seq, c * bp, *params, bp)
    return (y[0] + y[1]).reshape(n, bp, S5_WIDTH)[:, :b].reshape(n, b * S5_WIDTH)


def _out_odd_kernel(x_ref, oa_ref, ys_ref, u_ref, mod_ref, ng_ref, d_ref, gw_ref, gb_ref, woa_ref, wob_ref,
                    rw_ref, rb_ref, x1_ref, h_ref, gate_ref):
    y = ys_ref[...] + d_ref[...] * u_ref[...].astype(F32)
    gl = 0.5 * y * (1.0 + jnp.tanh(np.sqrt(2.0 / np.pi) * (y + 0.044715 * (y * y * y))))
    z = gl * jax.nn.sigmoid(jnp.dot(gl.astype(BF), gw_ref[...], preferred_element_type=F32) + gb_ref[...])
    mix = (jnp.dot(oa_ref[0], woa_ref[...], preferred_element_type=F32)
           + jnp.dot(z.astype(BF), wob_ref[...], preferred_element_type=F32))
    x1 = x_ref[0] + mod_ref[0, 2:3, :] * mix
    x1_ref[0] = x1
    h = _norm_mod(x1, ng_ref[...], mod_ref[0, 3:4, :], mod_ref[0, 4:5, :]).astype(BF)
    h_ref[0] = h
    logits = jnp.dot(h, rw_ref[...], preferred_element_type=F32) + rb_ref[...]
    lane = lax.broadcasted_iota(jnp.int32, logits.shape, 1)
    m1 = jnp.max(logits, axis=-1, keepdims=True)
    i1 = jnp.min(jnp.where(logits == m1, lane, LANES), axis=-1, keepdims=True)
    rest = jnp.where(lane == i1, -jnp.inf, logits)
    m2 = jnp.max(rest, axis=-1, keepdims=True)
    i2 = jnp.min(jnp.where(rest == m2, lane, LANES), axis=-1, keepdims=True)
    e2 = jnp.exp(m2 - m1)
    gate_ref[0] = jnp.where(lane == i1, 1.0 / (1.0 + e2), 0.0) + jnp.where(lane == i2, e2 / (1.0 + e2), 0.0)


def _out_odd(x, oa, ys, u, mods, norm_gain, d_skip, glu_w, glu_b, w_out, router_w, router_b):
    b, n, d = x.shape
    tm = _token_tile(n)
    wa = oa.shape[2]
    tok = lambda width: pl.BlockSpec((1, tm, width), lambda bi, i: (bi, i, 0))
    vec = lambda width: _const_spec((1, width))
    tmajor = pl.BlockSpec((tm, S5_WIDTH), lambda bi, i: (i, bi))
    return pl.pallas_call(
        _out_odd_kernel,
        grid=(b, n // tm),
        in_specs=[tok(d), tok(wa), tmajor, tmajor, _mod_spec(d, None), vec(d), vec(S5_WIDTH),
                  _const_spec(glu_w.shape), vec(S5_WIDTH), _const_spec((wa, d)), _const_spec((S5_WIDTH, d)),
                  _const_spec(router_w.shape), vec(LANES)],
        out_specs=[tok(d), tok(d), tok(LANES)],
        out_shape=[jax.ShapeDtypeStruct((b, n, d), F32), jax.ShapeDtypeStruct((b, n, d), BF),
                   jax.ShapeDtypeStruct((b, n, LANES), F32)],
        compiler_params=_cparams(("parallel", "parallel")),
        name="out_odd",
    )(x, oa, ys, u, mods, norm_gain, d_skip, glu_w, glu_b, w_out[:wa], w_out[wa:], router_w, router_b)


MOE_CHUNK = 512
MOE_ROWS = 256
MOE_F_CHUNKS = 4
MOE_ALIGN = 16


def _moe_capacity(tb):
    need = 2 * tb + N_EXPERTS * MOE_ALIGN + MOE_ROWS
    return -(-need // MOE_CHUNK) * MOE_CHUNK


def _moe_kernel(h_ref, gate_ref, x1_ref, mod_ref, w13_ref, w2_ref, o_ref, xy_ref, gp_ref, tok_ref, tokt_ref,
                cnt_ref, seg_ref):
    e = pl.program_id(1)
    q = pl.program_id(2)
    last_e = pl.num_programs(1) - 1
    last_q = pl.num_programs(2) - 1
    tb, d = h_ref.shape[1:]
    cap = xy_ref.shape[0]
    ch = MOE_CHUNK

    @pl.when((e == 0) & (q == 0))
    def _pack():
        o_ref[...] = jnp.zeros_like(o_ref)
        lane1 = lax.broadcasted_iota(jnp.int32, (1, LANES), 1)
        total = jnp.zeros((1, LANES), F32)
        for c in range(tb // ch):
            total = total + jnp.sum((gate_ref[0, c * ch:(c + 1) * ch, :] > 0.0).astype(F32), axis=0, keepdims=True)
        seg_vec = jnp.zeros((1, LANES), F32)
        start = jnp.int32(0)
        for k in range(N_EXPERTS):
            nk = jnp.sum(jnp.where(lane1 == k, total, 0.0)).astype(jnp.int32)
            cnt_ref[k] = nk
            seg_ref[k] = start
            seg_vec = jnp.where(lane1 == k, start.astype(F32), seg_vec)
            start = start + (nk + MOE_ALIGN - 1) // MOE_ALIGN * MOE_ALIGN
        tri = (lax.broadcasted_iota(jnp.int32, (ch, ch), 0)
               > lax.broadcasted_iota(jnp.int32, (ch, ch), 1)).astype(BF)
        lane = lax.broadcasted_iota(jnp.int32, (ch, LANES), 1)
        offset = jnp.zeros((1, LANES), F32)
        for c in range(tb // ch):
            g = gate_ref[0, c * ch:(c + 1) * ch, :]
            sel = g > 0.0
            sel_f = sel.astype(F32)
            rank = jnp.dot(tri, sel_f.astype(BF), preferred_element_type=F32) + offset
            pos = jnp.where(sel, rank + seg_vec, -1.0)
            pmax = jnp.max(pos, axis=-1, keepdims=True)
            pmin = jnp.min(jnp.where(sel, pos, 1e9), axis=-1, keepdims=True)
            gmin = jnp.sum(jnp.where(pos == pmin, g, 0.0), axis=-1, keepdims=True)
            gmax = jnp.sum(jnp.where(pos == pmax, g, 0.0), axis=-1, keepdims=True)
            pmax = jnp.where(pmax == pmin, -1.0, pmax)
            tok_ref[c * ch:(c + 1) * ch, :] = jnp.where(
                lane == 0, pmin, jnp.where(lane == 1, pmax, jnp.where(lane == 2, gmin, jnp.where(lane == 3, gmax, 0.0))))
            offset = offset + jnp.sum(sel_f, axis=0, keepdims=True)
        tokt_ref[...] = tok_ref[...].T
        pmin_t, pmax_t = tokt_ref[0:1, :], tokt_ref[1:2, :]
        row_id = lax.broadcasted_iota(jnp.int32, (ch, tb), 0).astype(F32)

        def pack_chunk(c, carry):
            r0 = pl.multiple_of(c * ch, ch)
            rid = row_id + r0.astype(F32)
            is_min, is_max = rid == pmin_t, rid == pmax_t
            onehot = jnp.where(is_min, 1.0, jnp.where(is_max, 1.0, 0.0)).astype(BF)
            xy_ref[pl.ds(r0, ch), :] = jnp.dot(onehot, h_ref[0], preferred_element_type=F32).astype(BF)
            gp_ref[pl.ds(r0, ch), :] = jnp.sum(
                jnp.where(is_min, tokt_ref[2:3, :], jnp.where(is_max, tokt_ref[3:4, :], 0.0)), axis=-1, keepdims=True)
            return carry

        lax.fori_loop(0, cap // ch, pack_chunk, 0)

    n = cnt_ref[e]
    seg = seg_ref[e]

    def ffn_tiles(starts, size):
        srcs = [pl.multiple_of(seg + r0, MOE_ALIGN) for r0 in starts]
        xs = [xy_ref[pl.ds(src, size), :] for src in srcs]
        fc = w2_ref.shape[1]
        proj = [jnp.dot(x, w13_ref[0], preferred_element_type=F32) for x in xs]
        ys = [jnp.dot((_silu(ag[:, :fc]) * ag[:, fc:]).astype(BF), w2_ref[0], preferred_element_type=F32)
              for ag in proj]

        sums = []
        for r0, y in zip(starts, ys):
            total = jnp.where(q == 0, y, o_ref[0, pl.ds(r0, size), :] + y)
            o_ref[0, pl.ds(r0, size), :] = total
            sums.append(total)

        @pl.when(q == last_q)
        def _():
            for r0, src, x, total in zip(starts, srcs, xs, sums):
                mine = lax.broadcasted_iota(jnp.int32, (size, 1), 0) < n - r0
                xy_ref[pl.ds(src, size), :] = jnp.where(mine, (gp_ref[pl.ds(src, size), :] * total).astype(BF), x)

    rows = MOE_ROWS
    n_full = n // rows
    tail = n - n_full * rows
    n_tiles = n_full + (tail > rows // 2).astype(jnp.int32)
    n_pairs = n_tiles // 2

    def tile_pair(i, carry):
        r0 = pl.multiple_of(i * 2 * rows, 2 * rows)
        ffn_tiles([r0, pl.multiple_of(r0 + rows, rows)], rows)
        return carry

    lax.fori_loop(0, n_pairs, tile_pair, 0)

    @pl.when(n_tiles > 2 * n_pairs)
    def _():
        ffn_tiles([pl.multiple_of(n_pairs * 2 * rows, rows)], rows)

    @pl.when((tail > 0) & (tail <= rows // 2))
    def _():
        ffn_tiles([pl.multiple_of(n_full * rows, rows)], rows // 2)

    @pl.when((e == last_e) & (q == last_q))
    def _combine():
        used = min(cap, -(-(2 * tb + N_EXPERTS * MOE_ALIGN) // MXU_TILE) * MXU_TILE)
        col_id = lax.broadcasted_iota(jnp.int32, (ch, used), 1).astype(F32)

        def combine_chunk(c, carry):
            tok_rows = pl.ds(pl.multiple_of(c * ch, ch), ch)
            pmin, pmax = tok_ref[tok_rows, 0:1], tok_ref[tok_rows, 1:2]
            onehot = jnp.where(col_id == pmin, 1.0, jnp.where(col_id == pmax, 1.0, 0.0)).astype(BF)
            mix = jnp.dot(onehot, xy_ref[0:used, :], preferred_element_type=F32)
            o_ref[0, tok_rows, :] = x1_ref[0, tok_rows, :] + mod_ref[0, 5:6, :] * mix
            return carry

        lax.fori_loop(0, tb // ch, combine_chunk, 0)


def _moe(h, gates, x1, mods, w1, w3, w2):
    b, n, d = h.shape
    n_exp, _, f = w1.shape
    assert n_exp == N_EXPERTS and n % MOE_CHUNK == 0 and MOE_F_CHUNKS > 1
    fc = f // MOE_F_CHUNKS
    w13 = jnp.concatenate([w[:, :, c * fc:(c + 1) * fc].astype(BF) for c in range(MOE_F_CHUNKS) for w in (w1, w3)],
                          axis=2)
    per_batch = lambda width: pl.BlockSpec((1, n, width), lambda bi, e, q: (bi, 0, 0), pipeline_mode=pl.Buffered(1))
    return pl.pallas_call(
        _moe_kernel,
        grid=(b, n_exp, MOE_F_CHUNKS),
        in_specs=[per_batch(d), per_batch(LANES), per_batch(d),
                  pl.BlockSpec((1, N_MOD, d), lambda bi, e, q: (bi, 0, 0)),
                  pl.BlockSpec((1, d, 2 * fc), lambda bi, e, q: (e, 0, q)),
                  pl.BlockSpec((1, fc, d), lambda bi, e, q: (e, q, 0))],
        out_specs=per_batch(d),
        out_shape=jax.ShapeDtypeStruct((b, n, d), F32),
        scratch_shapes=[pltpu.VMEM((_moe_capacity(n), d), BF), pltpu.VMEM((_moe_capacity(n), 1), F32),
                        pltpu.VMEM((n, LANES), F32),
                        pltpu.VMEM((LANES, n), F32), pltpu.SMEM((N_EXPERTS,), jnp.int32),
                        pltpu.SMEM((N_EXPERTS,), jnp.int32)],
        compiler_params=_cparams(("arbitrary", "arbitrary", "arbitrary")),
        name="moe",
    )(h, gates, x1, mods, w13, w2)


def _head_blockdiag(width):
    head = np.arange(width) // HEAD_DIM
    return jnp.asarray((head[:, None] == head[None, :]).astype(np.float32) / HEAD_DIM, dtype=BF)


def _tile_gain(gain, width, scale=1.0):
    return (jnp.tile(gain.astype(F32), width // HEAD_DIM) * scale).reshape(1, width)


def _even_layer(x, xc, mods, ctx_row, norm_mix, norm_ffn, w_in, q_gain, k_gain, rpb, w_out, w1, w3, w2):
    d = x.shape[2]
    bd = _head_blockdiag(NA_WIDTH)
    w_in = w_in.astype(BF)
    qg = _tile_gain(q_gain, NA_WIDTH, Q_SCALE)
    kg = _tile_gain(k_gain, NA_WIDTH)
    ng_mix, ng_ffn = norm_mix.reshape(1, d), norm_ffn.reshape(1, d)
    u, q, k, v = _in_even(x, mods, None, ng_mix, w_in, qg, kg, bd)
    uc, qc, kc, vc = _in_even(xc, mods, ctx_row, ng_mix, w_in, qg, kg, bd)
    o_na = _neighbourhood_attention(q, k, v, kc, vc, rpb)
    o_ctx = _context_attention(qc, kc, vc)
    w_out, w1, w3, w2 = (t.astype(BF) for t in (w_out, w1, w3, w2))
    x = _out_ffn_even(x, _fourier(u), o_na, mods, None, ng_ffn, w_out, w1, w3, w2)
    xc = _out_ffn_even(xc, _fourier(uc), o_ctx, mods, ctx_row, ng_ffn, w_out, w1, w3, w2)
    return x, xc


def _odd_layer(x, xc, mods, ctx_row, norm_mix, norm_ffn, w_in, q_gain, k_gain, s5, d_skip, glu_w, glu_b, w_out,
               router_w, router_b, ew1, ew3, ew2):
    b, n, d = x.shape
    kvw = GKV_HEADS * HEAD_DIM
    wq, wk, wv, wu = jnp.split(w_in, [GQ_WIDTH, GQ_WIDTH + kvw, GQ_WIDTH + 2 * kvw], axis=1)

    def twice(w):
        return jnp.repeat(w.reshape(d, GKV_HEADS, 1, HEAD_DIM), 2, axis=2).reshape(d, GKV_PAD)

    w_lat = jnp.concatenate([wq, twice(wk), twice(wv), wu], axis=1).astype(BF)
    w_ctx = jnp.concatenate([twice(wk), twice(wv), wu], axis=1).astype(BF)
    qg = _tile_gain(q_gain, GQ_WIDTH, Q_SCALE)
    kg = _tile_gain(k_gain, GKV_PAD)
    bdq, bdk = _head_blockdiag(GQ_WIDTH), _head_blockdiag(GKV_PAD)
    cos, sin_signed = _rope_tables(n)
    ng_mix, ng_ffn = norm_mix.reshape(1, d), norm_ffn.reshape(1, d)
    q, kk, vv, u = _in_odd(x, mods, None, ng_mix, w_lat, qg, kg, bdq, bdk, cos, sin_signed, True)
    kkc, vvc, uc = _in_odd(xc, mods, ctx_row, ng_mix, w_ctx, None, kg, None, bdk, None, None, False)
    vvt = jnp.concatenate([vvc, vv], axis=1).transpose(0, 2, 1)
    dup_rows = (jnp.arange(GKV_PAD) % LANES >= HEAD_DIM)[None, :, None]
    vvt = jnp.where(dup_rows, jnp.ones_like(vvt), vvt)
    o_attn = _gqa(q, jnp.concatenate([kkc, kk], axis=1), vvt)
    y_ssm = _s5_mixer_latent(u, uc, b, _s5_params(*s5))
    rw = jnp.pad(router_w, ((0, 0), (0, LANES - N_EXPERTS))).astype(BF)
    rb = jnp.pad(router_b.astype(F32), (0, LANES - N_EXPERTS), constant_values=NEG_INF).reshape(1, LANES)
    x1, h, gates = _out_odd(x, o_attn, y_ssm, u, mods, ng_ffn, d_skip.reshape(1, S5_WIDTH).astype(F32),
                            glu_w.astype(BF), glu_b.reshape(1, S5_WIDTH).astype(F32), w_out.astype(BF), rw, rb)
    return _moe(h, gates, x1, mods, ew1, ew3, ew2.astype(BF))


def kernel(x, c, ctx, c_ctx, ev_mod_w, ev_mod_b, ev_norm_mix, ev_norm_ffn, ev_w_in, ev_q_gain, ev_k_gain, ev_rpb,
           ev_w_out, ev_ffn_w1, ev_ffn_w3, ev_ffn_w2, od_mod_w, od_mod_b, od_norm_mix, od_norm_ffn, od_w_in,
           od_q_gain, od_k_gain, od_s5_a_re, od_s5_a_im, od_s5_log_step, od_s5_b_re, od_s5_b_im, od_s5_c_re,
           od_s5_c_im, od_s5_d, od_s5_glu_w, od_s5_glu_b, od_w_out, od_router_w, od_router_b, od_exp_w1, od_exp_w3,
           od_exp_w2):
    assert ev_mod_w.shape[0] == 1 and od_mod_w.shape[0] == 1, "one even and one odd layer"
    b, n, d = x.shape
    rows = -(-(b + 1) // 8) * 8
    cond = jnp.zeros((rows, d), F32).at[:b].set(c).at[b].set(c_ctx)
    mods_even = _modulation(cond, ev_mod_w[0], ev_mod_b[0])
    mods_odd = _modulation(cond, od_mod_w[0], od_mod_b[0])
    x, xc = _even_layer(x, ctx, mods_even, b, ev_norm_mix[0], ev_norm_ffn[0], ev_w_in[0], ev_q_gain[0], ev_k_gain[0],
                        ev_rpb[0], ev_w_out[0], ev_ffn_w1[0], ev_ffn_w3[0], ev_ffn_w2[0])
    s5 = (od_s5_a_re[0], od_s5_a_im[0], od_s5_log_step[0], od_s5_b_re[0], od_s5_b_im[0], od_s5_c_re[0], od_s5_c_im[0])
    return _odd_layer(x, xc, mods_odd, b, od_norm_mix[0], od_norm_ffn[0], od_w_in[0], od_q_gain[0], od_k_gain[0],
                      s5, od_s5_d[0], od_s5_glu_w[0], od_s5_glu_b[0], od_w_out[0], od_router_w[0], od_router_b[0],
                      od_exp_w1[0], od_exp_w3[0], od_exp_w2[0])
```
